```python
import jax, jax.numpy as jnp
from jax import lax
import numpy as np


D_MODEL = 2048
BATCH = 4
SEQ = 4096
DEPTH = 4

D_MIX = D_MODEL
CHUNK = 128
SGU_WIDTH = D_MIX // 4
SGU_GROUPS = 4
SGU_GROUP_DIM = SGU_WIDTH // SGU_GROUPS
HGRN_WIDTH = D_MIX // 4
HGRN_HEADS = 4
HGRN_HEAD_DIM = HGRN_WIDTH // HGRN_HEADS
MLA_HEADS = 8
MLA_V_DIM = (D_MIX - SGU_WIDTH - HGRN_WIDTH) // MLA_HEADS
MLA_NOPE_DIM = 128
MLA_ROPE_DIM = 64
MLA_QK_DIM = MLA_NOPE_DIM + MLA_ROPE_DIM
Q_LORA_RANK = D_MODEL // 4
KV_LORA_RANK = D_MODEL // 4
ROPE_THETA = 10000.0
ATTN_BLOCK = 128
N_GROUPS = 4
EXPERTS_PER_GROUP = 8
N_EXPERTS = N_GROUPS * EXPERTS_PER_GROUP
TOP_K = 2
D_EXPERT = D_MODEL // 4
MOE_BLOCK = 128
NORM_EPS = 1e-5
DEEPNORM_ALPHA = (2 * DEPTH) ** 0.25
DEEPNORM_BETA = (8 * DEPTH) ** -0.25
IN_SPLITS = (SGU_WIDTH, SGU_WIDTH, HGRN_WIDTH, HGRN_WIDTH, HGRN_WIDTH, HGRN_WIDTH, Q_LORA_RANK, KV_LORA_RANK, MLA_ROPE_DIM)
D_IN = SGU_WIDTH * 2 + HGRN_WIDTH * 4 + Q_LORA_RANK + KV_LORA_RANK + MLA_ROPE_DIM

kernel_name = 'hybrid_sgu_hgrn2_mla_hmoe_deepnorm'


def _split_points():
    pts, acc = [], 0
    for w in IN_SPLITS[:-1]:
        acc += w
        pts.append(acc)
    return pts


def layer_norm(x, g, b):
    xf = x.astype(jnp.float32)
    mu = jnp.mean(xf, axis=-1, keepdims=True)
    var = jnp.mean(jnp.square(xf - mu), axis=-1, keepdims=True)
    return ((xf - mu) * lax.rsqrt(var + NORM_EPS) * g.astype(jnp.float32) + b.astype(jnp.float32)).astype(x.dtype)


def rms_norm(x, g):
    xf = x.astype(jnp.float32)
    ms = jnp.mean(jnp.square(xf), axis=-1, keepdims=True)
    return (xf * lax.rsqrt(ms + NORM_EPS) * g.astype(jnp.float32)).astype(x.dtype)


def rope_tables(positions):
    inv = 1.0 / (ROPE_THETA ** (jnp.arange(0, MLA_ROPE_DIM, 2, dtype=jnp.float32) / MLA_ROPE_DIM))
    ang = positions.astype(jnp.float32)[..., None] * inv
    return jnp.cos(ang), jnp.sin(ang)


def apply_rope(x, cos, sin):
    x1, x2 = jnp.split(x.astype(jnp.float32), 2, axis=-1)
    return jnp.concatenate([x1 * cos - x2 * sin, x2 * cos + x1 * sin], axis=-1).astype(x.dtype)


def chunked_spatial_gating(u, v, ln_g, ln_b, ws, bs):
    B, S, _ = u.shape
    nc = S // CHUNK
    v = layer_norm(v, ln_g, ln_b).reshape(B, nc, CHUNK, SGU_GROUPS, SGU_GROUP_DIM)
    causal = jnp.tril(jnp.ones((CHUNK, CHUNK), dtype=bool))
    ws = jnp.where(causal[None], ws, jnp.zeros_like(ws))
    mixed = jnp.einsum('gts,bcsgd->bctgd', ws, v) + bs.T[:, :, None]
    return u * mixed.reshape(B, S, SGU_WIDTH).astype(u.dtype)


def _to_chunks(t):
    B, S, H, D = t.shape
    return t.reshape(B, S // CHUNK, CHUNK, H, D).transpose(1, 0, 3, 2, 4)


def gated_chunk_recurrence(q, k, v, log_f):
    B, S, H, K = q.shape
    V = v.shape[-1]
    causal = jnp.tril(jnp.ones((CHUNK, CHUNK), dtype=bool))

    def step(state, inp):
        qc, kc, vc, gc = inp
        G = jnp.cumsum(gc, axis=2)
        diff = G[:, :, :, None, :] - G[:, :, None, :, :]
        decay = jnp.exp(jnp.where(causal[None, None, :, :, None], diff, -jnp.inf))
        A = jnp.einsum('bhtk,bhtsk,bhsk->bhts', qc, decay, kc)
        o = jnp.einsum('bhts,bhsv->bhtv', A, vc) + jnp.einsum('bhtk,bhkv->bhtv', qc * jnp.exp(G), state)
        G_last = G[:, :, -1:, :]
        new_state = jnp.exp(G_last[:, :, 0, :])[..., None] * state + jnp.einsum('bhsk,bhsv->bhkv', kc * jnp.exp(G_last - G), vc)
        return new_state, o

    init = jnp.zeros((B, H, K, V), jnp.float32)
    _, o = lax.scan(step, init, (_to_chunks(q), _to_chunks(k), _to_chunks(v), _to_chunks(log_f)))
    return o.transpose(1, 0, 3, 2, 4).reshape(B, S, H, V)


def hgrn2_mixer(q_in, f_in, i_in, g_in, lb, norm_g):
    B, S, _ = q_in.shape
    H, K = HGRN_HEADS, HGRN_HEAD_DIM
    q = jax.nn.silu(q_in.astype(jnp.float32))
    fx = f_in.astype(jnp.float32)
    lb = lb.astype(jnp.float32)
    log_f = jnp.logaddexp(jnp.log(lb), jnp.log1p(-lb) + jax.nn.log_sigmoid(fx))
    k = (1.0 - lb) * jax.nn.sigmoid(-fx)
    v = i_in.astype(jnp.float32)
    heads = lambda t: t.reshape(B, S, H, K)
    o = gated_chunk_recurrence(heads(q), heads(k), heads(v), heads(log_f))
    o = rms_norm(o, norm_g.reshape(H, K)).reshape(B, S, HGRN_WIDTH)
    return (o * jax.nn.silu(g_in.astype(jnp.float32))).astype(q_in.dtype)


def causal_block_attention(q, k, v):
    B, S, H, DQK = q.shape
    nq = S // ATTN_BLOCK
    scale = DQK ** -0.5
    qb = q.reshape(B, nq, ATTN_BLOCK, H, DQK).transpose(1, 0, 2, 3, 4)
    key_pos = jnp.arange(S)

    def one_block(args):
        q_blk, blk = args
        s = jnp.einsum('bqhd,bkhd->bhqk', q_blk, k).astype(jnp.float32) * scale
        q_pos = blk * ATTN_BLOCK + jnp.arange(ATTN_BLOCK)
        s = jnp.where(key_pos[None, :] <= q_pos[:, None], s, -jnp.inf)
        p = jax.nn.softmax(s, axis=-1).astype(v.dtype)
        return jnp.einsum('bhqk,bkhd->bqhd', p, v)

    o = lax.map(one_block, (qb, jnp.arange(nq)))
    return o.transpose(1, 0, 2, 3, 4).reshape(B, S, H, v.shape[-1])


def mla_mixer(c_q, c_kv, k_rope, qn_g, w_uq, kvn_g, w_ukv, cos, sin):
    B, S, _ = c_q.shape
    q = (rms_norm(c_q, qn_g) @ w_uq).reshape(B, S, MLA_HEADS, MLA_QK_DIM)
    q_nope, q_rope = q[..., :MLA_NOPE_DIM], q[..., MLA_NOPE_DIM:]
    q_rope = apply_rope(q_rope, cos[:, :, None, :], sin[:, :, None, :])
    kv = (rms_norm(c_kv, kvn_g) @ w_ukv).reshape(B, S, MLA_HEADS, MLA_NOPE_DIM + MLA_V_DIM)
    k_nope, v = kv[..., :MLA_NOPE_DIM], kv[..., MLA_NOPE_DIM:]
    k_rope = apply_rope(k_rope, cos, sin)
    q = jnp.concatenate([q_nope, q_rope], axis=-1)
    k = jnp.concatenate([k_nope, jnp.broadcast_to(k_rope[:, :, None, :], (B, S, MLA_HEADS, MLA_ROPE_DIM))], axis=-1)
    o = causal_block_attention(q, k, v)
    return o.reshape(B, S, MLA_HEADS * MLA_V_DIM)


def hierarchical_moe(x, wg_r, bg_r, we_r, be_r, w_gate, w_up, w_down):
    B, S, D = x.shape
    N = B * S
    xt = x.reshape(N, D)
    group_prob = jax.nn.softmax((xt @ wg_r + bg_r).astype(jnp.float32), axis=-1)
    g_val, g_idx = lax.top_k(group_prob, 1)
    exp_logits = (xt @ we_r + be_r).astype(jnp.float32).reshape(N, N_GROUPS, EXPERTS_PER_GROUP)
    in_group = exp_logits[jnp.arange(N), g_idx[:, 0]]
    e_val, e_idx = lax.top_k(in_group, TOP_K)
    gates = g_val * jax.nn.softmax(e_val, axis=-1)
    experts = g_idx * EXPERTS_PER_GROUP + e_idx
    NK = N * TOP_K
    flat_e = experts.reshape(NK)
    flat_tok = jnp.repeat(jnp.arange(N, dtype=jnp.int32), TOP_K)
    flat_g = gates.reshape(NK)
    order = jnp.argsort(flat_e)
    se = flat_e[order]
    counts = jnp.bincount(flat_e, length=N_EXPERTS)
    padded = ((counts + MOE_BLOCK - 1) // MOE_BLOCK) * MOE_BLOCK
    pad_end = jnp.cumsum(padded)
    pad_start = pad_end - padded
    raw_start = jnp.cumsum(counts) - counts
    dest = pad_start[se] + jnp.arange(NK) - raw_start[se]
    P = ((NK + MOE_BLOCK - 1) // MOE_BLOCK) * MOE_BLOCK + N_EXPERTS * MOE_BLOCK
    nb = P // MOE_BLOCK
    slot_tok = jnp.full((P,), N, dtype=jnp.int32).at[dest].set(flat_tok[order])
    slot_gate = jnp.zeros((P,), jnp.float32).at[dest].set(flat_g[order])
    blk_exp = jnp.minimum(jnp.searchsorted(pad_end, jnp.arange(nb) * MOE_BLOCK, side='right'), N_EXPERTS - 1)
    x_pad = jnp.concatenate([xt, jnp.zeros((1, D), xt.dtype)], axis=0)
    xs = x_pad[slot_tok].reshape(nb, MOE_BLOCK, D)

    def run_block(args):
        xb, e = args
        h = jax.nn.silu(xb @ w_gate[e]) * (xb @ w_up[e])
        return h @ w_down[e]

    ys = lax.map(run_block, (xs, blk_exp)).reshape(P, D)
    out = jnp.zeros((N + 1, D), x.dtype).at[slot_tok].add(ys * slot_gate[:, None].astype(ys.dtype))
    return out[:N].reshape(B, S, D)


def setup_inputs(seed: int = 0) -> dict:
    key = jax.random.key(seed)
    ks = jax.random.split(key, 26)
    f32 = jnp.float32
    L = DEPTH

    def nrm(k, shape, scale):
        return jax.random.normal(k, shape, f32) * scale

    x = nrm(ks[0], (BATCH, SEQ, D_MODEL), 1.0)
    offset = jax.random.randint(ks[1], (BATCH, 1), 0, 1024, dtype=jnp.int32)
    positions = offset + jnp.arange(SEQ, dtype=jnp.int32)[None, :]
    return {
        'x': x,
        'positions': positions,
        'w_in': nrm(ks[2], (L, D_MODEL, D_IN), D_MODEL ** -0.5),
        'sgu_ln_g': 1.0 + nrm(ks[3], (L, SGU_WIDTH), 0.02),
        'sgu_ln_b': nrm(ks[4], (L, SGU_WIDTH), 0.02),
        'sgu_ws': nrm(ks[5], (L, SGU_GROUPS, CHUNK, CHUNK), CHUNK ** -0.5),
        'sgu_b': 1.0 + nrm(ks[6], (L, SGU_GROUPS, CHUNK), 0.1),
        'hgrn_lb_logits': nrm(ks[7], (L, HGRN_WIDTH), 1.0),
        'hgrn_norm_g': 1.0 + nrm(ks[8], (L, HGRN_WIDTH), 0.02),
        'mla_qn_g': 1.0 + nrm(ks[9], (L, Q_LORA_RANK), 0.02),
        'mla_w_uq': nrm(ks[10], (L, Q_LORA_RANK, MLA_HEADS * MLA_QK_DIM), Q_LORA_RANK ** -0.5),
        'mla_kvn_g': 1.0 + nrm(ks[11], (L, KV_LORA_RANK), 0.02),
        'mla_w_ukv': nrm(ks[12], (L, KV_LORA_RANK, MLA_HEADS * (MLA_NOPE_DIM + MLA_V_DIM)), KV_LORA_RANK ** -0.5),
        'w_out': nrm(ks[13], (L, D_MIX, D_MODEL), DEEPNORM_BETA * D_MIX ** -0.5),
        'ln1_g': 1.0 + nrm(ks[14], (L, D_MODEL), 0.02),
        'ln1_b': nrm(ks[15], (L, D_MODEL), 0.02),
        'router_group_w': nrm(ks[16], (L, D_MODEL, N_GROUPS), D_MODEL ** -0.5),
        'router_group_b': nrm(ks[17], (L, N_GROUPS), 0.01),
        'router_expert_w': nrm(ks[18], (L, D_MODEL, N_EXPERTS), D_MODEL ** -0.5),
        'router_expert_b': nrm(ks[19], (L, N_EXPERTS), 0.01),
        'expert_w_gate': nrm(ks[20], (L, N_EXPERTS, D_MODEL, D_EXPERT), D_MODEL ** -0.5),
        'expert_w_up': nrm(ks[21], (L, N_EXPERTS, D_MODEL, D_EXPERT), DEEPNORM_BETA * D_MODEL ** -0.5),
        'expert_w_down': nrm(ks[22], (L, N_EXPERTS, D_EXPERT, D_MODEL), DEEPNORM_BETA * D_EXPERT ** -0.5),
        'ln2_g': 1.0 + nrm(ks[23], (L, D_MODEL), 0.02),
        'ln2_b': nrm(ks[24], (L, D_MODEL), 0.02),
    }


def reference(x, positions, w_in, sgu_ln_g, sgu_ln_b, sgu_ws, sgu_b, hgrn_lb_logits, hgrn_norm_g, mla_qn_g, mla_w_uq, mla_kvn_g, mla_w_ukv, w_out, ln1_g, ln1_b, router_group_w, router_group_b, router_expert_w, router_expert_b, expert_w_gate, expert_w_up, expert_w_down, ln2_g, ln2_b):
    cos, sin = rope_tables(positions)
    lb_cum = jnp.cumsum(jax.nn.softmax(hgrn_lb_logits.astype(jnp.float32), axis=0), axis=0)
    lower_bounds = lb_cum - lb_cum[0:1]
    split_points = _split_points()
    for l in range(DEPTH):
        h = x @ w_in[l]
        a_u, a_v, b_q, b_f, b_i, b_g, c_q, c_kv, c_kr = jnp.split(h, split_points, axis=-1)
        y_a = chunked_spatial_gating(jax.nn.gelu(a_u), jax.nn.gelu(a_v), sgu_ln_g[l], sgu_ln_b[l], sgu_ws[l], sgu_b[l])
        y_b = hgrn2_mixer(b_q, b_f, b_i, b_g, lower_bounds[l], hgrn_norm_g[l])
        y_c = mla_mixer(c_q, c_kv, c_kr, mla_qn_g[l], mla_w_uq[l], mla_kvn_g[l], mla_w_ukv[l], cos, sin)
        mix = jnp.concatenate([y_a, y_b.astype(y_a.dtype), y_c.astype(y_a.dtype)], axis=-1) @ w_out[l]
        x = layer_norm(DEEPNORM_ALPHA * x + mix, ln1_g[l], ln1_b[l])
        ffn = hierarchical_moe(x, router_group_w[l], router_group_b[l], router_expert_w[l], router_expert_b[l], expert_w_gate[l], expert_w_up[l], expert_w_down[l])
        x = layer_norm(DEEPNORM_ALPHA * x + ffn, ln2_g[l], ln2_b[l])
    return x
```

```python
import functools

import jax
import jax.numpy as jnp
from jax import lax
from jax.experimental import pallas as pl
from jax.experimental.pallas import tpu as pltpu

D_MODEL = 2048
DEPTH = 4
CHUNK = 128
SGU_WIDTH = D_MODEL // 4
SGU_GROUPS = 4
SGU_GROUP_DIM = SGU_WIDTH // SGU_GROUPS
HGRN_WIDTH = D_MODEL // 4
HGRN_HEADS = 4
HGRN_HEAD_DIM = HGRN_WIDTH // HGRN_HEADS
MLA_HEADS = 8
MLA_V_DIM = (D_MODEL - SGU_WIDTH - HGRN_WIDTH) // MLA_HEADS
MLA_NOPE_DIM = 128
MLA_ROPE_DIM = 64
MLA_QK_DIM = MLA_NOPE_DIM + MLA_ROPE_DIM
Q_LORA_RANK = D_MODEL // 4
KV_LORA_RANK = D_MODEL // 4
ROPE_THETA = 10000.0
N_GROUPS = 4
EXPERTS_PER_GROUP = 8
N_EXPERTS = N_GROUPS * EXPERTS_PER_GROUP
TOP_K = 2
D_EXPERT = D_MODEL // 4
NORM_EPS = 1e-5
DEEPNORM_ALPHA = (2 * DEPTH) ** 0.25
D_IN = SGU_WIDTH * 2 + HGRN_WIDTH * 4 + Q_LORA_RANK + KV_LORA_RANK + MLA_ROPE_DIM

LANES = 128
D_IN_PAD = ((D_IN + LANES - 1) // LANES) * LANES
Q_HEAD_PAD = 2 * LANES
VMEM_LIMIT = 56 * 1024 * 1024

TM_INPROJ = 512
TN_INPROJ = D_IN_PAD // 3
TM_SGU = 512
TM_MLA = 512
TQ_ATTN = 512
TM_OUT = 256
MOE_ROWS = 256
TM_COMB = 256

BF16 = jnp.bfloat16
F32 = jnp.float32


def _params(*sem):
    return pltpu.CompilerParams(dimension_semantics=sem, vmem_limit_bytes=VMEM_LIMIT)


def _rope_kernel(pos_ref, inv_ref, sgn_ref, c_ref, s_ref):
    ang = pos_ref[...].astype(F32) * inv_ref[...]
    live = sgn_ref[...] != 0.0
    c_ref[...] = jnp.where(live, jnp.cos(ang), 0.0)
    s_ref[...] = jnp.sin(ang) * sgn_ref[...]


def _rope_tables(positions):
    n = positions.size
    tm = min(n, 1024)
    half = MLA_ROPE_DIM // 2
    inv = 1.0 / (ROPE_THETA ** (jnp.arange(0, MLA_ROPE_DIM, 2, dtype=F32) / MLA_ROPE_DIM))
    zeros = jnp.zeros((LANES - 2 * half,), F32)
    inv_t = jnp.concatenate([inv, inv, zeros])[None, :]
    sgn_t = jnp.concatenate([-jnp.ones((half,), F32), jnp.ones((half,), F32), zeros])[None, :]
    row = pl.BlockSpec((1, LANES), lambda i: (0, 0))
    return pl.pallas_call(
        _rope_kernel,
        out_shape=(jax.ShapeDtypeStruct((n, LANES), F32),) * 2,
        grid=(n // tm,),
        in_specs=[pl.BlockSpec((tm, 1), lambda i: (i, 0)), row, row],
        out_specs=(pl.BlockSpec((tm, LANES), lambda i: (i, 0)),) * 2,
        compiler_params=_params("parallel"),
        name="rope_tables",
    )(positions.reshape(n, 1), inv_t, sgn_t)


def _inproj_kernel(x_ref, w_ref, o_ref):
    o_ref[...] = jnp.dot(x_ref[...].astype(BF16), w_ref[...], preferred_element_type=F32)


def _inproj(x, w):
    n, d = x.shape
    tm = min(n, TM_INPROJ)
    return pl.pallas_call(
        _inproj_kernel,
        out_shape=jax.ShapeDtypeStruct((n, D_IN_PAD), F32),
        grid=(D_IN_PAD // TN_INPROJ, n // tm),
        in_specs=[pl.BlockSpec((tm, d), lambda j, i: (i, 0)),
                  pl.BlockSpec((d, TN_INPROJ), lambda j, i: (0, j))],
        out_specs=pl.BlockSpec((tm, TN_INPROJ), lambda j, i: (i, j)),
        compiler_params=_params("parallel", "parallel"),
        name="in_proj",
    )(x, w)


def _sgu_kernel(u_ref, v_ref, g_ref, b_ref, ws_ref, bs_ref, o_ref):
    u = jax.nn.gelu(u_ref[...])
    v = jax.nn.gelu(v_ref[...])
    mu = jnp.mean(v, axis=-1, keepdims=True)
    vc = v - mu
    var = jnp.mean(vc * vc, axis=-1, keepdims=True)
    vn = (vc * lax.rsqrt(var + NORM_EPS) * g_ref[...] + b_ref[...]).astype(BF16)
    row = lax.broadcasted_iota(jnp.int32, (CHUNK, CHUNK), 0)
    col = lax.broadcasted_iota(jnp.int32, (CHUNK, CHUNK), 1)
    causal = row >= col
    bs = bs_ref[...]
    for g in range(SGU_GROUPS):
        w = jnp.where(causal, ws_ref[g], 0.0).astype(BF16)
        cs = slice(g * SGU_GROUP_DIM, (g + 1) * SGU_GROUP_DIM)
        for c in range(u.shape[0] // CHUNK):
            rs = slice(c * CHUNK, (c + 1) * CHUNK)
            mixed = jnp.dot(w, vn[rs, cs], preferred_element_type=F32) + bs[:, g:g + 1]
            o_ref[rs, cs] = (u[rs, cs] * mixed).astype(o_ref.dtype)


def _sgu(h, ln_g, ln_b, ws, bs):
    n = h.shape[0]
    tm = min(n, TM_SGU)
    full = lambda shape: pl.BlockSpec(shape, lambda i: (0,) * len(shape))
    return pl.pallas_call(
        _sgu_kernel,
        out_shape=jax.ShapeDtypeStruct((n, SGU_WIDTH), BF16),
        grid=(n // tm,),
        in_specs=[pl.BlockSpec((tm, SGU_WIDTH), lambda i: (i, 0)),
                  pl.BlockSpec((tm, SGU_WIDTH), lambda i: (i, 1)),
                  full((1, SGU_WIDTH)), full((1, SGU_WIDTH)),
                  full((SGU_GROUPS, CHUNK, CHUNK)), full((CHUNK, SGU_GROUPS))],
        out_specs=pl.BlockSpec((tm, SGU_WIDTH), lambda i: (i, 0)),
        compiler_params=_params("parallel"),
        name="sgu",
    )(h, h, ln_g[None, :], ln_b[None, :], ws, bs.T)


def _dot_nt(a, b):
    return lax.dot_general(a, b, (((1,), (1,)), ((), ())), preferred_element_type=F32)


def _hgrn_kernel(q_ref, f_ref, i_ref, g_ref, la_ref, lc_ref, om_ref, ng_ref, o_ref, state_ref):
    @pl.when(pl.program_id(1) == 0)
    def _():
        state_ref[...] = jnp.zeros_like(state_ref)

    T, K = CHUNK, HGRN_HEAD_DIM
    fx = f_ref[...]
    log_f = jnp.logaddexp(la_ref[...], lc_ref[...] + jax.nn.log_sigmoid(fx))
    key = om_ref[...] * jax.nn.sigmoid(-fx)
    qry = jax.nn.silu(q_ref[...])
    val = i_ref[...]
    gate = jax.nn.silu(g_ref[...])

    row = lax.broadcasted_iota(jnp.int32, (T, T), 0)
    col = lax.broadcasted_iota(jnp.int32, (T, T), 1)
    tril = (row >= col).astype(F32)
    G_all = jnp.dot(tril, log_f, preferred_element_type=F32, precision=lax.Precision.HIGHEST)

    SUB = 8
    for h in range(HGRN_HEADS):
        cs = slice(h * K, (h + 1) * K)
        G, q, k, v = G_all[:, cs], qry[:, cs], key[:, cs], val[:, cs]
        A = jnp.zeros((T, T), F32)
        m = T // 2
        while m >= SUB:
            blk = 2 * m
            g_mid = jnp.concatenate(
                [jnp.broadcast_to(G[b * blk + m - 1:b * blk + m, :], (blk, K)) for b in range(T // blk)], axis=0)
            lower = (row % blk) >= m
            qs = jnp.where(lower, q * jnp.exp(G - g_mid), 0.0).astype(BF16)
            ks = jnp.where(lower, 0.0, k * jnp.exp(g_mid - G)).astype(BF16)
            same = (row // blk) == (col // blk)
            A = A + jnp.where(same, _dot_nt(qs, ks), 0.0)
            m //= 2
        rmod = row % SUB
        for d in range(SUB):
            if d == 0:
                prod = q * k
            else:
                kd = pltpu.roll(k, d, axis=0)
                gd = pltpu.roll(G, d, axis=0)
                prod = jnp.where(rmod >= d, q * kd * jnp.exp(G - gd), 0.0)
            A = A + jnp.where(col == row - d, jnp.sum(prod, axis=-1, keepdims=True), 0.0)

        st = state_ref[h]
        o = jnp.dot(A.astype(BF16), v.astype(BF16), preferred_element_type=F32)
        o = o + _dot_nt((q * jnp.exp(G)).astype(BF16), st.astype(BF16))
        g_last = G[T - 1:T, :]
        k_dec = (k * jnp.exp(g_last - G)).astype(BF16)
        state_ref[h] = jnp.exp(g_last) * st + jnp.dot(v.T.astype(BF16), k_dec, preferred_element_type=F32)

        ms = jnp.mean(o * o, axis=-1, keepdims=True)
        o = o * lax.rsqrt(ms + NORM_EPS) * ng_ref[:, cs]
        o_ref[:, cs] = (o * gate[:, cs]).astype(o_ref.dtype)


def _hgrn(h, lb, norm_g, batch):
    n = h.shape[0]
    nc = n // batch // CHUNK
    W = HGRN_WIDTH
    col = lambda j: pl.BlockSpec((CHUNK, W), lambda b, c: (b * nc + c, j))
    vec = pl.BlockSpec((1, W), lambda b, c: (0, 0))
    base = 2 * SGU_WIDTH // W
    return pl.pallas_call(
        _hgrn_kernel,
        out_shape=jax.ShapeDtypeStruct((n, W), BF16),
        grid=(batch, nc),
        in_specs=[col(base), col(base + 1), col(base + 2), col(base + 3), vec, vec, vec, vec],
        out_specs=pl.BlockSpec((CHUNK, W), lambda b, c: (b * nc + c, 0)),
        scratch_shapes=[pltpu.VMEM((HGRN_HEADS, HGRN_HEAD_DIM, HGRN_HEAD_DIM), F32)],
        compiler_params=_params("parallel", "arbitrary"),
        name="hgrn2",
    )(h, h, h, h, jnp.log(lb)[None, :], jnp.log1p(-lb)[None, :], (1.0 - lb)[None, :], norm_g[None, :])


def _rms(x, g):
    ms = jnp.mean(x * x, axis=-1, keepdims=True)
    return x * lax.rsqrt(ms + NORM_EPS) * g


def _rope(x, c, s):
    half = MLA_ROPE_DIM // 2
    lane = lax.broadcasted_iota(jnp.int32, x.shape, 1)
    swapped = jnp.where(lane < half, pltpu.roll(x, LANES - half, axis=1), pltpu.roll(x, half, axis=1))
    return x * c + swapped * s


def _mla_up_kernel(cq_ref, ckv_ref, kr_ref, c_ref, s_ref, qg_ref, kvg_ref, wq_ref, wkv_ref,
                   q_ref, kn_ref, v_ref, kro_ref):
    c, s = c_ref[...], s_ref[...]
    scale = MLA_QK_DIM ** -0.5
    q = jnp.dot(_rms(cq_ref[...], qg_ref[...]).astype(BF16), wq_ref[...], preferred_element_type=F32)
    kv = jnp.dot(_rms(ckv_ref[...], kvg_ref[...]).astype(BF16), wkv_ref[...], preferred_element_type=F32)
    for h in range(MLA_HEADS):
        base = h * Q_HEAD_PAD
        q_ref[h, :, :LANES] = (q[:, base:base + LANES] * scale).astype(BF16)
        q_ref[h, :, LANES:] = (_rope(q[:, base + LANES:base + 2 * LANES], c, s) * scale).astype(BF16)
        kbase = h * (MLA_NOPE_DIM + MLA_V_DIM)
        kn_ref[h] = kv[:, kbase:kbase + MLA_NOPE_DIM].astype(BF16)
        v_ref[h] = kv[:, kbase + MLA_NOPE_DIM:kbase + MLA_NOPE_DIM + MLA_V_DIM].astype(BF16)
    kro_ref[...] = _rope(kr_ref[...], c, s).astype(BF16)


def _mla_up(h, tab_c, tab_s, qn_g, w_uq, kvn_g, w_ukv):
    n = h.shape[0]
    tm = min(n, TM_MLA)
    H = MLA_HEADS
    cq_blk = (2 * SGU_WIDTH + 4 * HGRN_WIDTH) // Q_LORA_RANK
    kr_blk = (2 * SGU_WIDTH + 4 * HGRN_WIDTH + Q_LORA_RANK + KV_LORA_RANK) // LANES
    full = lambda shape: pl.BlockSpec(shape, lambda i: (0,) * len(shape))
    heads = lambda w: pl.BlockSpec((H, tm, w), lambda i: (0, i, 0))
    return pl.pallas_call(
        _mla_up_kernel,
        out_shape=(jax.ShapeDtypeStruct((H, n, Q_HEAD_PAD), BF16),
                   jax.ShapeDtypeStruct((H, n, MLA_NOPE_DIM), BF16),
                   jax.ShapeDtypeStruct((H, n, MLA_V_DIM), BF16),
                   jax.ShapeDtypeStruct((n, LANES), BF16)),
        grid=(n // tm,),
        in_specs=[pl.BlockSpec((tm, Q_LORA_RANK), lambda i: (i, cq_blk)),
                  pl.BlockSpec((tm, KV_LORA_RANK), lambda i: (i, cq_blk + 1)),
                  pl.BlockSpec((tm, LANES), lambda i: (i, kr_blk)),
                  pl.BlockSpec((tm, LANES), lambda i: (i, 0)),
                  pl.BlockSpec((tm, LANES), lambda i: (i, 0)),
                  full((1, Q_LORA_RANK)), full((1, KV_LORA_RANK)),
                  full((Q_LORA_RANK, H * Q_HEAD_PAD)),
                  full((KV_LORA_RANK, H * (MLA_NOPE_DIM + MLA_V_DIM)))],
        out_specs=(heads(Q_HEAD_PAD), heads(MLA_NOPE_DIM), heads(MLA_V_DIM),
                   pl.BlockSpec((tm, LANES), lambda i: (i, 0))),
        compiler_params=_params("parallel"),
        name="mla_up",
    )(h, h, h, tab_c, tab_s, qn_g[None, :], kvn_g[None, :], w_uq, w_ukv)


def _attn_kernel(q_ref, kn_ref, kr_ref, v_ref, o_ref, m_ref, l_ref, acc_ref):
    i = pl.program_id(2)
    tq = q_ref.shape[1]
    q = q_ref[0]
    m_ref[...] = jnp.full_like(m_ref, -jnp.inf)
    l_ref[...] = jnp.zeros_like(l_ref)
    acc_ref[...] = jnp.zeros_like(acc_ref)

    def step(j, masked):
        start = pl.multiple_of(j * tq, tq)
        k = jnp.concatenate([kn_ref[0, pl.ds(start, tq), :], kr_ref[pl.ds(start, tq), :]], axis=-1)
        s = _dot_nt(q, k)
        if masked:
            row = lax.broadcasted_iota(jnp.int32, s.shape, 0)
            col = lax.broadcasted_iota(jnp.int32, s.shape, 1)
            s = jnp.where(col <= row, s, -jnp.inf)
        m_prev = m_ref[...]
        m_new = jnp.maximum(m_prev, jnp.max(s, axis=-1, keepdims=True))
        p = jnp.exp(s - m_new)
        alpha = jnp.exp(m_prev - m_new)
        l_ref[...] = alpha * l_ref[...] + jnp.sum(p, axis=-1, keepdims=True)
        acc_ref[...] = alpha * acc_ref[...] + jnp.dot(
            p.astype(BF16), v_ref[0, pl.ds(start, tq), :], preferred_element_type=F32)
        m_ref[...] = m_new

    def body(j, carry):
        step(j, masked=False)
        return carry

    lax.fori_loop(0, i, body, 0)
    step(i, masked=True)
    o_ref[...] = (acc_ref[...] / l_ref[...]).astype(o_ref.dtype)


def _attention(q, kn, kr, v, batch):
    H, n, _ = q.shape
    S = n // batch
    tq = min(S, TQ_ATTN)
    nq = S // tq
    return pl.pallas_call(
        _attn_kernel,
        out_shape=jax.ShapeDtypeStruct((n, H * MLA_V_DIM), BF16),
        grid=(batch, H, nq),
        in_specs=[pl.BlockSpec((1, tq, Q_HEAD_PAD), lambda b, h, i: (h, b * nq + i, 0)),
                  pl.BlockSpec((1, S, MLA_NOPE_DIM), lambda b, h, i: (h, b, 0)),
                  pl.BlockSpec((S, LANES), lambda b, h, i: (b, 0)),
                  pl.BlockSpec((1, S, MLA_V_DIM), lambda b, h, i: (h, b, 0))],
        out_specs=pl.BlockSpec((tq, MLA_V_DIM), lambda b, h, i: (b * nq + i, h)),
        scratch_shapes=[pltpu.VMEM((tq, 1), F32), pltpu.VMEM((tq, 1), F32), pltpu.VMEM((tq, MLA_V_DIM), F32)],
        compiler_params=_params("parallel", "parallel", "arbitrary"),
        name="mla_attention",
    )(q, kn, kr, v)


def _layer_norm(x, g, b):
    mu = jnp.mean(x, axis=-1, keepdims=True)
    xc = x - mu
    var = jnp.mean(xc * xc, axis=-1, keepdims=True)
    return xc * lax.rsqrt(var + NORM_EPS) * g + b


def _first_index(hit, lane):
    return jnp.min(jnp.where(hit, lane, LANES), axis=-1, keepdims=True)


def _route(logits):
    lane = lax.broadcasted_iota(jnp.int32, logits.shape, 1)
    neg = -jnp.inf
    gl = jnp.where((lane >= N_EXPERTS) & (lane < N_EXPERTS + N_GROUPS), logits, neg)
    ge = jnp.exp(gl - jnp.max(gl, axis=-1, keepdims=True))
    gp = ge / jnp.sum(ge, axis=-1, keepdims=True)
    g_val = jnp.max(gp, axis=-1, keepdims=True)
    g_idx = _first_index(gp == g_val, lane) - N_EXPERTS
    el = jnp.where((lane // EXPERTS_PER_GROUP == g_idx) & (lane < N_EXPERTS), logits, neg)
    m1 = jnp.max(el, axis=-1, keepdims=True)
    i1 = _first_index(el == m1, lane)
    el2 = jnp.where(lane == i1, neg, el)
    m2 = jnp.max(el2, axis=-1, keepdims=True)
    i2 = _first_index(el2 == m2, lane)
    e2 = jnp.exp(m2 - m1)
    den = 1.0 + e2
    w1 = g_val * (1.0 / den)
    w2 = g_val * (e2 / den)
    out = jnp.where(lane == 0, w1, 0.0)
    out = jnp.where(lane == 1, w2, out)
    out = jnp.where(lane == 2, i1.astype(F32), out)
    out = jnp.where(lane == 3, i2.astype(F32), out)
    return out


def _outproj_kernel(ya_ref, yb_ref, yc_ref, x_ref, w_ref, g_ref, b_ref, wr_ref, br_ref, x1_ref, rt_ref):
    wa = SGU_WIDTH
    wb = SGU_WIDTH + HGRN_WIDTH
    mix = jnp.dot(ya_ref[...], w_ref[:wa, :], preferred_element_type=F32)
    mix = mix + jnp.dot(yb_ref[...], w_ref[wa:wb, :], preferred_element_type=F32)
    mix = mix + jnp.dot(yc_ref[...], w_ref[wb:, :], preferred_element_type=F32)
    x1 = _layer_norm(DEEPNORM_ALPHA * x_ref[...] + mix, g_ref[...], b_ref[...])
    x1_ref[...] = x1
    logits = jnp.dot(x1, wr_ref[...], preferred_element_type=F32, precision=lax.Precision.HIGHEST) + br_ref[...]
    rt_ref[...] = _route(logits)


def _outproj(ya, yb, yc, x, w_out, ln_g, ln_b, wr, br):
    n, d = x.shape
    tm = min(n, TM_OUT)
    full = lambda shape: pl.BlockSpec(shape, lambda i: (0,) * len(shape))
    rows = lambda w: pl.BlockSpec((tm, w), lambda i: (i, 0))
    return pl.pallas_call(
        _outproj_kernel,
        out_shape=(jax.ShapeDtypeStruct((n, d), F32), jax.ShapeDtypeStruct((n, LANES), F32)),
        grid=(n // tm,),
        in_specs=[rows(ya.shape[1]), rows(yb.shape[1]), rows(yc.shape[1]), rows(d),
                  full((d, d)), full((1, d)), full((1, d)), full((d, LANES)), full((1, LANES))],
        out_specs=(rows(d), rows(LANES)),
        compiler_params=_params("parallel"),
        name="out_proj_ln_router",
    )(ya, yb, yc, x, w_out, ln_g[None, :], ln_b[None, :], wr, br)


def _moe_kernel(tok_ref, exp_ref, used_ref, x_hbm, wg_ref, wu_ref, wd_ref, y_ref, xbuf, sem):
    i = pl.program_id(0)
    nb = pl.num_programs(0)
    rows = xbuf.shape[1]
    used = used_ref[0]

    def gather(block, slot):
        def issue(r, carry):
            tok = tok_ref[block * rows + r]
            pltpu.make_async_copy(x_hbm.at[pl.ds(tok, 1), :], xbuf.at[slot, pl.ds(r, 1), :], sem.at[slot]).start()
            return carry
        lax.fori_loop(0, rows, issue, 0, unroll=8)

    @pl.when((i == 0) & (used > 0))
    def _():
        gather(0, 0)

    @pl.when((i + 1 < nb) & (i + 1 < used))
    def _():
        gather(i + 1, (i + 1) % 2)

    @pl.when(i < used)
    def _():
        slot = i % 2
        pltpu.make_async_copy(x_hbm.at[pl.ds(0, rows), :], xbuf.at[slot], sem.at[slot]).wait()
        xb = xbuf[slot].astype(BF16)
        hg = jnp.dot(xb, wg_ref[0], preferred_element_type=F32)
        hu = jnp.dot(xb, wu_ref[0], preferred_element_type=F32)
        hh = (jax.nn.silu(hg) * hu).astype(BF16)
        y_ref[...] = jnp.dot(hh, wd_ref[0], preferred_element_type=F32)

    @pl.when(i >= used)
    def _():
        y_ref[...] = jnp.zeros_like(y_ref)


def _moe(slot_tok, blk_exp, used, x1, w_gate, w_up, w_down):
    n, d = x1.shape
    nb = blk_exp.shape[0]
    rows = slot_tok.shape[0] // nb
    de = w_gate.shape[2]
    return pl.pallas_call(
        _moe_kernel,
        out_shape=jax.ShapeDtypeStruct((nb * rows, d), F32),
        grid_spec=pltpu.PrefetchScalarGridSpec(
            num_scalar_prefetch=3,
            grid=(nb,),
            in_specs=[pl.BlockSpec(memory_space=pl.ANY),
                      pl.BlockSpec((1, d, de), lambda i, tok, exp, used: (exp[i], 0, 0)),
                      pl.BlockSpec((1, d, de), lambda i, tok, exp, used: (exp[i], 0, 0)),
                      pl.BlockSpec((1, de, d), lambda i, tok, exp, used: (exp[i], 0, 0))],
            out_specs=pl.BlockSpec((rows, d), lambda i, tok, exp, used: (i, 0)),
            scratch_shapes=[pltpu.VMEM((2, rows, d), F32), pltpu.SemaphoreType.DMA((2,))]),
        compiler_params=_params("arbitrary"),
        name="moe_experts",
    )(slot_tok, blk_exp, used, x1, w_gate, w_up, w_down)


def _combine_kernel(pos_ref, y_hbm, x_ref, rt_ref, g_ref, b_ref, o_ref, ybuf, sem):
    i = pl.program_id(0)
    nb = pl.num_programs(0)
    tm = x_ref.shape[0]

    def gather(block, slot):
        def issue(r, carry):
            for k in range(TOP_K):
                p = pos_ref[(block * tm + r) * TOP_K + k]
                pltpu.make_async_copy(y_hbm.at[pl.ds(p, 1), :], ybuf.at[slot, k, pl.ds(r, 1), :], sem.at[slot]).start()
            return carry
        lax.fori_loop(0, tm, issue, 0, unroll=4)

    @pl.when(i == 0)
    def _():
        gather(0, 0)

    @pl.when(i + 1 < nb)
    def _():
        gather(i + 1, (i + 1) % 2)

    slot = i % 2
    for k in range(TOP_K):
        pltpu.make_async_copy(y_hbm.at[pl.ds(0, tm), :], ybuf.at[slot, k], sem.at[slot]).wait()
    rt = rt_ref[...]
    ffn = ybuf[slot, 0] * rt[:, 0:1] + ybuf[slot, 1] * rt[:, 1:2]
    o_ref[...] = _layer_norm(DEEPNORM_ALPHA * x_ref[...] + ffn, g_ref[...], b_ref[...])


def _combine(pos, ys, x1, route, ln_g, ln_b):
    n, d = x1.shape
    tm = min(n, TM_COMB)
    return pl.pallas_call(
        _combine_kernel,
        out_shape=jax.ShapeDtypeStruct((n, d), F32),
        grid_spec=pltpu.PrefetchScalarGridSpec(
            num_scalar_prefetch=1,
            grid=(n // tm,),
            in_specs=[pl.BlockSpec(memory_space=pl.ANY),
                      pl.BlockSpec((tm, d), lambda i, pos: (i, 0)),
                      pl.BlockSpec((tm, LANES), lambda i, pos: (i, 0)),
                      pl.BlockSpec((1, d), lambda i, pos: (0, 0)),
                      pl.BlockSpec((1, d), lambda i, pos: (0, 0))],
            out_specs=pl.BlockSpec((tm, d), lambda i, pos: (i, 0)),
            scratch_shapes=[pltpu.VMEM((2, TOP_K, tm, d), F32), pltpu.SemaphoreType.DMA((2,))]),
        compiler_params=_params("arbitrary"),
        name="moe_combine_ln",
    )(pos, ys, x1, route, ln_g[None, :], ln_b[None, :])


def _dispatch_plan(route, rows):
    n = route.shape[0]
    nk = n * TOP_K
    flat_e = route[:, 2:2 + TOP_K].astype(jnp.int32).reshape(nk)
    onehot = (flat_e[:, None] == jnp.arange(N_EXPERTS, dtype=jnp.int32)[None, :]).astype(jnp.int32)
    csum = jnp.cumsum(onehot, axis=0)
    rank = jnp.take_along_axis(csum, flat_e[:, None], axis=1)[:, 0] - 1
    counts = csum[-1]
    padded = ((counts + rows - 1) // rows) * rows
    pad_end = jnp.cumsum(padded)
    pad_start = pad_end - padded
    dest = pad_start[flat_e] + rank
    nb = (nk + rows - 1) // rows + N_EXPERTS
    slot_tok = jnp.zeros((nb * rows,), jnp.int32).at[dest].set(jnp.arange(nk, dtype=jnp.int32) // TOP_K)
    blk_exp = jnp.minimum(
        jnp.searchsorted(pad_end, jnp.arange(nb, dtype=jnp.int32) * rows, side="right"), N_EXPERTS - 1
    ).astype(jnp.int32)
    used = (pad_end[-1] // rows).astype(jnp.int32).reshape(1)
    return slot_tok, blk_exp, used, dest.astype(jnp.int32)


def _pad_q_heads(w_uq):
    r = w_uq.shape[0]
    w = w_uq.reshape(r, MLA_HEADS, MLA_QK_DIM)
    w = jnp.pad(w, ((0, 0), (0, 0), (0, Q_HEAD_PAD - MLA_QK_DIM)))
    return w.reshape(r, MLA_HEADS * Q_HEAD_PAD)


def kernel(x, positions, w_in, sgu_ln_g, sgu_ln_b, sgu_ws, sgu_b, hgrn_lb_logits, hgrn_norm_g, mla_qn_g, mla_w_uq, mla_kvn_g, mla_w_ukv, w_out, ln1_g, ln1_b, router_group_w, router_group_b, router_expert_w, router_expert_b, expert_w_gate, expert_w_up, expert_w_down, ln2_g, ln2_b):
    B, S, D = x.shape
    n = B * S
    depth = w_in.shape[0]
    tab_c, tab_s = _rope_tables(positions)
    lb_cum = jnp.cumsum(jax.nn.softmax(hgrn_lb_logits.astype(F32), axis=0), axis=0)
    lower_bounds = lb_cum - lb_cum[0:1]
    xt = x.reshape(n, D)
    for l in range(depth):
        w_in_l = jnp.pad(w_in[l], ((0, 0), (0, D_IN_PAD - D_IN))).astype(BF16)
        h = _inproj(xt, w_in_l)
        y_a = _sgu(h, sgu_ln_g[l], sgu_ln_b[l], sgu_ws[l], sgu_b[l])
        y_b = _hgrn(h, lower_bounds[l], hgrn_norm_g[l], B)
        q, kn, v, kr = _mla_up(h, tab_c, tab_s, mla_qn_g[l], _pad_q_heads(mla_w_uq[l]).astype(BF16),
                               mla_kvn_g[l], mla_w_ukv[l].astype(BF16))
        y_c = _attention(q, kn, kr, v, B)
        wr = jnp.pad(jnp.concatenate([router_expert_w[l], router_group_w[l]], axis=1),
                     ((0, 0), (0, LANES - N_EXPERTS - N_GROUPS)))
        br = jnp.pad(jnp.concatenate([router_expert_b[l], router_group_b[l]]), (0, LANES - N_EXPERTS - N_GROUPS))
        x1, route = _outproj(y_a, y_b, y_c, xt, w_out[l].astype(BF16), ln1_g[l], ln1_b[l], wr, br[None, :])
        slot_tok, blk_exp, used, pos = _dispatch_plan(route, MOE_ROWS)
        ys = _moe(slot_tok, blk_exp, used, x1, expert_w_gate[l].astype(BF16), expert_w_up[l].astype(BF16),
                  expert_w_down[l].astype(BF16))
        xt = _combine(pos, ys, x1, route, ln2_g[l], ln2_b[l])
    return xt.reshape(B, S, D)
```

```python
import functools

import jax
import jax.numpy as jnp
from jax import lax
from jax.experimental import pallas as pl
from jax.experimental.pallas import tpu as pltpu

D_MODEL = 2048
DEPTH = 4
CHUNK = 128
SGU_WIDTH = D_MODEL // 4
SGU_GROUPS = 4
SGU_GROUP_DIM = SGU_WIDTH // SGU_GROUPS
HGRN_WIDTH = D_MODEL // 4
HGRN_HEADS = 4
HGRN_HEAD_DIM = HGRN_WIDTH // HGRN_HEADS
MLA_HEADS = 8
MLA_V_DIM = (D_MODEL - SGU_WIDTH - HGRN_WIDTH) // MLA_HEADS
MLA_NOPE_DIM = 128
MLA_ROPE_DIM = 64
MLA_QK_DIM = MLA_NOPE_DIM + MLA_ROPE_DIM
Q_LORA_RANK = D_MODEL // 4
KV_LORA_RANK = D_MODEL // 4
ROPE_THETA = 10000.0
N_GROUPS = 4
EXPERTS_PER_GROUP = 8
N_EXPERTS = N_GROUPS * EXPERTS_PER_GROUP
TOP_K = 2
D_EXPERT = D_MODEL // 4
NORM_EPS = 1e-5
DEEPNORM_ALPHA = (2 * DEPTH) ** 0.25
D_IN = SGU_WIDTH * 2 + HGRN_WIDTH * 4 + Q_LORA_RANK + KV_LORA_RANK + MLA_ROPE_DIM

LANES = 128
D_IN_PAD = ((D_IN + LANES - 1) // LANES) * LANES
Q_HEAD_PAD = 2 * LANES
VMEM_LIMIT = 56 * 1024 * 1024

TM_INPROJ = 512
TN_INPROJ = D_IN_PAD // 3
TM_SGU = 512
TM_MLA = 512
TQ_ATTN = 1024
TK_ATTN = 512
TM_OUT = 256
MOE_ROWS = 256
TM_COMB = 256

BF16 = jnp.bfloat16
F32 = jnp.float32
LOG2_E = 1.4426950408889634


def _params(*sem):
    return pltpu.CompilerParams(dimension_semantics=sem, vmem_limit_bytes=VMEM_LIMIT)


def _rope_kernel(pos_ref, inv_ref, sgn_ref, c_ref, s_ref):
    ang = pos_ref[...].astype(F32) * inv_ref[...]
    live = sgn_ref[...] != 0.0
    c_ref[...] = jnp.where(live, jnp.cos(ang), 0.0)
    s_ref[...] = jnp.sin(ang) * sgn_ref[...]


def _rope_tables(positions):
    n = positions.size
    tm = min(n, 1024)
    half = MLA_ROPE_DIM // 2
    inv = 1.0 / (ROPE_THETA ** (jnp.arange(0, MLA_ROPE_DIM, 2, dtype=F32) / MLA_ROPE_DIM))
    zeros = jnp.zeros((LANES - 2 * half,), F32)
    inv_t = jnp.concatenate([inv, inv, zeros])[None, :]
    sgn_t = jnp.concatenate([-jnp.ones((half,), F32), jnp.ones((half,), F32), zeros])[None, :]
    row = pl.BlockSpec((1, LANES), lambda i: (0, 0))
    return pl.pallas_call(
        _rope_kernel,
        out_shape=(jax.ShapeDtypeStruct((n, LANES), F32),) * 2,
        grid=(n // tm,),
        in_specs=[pl.BlockSpec((tm, 1), lambda i: (i, 0)), row, row],
        out_specs=(pl.BlockSpec((tm, LANES), lambda i: (i, 0)),) * 2,
        compiler_params=_params("parallel"),
        name="rope_tables",
    )(positions.reshape(n, 1), inv_t, sgn_t)


def _inproj_kernel(x_ref, w_ref, o_ref):
    o_ref[...] = jnp.dot(x_ref[...].astype(BF16), w_ref[...], preferred_element_type=F32)


def _inproj(x, w):
    n, d = x.shape
    tm = min(n, TM_INPROJ)
    return pl.pallas_call(
        _inproj_kernel,
        out_shape=jax.ShapeDtypeStruct((n, D_IN_PAD), F32),
        grid=(D_IN_PAD // TN_INPROJ, n // tm),
        in_specs=[pl.BlockSpec((tm, d), lambda j, i: (i, 0)),
                  pl.BlockSpec((d, TN_INPROJ), lambda j, i: (0, j))],
        out_specs=pl.BlockSpec((tm, TN_INPROJ), lambda j, i: (i, j)),
        compiler_params=_params("parallel", "parallel"),
        name="in_proj",
    )(x, w)


def _sgu_kernel(u_ref, v_ref, g_ref, b_ref, ws_ref, bs_ref, o_ref):
    u = jax.nn.gelu(u_ref[...])
    v = jax.nn.gelu(v_ref[...])
    mu = jnp.mean(v, axis=-1, keepdims=True)
    vc = v - mu
    var = jnp.mean(vc * vc, axis=-1, keepdims=True)
    vn = (vc * lax.rsqrt(var + NORM_EPS) * g_ref[...] + b_ref[...]).astype(BF16)
    row = lax.broadcasted_iota(jnp.int32, (CHUNK, CHUNK), 0)
    col = lax.broadcasted_iota(jnp.int32, (CHUNK, CHUNK), 1)
    causal = row >= col
    bs = bs_ref[...]
    for g in range(SGU_GROUPS):
        w = jnp.where(causal, ws_ref[g], 0.0).astype(BF16)
        cs = slice(g * SGU_GROUP_DIM, (g + 1) * SGU_GROUP_DIM)
        for c in range(u.shape[0] // CHUNK):
            rs = slice(c * CHUNK, (c + 1) * CHUNK)
            mixed = jnp.dot(w, vn[rs, cs], preferred_element_type=F32) + bs[:, g:g + 1]
            o_ref[rs, cs] = (u[rs, cs] * mixed).astype(o_ref.dtype)


def _sgu(h, ln_g, ln_b, ws, bs):
    n = h.shape[0]
    tm = min(n, TM_SGU)
    full = lambda shape: pl.BlockSpec(shape, lambda i: (0,) * len(shape))
    return pl.pallas_call(
        _sgu_kernel,
        out_shape=jax.ShapeDtypeStruct((n, SGU_WIDTH), BF16),
        grid=(n // tm,),
        in_specs=[pl.BlockSpec((tm, SGU_WIDTH), lambda i: (i, 0)),
                  pl.BlockSpec((tm, SGU_WIDTH), lambda i: (i, 1)),
                  full((1, SGU_WIDTH)), full((1, SGU_WIDTH)),
                  full((SGU_GROUPS, CHUNK, CHUNK)), full((CHUNK, SGU_GROUPS))],
        out_specs=pl.BlockSpec((tm, SGU_WIDTH), lambda i: (i, 0)),
        compiler_params=_params("parallel"),
        name="sgu",
    )(h, h, ln_g[None, :], ln_b[None, :], ws, bs.T)


def _dot_nt(a, b):
    return lax.dot_general(a, b, (((1,), (1,)), ((), ())), preferred_element_type=F32)


def _hgrn_kernel(q_ref, f_ref, i_ref, g_ref, la_ref, lc_ref, om_ref, ng_ref, o_ref, state_ref):
    @pl.when(pl.program_id(1) == 0)
    def _():
        state_ref[...] = jnp.zeros_like(state_ref)

    T, K = CHUNK, HGRN_HEAD_DIM
    fx = f_ref[...]
    log_f = jnp.logaddexp(la_ref[...], lc_ref[...] + jax.nn.log_sigmoid(fx))
    key = om_ref[...] * jax.nn.sigmoid(-fx)
    qry = jax.nn.silu(q_ref[...])
    val = i_ref[...]
    gate = jax.nn.silu(g_ref[...])

    row = lax.broadcasted_iota(jnp.int32, (T, T), 0)
    col = lax.broadcasted_iota(jnp.int32, (T, T), 1)
    tril = (row >= col).astype(F32)
    G_all = jnp.dot(tril, log_f, preferred_element_type=F32, precision=lax.Precision.HIGHEST)

    SUB = 8
    for h in range(HGRN_HEADS):
        cs = slice(h * K, (h + 1) * K)
        G, q, k, v = G_all[:, cs], qry[:, cs], key[:, cs], val[:, cs]
        A = jnp.zeros((T, T), F32)
        m = T // 2
        while m >= SUB:
            blk = 2 * m
            g_mid = jnp.concatenate(
                [jnp.broadcast_to(G[b * blk + m - 1:b * blk + m, :], (blk, K)) for b in range(T // blk)], axis=0)
            lower = (row % blk) >= m
            qs = jnp.where(lower, q * jnp.exp(G - g_mid), 0.0).astype(BF16)
            ks = jnp.where(lower, 0.0, k * jnp.exp(g_mid - G)).astype(BF16)
            same = (row // blk) == (col // blk)
            A = A + jnp.where(same, _dot_nt(qs, ks), 0.0)
            m //= 2
        rmod = row % SUB
        for d in range(SUB):
            if d == 0:
                prod = q * k
            else:
                kd = pltpu.roll(k, d, axis=0)
                gd = pltpu.roll(G, d, axis=0)
                prod = jnp.where(rmod >= d, q * kd * jnp.exp(G - gd), 0.0)
            A = A + jnp.where(col == row - d, jnp.sum(prod, axis=-1, keepdims=True), 0.0)

        st = state_ref[h]
        o = jnp.dot(A.astype(BF16), v.astype(BF16), preferred_element_type=F32)
        o = o + _dot_nt((q * jnp.exp(G)).astype(BF16), st.astype(BF16))
        g_last = G[T - 1:T, :]
        k_dec = (k * jnp.exp(g_last - G)).astype(BF16)
        state_ref[h] = jnp.exp(g_last) * st + jnp.dot(v.T.astype(BF16), k_dec, preferred_element_type=F32)

        ms = jnp.mean(o * o, axis=-1, keepdims=True)
        o = o * lax.rsqrt(ms + NORM_EPS) * ng_ref[:, cs]
        o_ref[:, cs] = (o * gate[:, cs]).astype(o_ref.dtype)


def _hgrn(h, lb, norm_g, batch):
    n = h.shape[0]
    nc = n // batch // CHUNK
    W = HGRN_WIDTH
    col = lambda j: pl.BlockSpec((CHUNK, W), lambda b, c: (b * nc + c, j))
    vec = pl.BlockSpec((1, W), lambda b, c: (0, 0))
    base = 2 * SGU_WIDTH // W
    return pl.pallas_call(
        _hgrn_kernel,
        out_shape=jax.ShapeDtypeStruct((n, W), BF16),
        grid=(batch, nc),
        in_specs=[col(base), col(base + 1), col(base + 2), col(base + 3), vec, vec, vec, vec],
        out_specs=pl.BlockSpec((CHUNK, W), lambda b, c: (b * nc + c, 0)),
        scratch_shapes=[pltpu.VMEM((HGRN_HEADS, HGRN_HEAD_DIM, HGRN_HEAD_DIM), F32)],
        compiler_params=_params("parallel", "arbitrary"),
        name="hgrn2",
    )(h, h, h, h, jnp.log(lb)[None, :], jnp.log1p(-lb)[None, :], (1.0 - lb)[None, :], norm_g[None, :])


def _rms(x, g):
    ms = jnp.mean(x * x, axis=-1, keepdims=True)
    return x * lax.rsqrt(ms + NORM_EPS) * g


def _rope(x, c, s):
    half = MLA_ROPE_DIM // 2
    lane = lax.broadcasted_iota(jnp.int32, x.shape, 1)
    swapped = jnp.where(lane < half, pltpu.roll(x, LANES - half, axis=1), pltpu.roll(x, half, axis=1))
    return x * c + swapped * s


def _mla_up_kernel(cq_ref, ckv_ref, kr_ref, c_ref, s_ref, qg_ref, kvg_ref, wq_ref, wkv_ref,
                   q_ref, kn_ref, v_ref, kro_ref):
    c, s = c_ref[...], s_ref[...]
    scale = MLA_QK_DIM ** -0.5 * LOG2_E
    q = jnp.dot(_rms(cq_ref[...], qg_ref[...]).astype(BF16), wq_ref[...], preferred_element_type=F32)
    kv = jnp.dot(_rms(ckv_ref[...], kvg_ref[...]).astype(BF16), wkv_ref[...], preferred_element_type=F32)
    for h in range(MLA_HEADS):
        base = h * Q_HEAD_PAD
        q_ref[h, :, :LANES] = (q[:, base:base + LANES] * scale).astype(BF16)
        q_ref[h, :, LANES:] = (_rope(q[:, base + LANES:base + 2 * LANES], c, s) * scale).astype(BF16)
        kbase = h * (MLA_NOPE_DIM + MLA_V_DIM)
        kn_ref[h] = kv[:, kbase:kbase + MLA_NOPE_DIM].astype(BF16)
        v_ref[h] = kv[:, kbase + MLA_NOPE_DIM:kbase + MLA_NOPE_DIM + MLA_V_DIM].astype(BF16)
    kro_ref[...] = _rope(kr_ref[...], c, s).astype(BF16)


def _mla_up(h, tab_c, tab_s, qn_g, w_uq, kvn_g, w_ukv):
    n = h.shape[0]
    tm = min(n, TM_MLA)
    H = MLA_HEADS
    cq_blk = (2 * SGU_WIDTH + 4 * HGRN_WIDTH) // Q_LORA_RANK
    kr_blk = (2 * SGU_WIDTH + 4 * HGRN_WIDTH + Q_LORA_RANK + KV_LORA_RANK) // LANES
    full = lambda shape: pl.BlockSpec(shape, lambda i: (0,) * len(shape))
    heads = lambda w: pl.BlockSpec((H, tm, w), lambda i: (0, i, 0))
    return pl.pallas_call(
        _mla_up_kernel,
        out_shape=(jax.ShapeDtypeStruct((H, n, Q_HEAD_PAD), BF16),
                   jax.ShapeDtypeStruct((H, n, MLA_NOPE_DIM), BF16),
                   jax.ShapeDtypeStruct((H, n, MLA_V_DIM), BF16),
                   jax.ShapeDtypeStruct((n, LANES), BF16)),
        grid=(n // tm,),
        in_specs=[pl.BlockSpec((tm, Q_LORA_RANK), lambda i: (i, cq_blk)),
                  pl.BlockSpec((tm, KV_LORA_RANK), lambda i: (i, cq_blk + 1)),
                  pl.BlockSpec((tm, LANES), lambda i: (i, kr_blk)),
                  pl.BlockSpec((tm, LANES), lambda i: (i, 0)),
                  pl.BlockSpec((tm, LANES), lambda i: (i, 0)),
                  full((1, Q_LORA_RANK)), full((1, KV_LORA_RANK)),
                  full((Q_LORA_RANK, H * Q_HEAD_PAD)),
                  full((KV_LORA_RANK, H * (MLA_NOPE_DIM + MLA_V_DIM)))],
        out_specs=(heads(Q_HEAD_PAD), heads(MLA_NOPE_DIM), heads(MLA_V_DIM),
                   pl.BlockSpec((tm, LANES), lambda i: (i, 0))),
        compiler_params=_params("parallel"),
        name="mla_up",
    )(h, h, h, tab_c, tab_s, qn_g[None, :], kvn_g[None, :], w_uq, w_ukv)


def _attn_kernel(q_ref, kn_ref, kr_ref, v_ref, o_ref, m_ref, l_ref, acc_ref):
    i = pl.program_id(2)
    nch, tk, _ = acc_ref.shape
    m_ref[...] = jnp.full_like(m_ref, -jnp.inf)
    l_ref[...] = jnp.zeros_like(l_ref)
    acc_ref[...] = jnp.zeros_like(acc_ref)

    def chain(c, j, masked):
        start = pl.multiple_of(j * tk, tk)
        q = q_ref[0, c * tk:(c + 1) * tk, :]
        k = jnp.concatenate([kn_ref[0, pl.ds(start, tk), :], kr_ref[pl.ds(start, tk), :]], axis=-1)
        s = _dot_nt(q, k)
        if masked:
            row = lax.broadcasted_iota(jnp.int32, s.shape, 0)
            col = lax.broadcasted_iota(jnp.int32, s.shape, 1)
            s = jnp.where(col <= row, s, -jnp.inf)
        chunks = [s[:, t * LANES:(t + 1) * LANES] for t in range(tk // LANES)]
        smax = functools.reduce(jnp.maximum, chunks)
        m_prev = m_ref[c]
        m_new = jnp.maximum(m_prev, jnp.max(smax, axis=-1, keepdims=True))
        alpha = jnp.exp2(m_prev - m_new)
        ps = [jnp.exp2(ch - m_new) for ch in chunks]
        l_ref[c] = alpha * l_ref[c] + functools.reduce(jnp.add, ps)
        p = jnp.concatenate([x.astype(BF16) for x in ps], axis=-1)
        acc_ref[c] = alpha * acc_ref[c] + jnp.dot(p, v_ref[0, pl.ds(start, tk), :], preferred_element_type=F32)
        m_ref[c] = m_new

    def body(j, carry):
        for c in range(nch):
            chain(c, j, masked=False)
        return carry

    lax.fori_loop(0, nch * i, body, 0)
    for t in range(nch):
        for c in range(t, nch):
            chain(c, nch * i + t, masked=(c == t))
    for c in range(nch):
        l = jnp.sum(l_ref[c], axis=-1, keepdims=True)
        o_ref[c * tk:(c + 1) * tk, :] = (acc_ref[c] / l).astype(o_ref.dtype)


def _attention(q, kn, kr, v, batch):
    H, n, _ = q.shape
    S = n // batch
    tq = min(S, TQ_ATTN)
    tk = min(tq, TK_ATTN)
    nq = S // tq
    return pl.pallas_call(
        _attn_kernel,
        out_shape=jax.ShapeDtypeStruct((n, H * MLA_V_DIM), BF16),
        grid=(batch, H, nq),
        in_specs=[pl.BlockSpec((1, tq, Q_HEAD_PAD), lambda b, h, i: (h, b * nq + i, 0)),
                  pl.BlockSpec((1, S, MLA_NOPE_DIM), lambda b, h, i: (h, b, 0)),
                  pl.BlockSpec((S, LANES), lambda b, h, i: (b, 0)),
                  pl.BlockSpec((1, S, MLA_V_DIM), lambda b, h, i: (h, b, 0))],
        out_specs=pl.BlockSpec((tq, MLA_V_DIM), lambda b, h, i: (b * nq + i, h)),
        scratch_shapes=[pltpu.VMEM((tq // tk, tk, LANES), F32), pltpu.VMEM((tq // tk, tk, LANES), F32),
                        pltpu.VMEM((tq // tk, tk, MLA_V_DIM), F32)],
        compiler_params=_params("parallel", "parallel", "arbitrary"),
        name="mla_attention",
    )(q, kn, kr, v)


def _layer_norm(x, g, b):
    mu = jnp.mean(x, axis=-1, keepdims=True)
    xc = x - mu
    var = jnp.mean(xc * xc, axis=-1, keepdims=True)
    return xc * lax.rsqrt(var + NORM_EPS) * g + b


def _first_index(hit, lane):
    return jnp.min(jnp.where(hit, lane, LANES), axis=-1, keepdims=True)


def _route(logits):
    lane = lax.broadcasted_iota(jnp.int32, logits.shape, 1)
    neg = -jnp.inf
    gl = jnp.where((lane >= N_EXPERTS) & (lane < N_EXPERTS + N_GROUPS), logits, neg)
    ge = jnp.exp(gl - jnp.max(gl, axis=-1, keepdims=True))
    gp = ge / jnp.sum(ge, axis=-1, keepdims=True)
    g_val = jnp.max(gp, axis=-1, keepdims=True)
    g_idx = _first_index(gp == g_val, lane) - N_EXPERTS
    el = jnp.where((lane // EXPERTS_PER_GROUP == g_idx) & (lane < N_EXPERTS), logits, neg)
    m1 = jnp.max(el, axis=-1, keepdims=True)
    i1 = _first_index(el == m1, lane)
    el2 = jnp.where(lane == i1, neg, el)
    m2 = jnp.max(el2, axis=-1, keepdims=True)
    i2 = _first_index(el2 == m2, lane)
    e2 = jnp.exp(m2 - m1)
    den = 1.0 + e2
    w1 = g_val * (1.0 / den)
    w2 = g_val * (e2 / den)
    return w1, w2, i1, i2


def _outproj_kernel(ya_ref, yb_ref, yc_ref, x_ref, w_ref, g_ref, b_ref, wr_ref, br_ref, x1_ref, rt_ref, cnt_ref):
    @pl.when(pl.program_id(0) == 0)
    def _():
        cnt_ref[...] = jnp.zeros_like(cnt_ref)

    wa = SGU_WIDTH
    wb = SGU_WIDTH + HGRN_WIDTH
    mix = jnp.dot(ya_ref[...], w_ref[:wa, :], preferred_element_type=F32)
    mix = mix + jnp.dot(yb_ref[...], w_ref[wa:wb, :], preferred_element_type=F32)
    mix = mix + jnp.dot(yc_ref[...], w_ref[wb:, :], preferred_element_type=F32)
    x1 = _layer_norm(DEEPNORM_ALPHA * x_ref[...] + mix, g_ref[...], b_ref[...])
    x1_ref[...] = x1
    x_hi = x1.astype(BF16)
    x_lo = (x1 - x_hi.astype(F32)).astype(BF16)
    r = jnp.dot(x_hi, wr_ref[...], preferred_element_type=F32)
    logits = r[:, :LANES] + r[:, LANES:] + jnp.dot(x_lo, wr_ref[:, :LANES], preferred_element_type=F32) + br_ref[...]
    w1, w2, i1, i2 = _route(logits)

    tm = logits.shape[0]
    lane = lax.broadcasted_iota(jnp.int32, logits.shape, 1)
    hit1, hit2 = lane == i1, lane == i2
    oh1, oh2 = hit1.astype(F32), hit2.astype(F32)
    row = lax.broadcasted_iota(jnp.int32, (tm, tm), 0)
    col = lax.broadcasted_iota(jnp.int32, (tm, tm), 1)
    before = (col < row).astype(BF16)
    pre = jnp.dot(before, jnp.concatenate([oh1, oh2], axis=-1).astype(BF16), preferred_element_type=F32)
    tot1 = jnp.sum(oh1, axis=0, keepdims=True)
    tot2 = jnp.sum(oh2, axis=0, keepdims=True)
    carry = cnt_ref[...]
    rank1 = jnp.sum(jnp.where(hit1, pre[:, :LANES] + carry, 0.0), axis=-1, keepdims=True)
    rank2 = jnp.sum(jnp.where(hit2, pre[:, LANES:] + (carry + tot1), 0.0), axis=-1, keepdims=True)
    cnt_ref[...] = carry + tot1 + tot2

    out = jnp.where(lane == 0, w1, 0.0)
    out = jnp.where(lane == 1, w2, out)
    out = jnp.where(lane == 2, i1.astype(F32), out)
    out = jnp.where(lane == 3, i2.astype(F32), out)
    out = jnp.where(lane == 4, rank1, out)
    out = jnp.where(lane == 5, rank2, out)
    rt_ref[...] = out


def _outproj(ya, yb, yc, x, w_out, ln_g, ln_b, wr, br):
    n, d = x.shape
    tm = min(n, TM_OUT)
    full = lambda shape: pl.BlockSpec(shape, lambda i: (0,) * len(shape))
    rows = lambda w: pl.BlockSpec((tm, w), lambda i: (i, 0))
    wr_hi = wr.astype(BF16)
    wr_lo = (wr - wr_hi.astype(F32)).astype(BF16)
    return pl.pallas_call(
        _outproj_kernel,
        out_shape=(jax.ShapeDtypeStruct((n, d), F32), jax.ShapeDtypeStruct((n, LANES), F32),
                   jax.ShapeDtypeStruct((1, LANES), F32)),
        grid=(n // tm,),
        in_specs=[rows(ya.shape[1]), rows(yb.shape[1]), rows(yc.shape[1]), rows(d),
                  full((d, d)), full((1, d)), full((1, d)), full((d, 2 * LANES)), full((1, LANES))],
        out_specs=(rows(d), rows(LANES), full((1, LANES))),
        compiler_params=_params("arbitrary"),
        name="out_proj_ln_router",
    )(ya, yb, yc, x, w_out, ln_g[None, :], ln_b[None, :], jnp.concatenate([wr_hi, wr_lo], axis=1), br)


def _moe_kernel(src_ref, dst_ref, exp_ref, used_ref, x_hbm, wg_ref, wu_ref, wd_ref, y_hbm,
                xbuf0, xbuf1, ybuf0, ybuf1, wgb, wub, wdb, gsem, ssem):
    xbufs, ybufs = (xbuf0, xbuf1), (ybuf0, ybuf1)
    i = pl.program_id(0)
    nb = pl.num_programs(0) - 1
    rows = xbuf0.shape[0]
    used = used_ref[0]

    def gather_copy(tok, r, slot):
        return pltpu.make_async_copy(x_hbm.at[pl.ds(tok, 1), :], xbufs[slot].at[pl.ds(r, 1), :], gsem.at[slot])

    def scatter_copy(dst, r, slot):
        return pltpu.make_async_copy(ybufs[slot].at[pl.ds(r, 1), :], y_hbm.at[pl.ds(dst, 1), :], ssem.at[slot])

    def wait_gather(slot):
        pltpu.make_async_copy(x_hbm.at[pl.ds(0, rows), :], xbuf0, gsem.at[slot]).wait()

    def wait_scatter(slot):
        pltpu.make_async_copy(ybuf0, y_hbm.at[pl.ds(0, rows), :], ssem.at[slot]).wait()

    @pl.when(i == 0)
    def _():
        ybuf1[...] = jnp.zeros_like(ybuf1)
        n_real = y_hbm.shape[0] - 3 * rows
        fills = [pltpu.make_async_copy(ybuf1, y_hbm.at[pl.ds(n_real + (1 + t) * rows, rows), :], ssem.at[2])
                 for t in range(2)]
        for f in fills:
            f.start()
        for f in fills:
            f.wait()

        def issue(r, carry):
            gather_copy(src_ref[r], r, 0).start()
            return carry
        lax.fori_loop(0, rows, issue, 0, unroll=8)

    @pl.when((i >= 1) & (i <= used))
    def _():
        wait_scatter(i % 2)

    cur = jnp.minimum(i, nb - 1)
    @pl.when((i < used) & ((i == 0) | (exp_ref[cur] != exp_ref[jnp.maximum(cur - 1, 0)])))
    def _():
        wgb[...] = wg_ref[0].astype(BF16)
        wub[...] = wu_ref[0].astype(BF16)
        wdb[...] = wd_ref[0].astype(BF16)

    def compute_block(slot):
        other = 1 - slot
        wait_gather(slot)
        nxt = jnp.minimum(i + 1, nb - 1) * rows
        prv = i * rows
        for r in range(rows):
            gather_copy(src_ref[nxt + r], r, other).start()
            scatter_copy(dst_ref[prv + r], r, other).start()
        xb = xbufs[slot][...].astype(BF16)
        hg = jnp.dot(xb, wgb[...], preferred_element_type=F32)
        hu = jnp.dot(xb, wub[...], preferred_element_type=F32)
        hh = (jax.nn.silu(hg) * hu).astype(BF16)
        ybufs[slot][...] = jnp.dot(hh, wdb[...], preferred_element_type=F32)

    def drain_block(slot):
        def issue(r, carry):
            scatter_copy(dst_ref[i * rows + r], r, slot).start()
            return carry
        lax.fori_loop(0, rows, issue, 0, unroll=8)
        wait_scatter(slot)

    for parity in range(2):
        pl.when((i < used) & (i % 2 == parity))(functools.partial(compute_block, parity))

    @pl.when(i == used)
    def _():
        wait_gather(i % 2)

    for parity in range(2):
        pl.when((i == used) & (i >= 1) & ((i + 1) % 2 == parity))(functools.partial(drain_block, parity))


def _moe(slot_src, slot_dst, blk_exp, used, x1, w_gate, w_up, w_down, n_out):
    n, d = x1.shape
    nb = blk_exp.shape[0]
    rows = slot_src.shape[0] // nb
    de = w_gate.shape[2]
    blk = lambda i, src, dst, exp, used: (exp[jnp.minimum(i, nb - 1)], 0, 0)
    return pl.pallas_call(
        _moe_kernel,
        out_shape=jax.ShapeDtypeStruct((n_out, d), F32),
        grid_spec=pltpu.PrefetchScalarGridSpec(
            num_scalar_prefetch=4,
            grid=(nb + 1,),
            in_specs=[pl.BlockSpec(memory_space=pl.ANY),
                      pl.BlockSpec((1, d, de), blk), pl.BlockSpec((1, d, de), blk), pl.BlockSpec((1, de, d), blk)],
            out_specs=pl.BlockSpec(memory_space=pl.ANY),
            scratch_shapes=[pltpu.VMEM((rows, d), F32)] * 4 + [
                            pltpu.VMEM((d, de), BF16), pltpu.VMEM((d, de), BF16), pltpu.VMEM((de, d), BF16),
                            pltpu.SemaphoreType.DMA((2,)), pltpu.SemaphoreType.DMA((3,))]),
        compiler_params=_params("arbitrary"),
        name="moe_experts",
    )(slot_src, slot_dst, blk_exp, used, x1, w_gate, w_up, w_down)


def _combine_kernel(y_ref, x_ref, rt_ref, g_ref, b_ref, o_ref):
    d = x_ref.shape[1]
    rt = rt_ref[...]
    ffn = y_ref[:, :d] * rt[:, 0:1] + y_ref[:, d:] * rt[:, 1:2]
    o_ref[...] = _layer_norm(DEEPNORM_ALPHA * x_ref[...] + ffn, g_ref[...], b_ref[...])


def _combine(y_tok, x1, route, ln_g, ln_b):
    n, d = x1.shape
    tm = min(n, TM_COMB)
    y2 = y_tok.reshape(y_tok.shape[0] // TOP_K, TOP_K * d)
    return pl.pallas_call(
        _combine_kernel,
        out_shape=jax.ShapeDtypeStruct((n, d), F32),
        grid=(n // tm,),
        in_specs=[pl.BlockSpec((tm, TOP_K * d), lambda i: (i, 0)),
                  pl.BlockSpec((tm, d), lambda i: (i, 0)),
                  pl.BlockSpec((tm, LANES), lambda i: (i, 0)),
                  pl.BlockSpec((1, d), lambda i: (0, 0)),
                  pl.BlockSpec((1, d), lambda i: (0, 0))],
        out_specs=pl.BlockSpec((tm, d), lambda i: (i, 0)),
        compiler_params=_params("parallel"),
        name="moe_combine_ln",
    )(y2, x1, route, ln_g[None, :], ln_b[None, :])


def _dispatch_plan(route, counts, rows):
    n = route.shape[0]
    nk = n * TOP_K
    nb = (nk + rows - 1) // rows + N_EXPERTS
    counts = counts[0, :N_EXPERTS].astype(jnp.int32)
    padded = ((counts + rows - 1) // rows) * rows
    pad_end = jnp.cumsum(padded)
    pad_start = pad_end - padded
    flat_e = route[:, 2:2 + TOP_K].astype(jnp.int32).reshape(nk)
    rank = route[:, 2 + TOP_K:2 + 2 * TOP_K].astype(jnp.int32).reshape(nk)
    dest = pad_start[flat_e] + rank
    asg = jnp.full((nb * rows,), -1, jnp.int32).at[dest].set(jnp.arange(nk, dtype=jnp.int32))
    slot = jnp.arange(nb * rows, dtype=jnp.int32)
    dump = nk + rows + ((slot // rows) % 2) * rows + slot % rows
    slot_src = jnp.where(asg >= 0, asg // TOP_K, 0)
    slot_dst = jnp.concatenate([nk + jnp.arange(rows, dtype=jnp.int32), jnp.where(asg >= 0, asg, dump)])
    blk_exp = jnp.minimum(
        jnp.sum((pad_end[None, :] <= (jnp.arange(nb, dtype=jnp.int32) * rows)[:, None]).astype(jnp.int32), axis=1),
        N_EXPERTS - 1).astype(jnp.int32)
    used = (pad_end[-1] // rows).astype(jnp.int32).reshape(1)
    return slot_src, slot_dst, blk_exp, used, nk + 3 * rows


def _pad_q_heads(w_uq):
    r = w_uq.shape[0]
    w = w_uq.reshape(r, MLA_HEADS, MLA_QK_DIM)
    w = jnp.pad(w, ((0, 0), (0, 0), (0, Q_HEAD_PAD - MLA_QK_DIM)))
    return w.reshape(r, MLA_HEADS * Q_HEAD_PAD)


def kernel(x, positions, w_in, sgu_ln_g, sgu_ln_b, sgu_ws, sgu_b, hgrn_lb_logits, hgrn_norm_g, mla_qn_g, mla_w_uq, mla_kvn_g, mla_w_ukv, w_out, ln1_g, ln1_b, router_group_w, router_group_b, router_expert_w, router_expert_b, expert_w_gate, expert_w_up, expert_w_down, ln2_g, ln2_b):
    B, S, D = x.shape
    n = B * S
    depth = w_in.shape[0]
    tab_c, tab_s = _rope_tables(positions)
    lb_cum = jnp.cumsum(jax.nn.softmax(hgrn_lb_logits.astype(F32), axis=0), axis=0)
    lower_bounds = lb_cum - lb_cum[0:1]
    xt = x.reshape(n, D)
    for l in range(depth):
        w_in_l = jnp.pad(w_in[l], ((0, 0), (0, D_IN_PAD - D_IN))).astype(BF16)
        h = _inproj(xt, w_in_l)
        y_a = _sgu(h, sgu_ln_g[l], sgu_ln_b[l], sgu_ws[l], sgu_b[l])
        y_b = _hgrn(h, lower_bounds[l], hgrn_norm_g[l], B)
        q, kn, v, kr = _mla_up(h, tab_c, tab_s, mla_qn_g[l], _pad_q_heads(mla_w_uq[l]).astype(BF16),
                               mla_kvn_g[l], mla_w_ukv[l].astype(BF16))
        y_c = _attention(q, kn, kr, v, B)
        wr = jnp.pad(jnp.concatenate([router_expert_w[l], router_group_w[l]], axis=1),
                     ((0, 0), (0, LANES - N_EXPERTS - N_GROUPS)))
        br = jnp.pad(jnp.concatenate([router_expert_b[l], router_group_b[l]]), (0, LANES - N_EXPERTS - N_GROUPS))
        x1, route, counts = _outproj(y_a, y_b, y_c, xt, w_out[l].astype(BF16), ln1_g[l], ln1_b[l], wr, br[None, :])
        slot_src, slot_dst, blk_exp, used, n_out = _dispatch_plan(route, counts, MOE_ROWS)
        y_tok = _moe(slot_src, slot_dst, blk_exp, used, x1, expert_w_gate[l], expert_w_up[l], expert_w_down[l], n_out)
        xt = _combine(y_tok, x1, route, ln2_g[l], ln2_b[l])
    return xt.reshape(B, S, D)
```

```python
import functools

import jax
import jax.numpy as jnp
from jax import lax
from jax.experimental import pallas as pl
from jax.experimental.pallas import tpu as pltpu

D_MODEL = 2048
DEPTH = 4
CHUNK = 128
SGU_WIDTH = D_MODEL // 4
SGU_GROUPS = 4
SGU_GROUP_DIM = SGU_WIDTH // SGU_GROUPS
HGRN_WIDTH = D_MODEL // 4
HGRN_HEADS = 4
HGRN_HEAD_DIM = HGRN_WIDTH // HGRN_HEADS
MLA_HEADS = 8
MLA_V_DIM = (D_MODEL - SGU_WIDTH - HGRN_WIDTH) // MLA_HEADS
MLA_NOPE_DIM = 128
MLA_ROPE_DIM = 64
MLA_QK_DIM = MLA_NOPE_DIM + MLA_ROPE_DIM
Q_LORA_RANK = D_MODEL // 4
KV_LORA_RANK = D_MODEL // 4
ROPE_THETA = 10000.0
N_GROUPS = 4
EXPERTS_PER_GROUP = 8
N_EXPERTS = N_GROUPS * EXPERTS_PER_GROUP
TOP_K = 2
D_EXPERT = D_MODEL // 4
NORM_EPS = 1e-5
DEEPNORM_ALPHA = (2 * DEPTH) ** 0.25
D_IN = SGU_WIDTH * 2 + HGRN_WIDTH * 4 + Q_LORA_RANK + KV_LORA_RANK + MLA_ROPE_DIM

LANES = 128
D_IN_PAD = ((D_IN + LANES - 1) // LANES) * LANES
Q_HEAD_PAD = 2 * LANES
VMEM_LIMIT = 56 * 1024 * 1024

TM_INPROJ = 512
TN_INPROJ = D_IN_PAD // 3
TM_SGU = 512
TM_MLA = 512
TQ_ATTN = 1024
TK_ATTN = 512
TM_OUT = 256
MOE_ROWS = 256
MOE_VMEM_STRIDE = 20
TM_COMB = 256

BF16 = jnp.bfloat16
F32 = jnp.float32
LOG2_E = 1.4426950408889634


def _params(*sem):
    return pltpu.CompilerParams(dimension_semantics=sem, vmem_limit_bytes=VMEM_LIMIT)


def _rope_kernel(pos_ref, inv_ref, sgn_ref, c_ref, s_ref):
    ang = pos_ref[...].astype(F32) * inv_ref[...]
    live = sgn_ref[...] != 0.0
    c_ref[...] = jnp.where(live, jnp.cos(ang), 0.0)
    s_ref[...] = jnp.sin(ang) * sgn_ref[...]


def _rope_tables(positions):
    n = positions.size
    tm = min(n, 1024)
    half = MLA_ROPE_DIM // 2
    inv = 1.0 / (ROPE_THETA ** (jnp.arange(0, MLA_ROPE_DIM, 2, dtype=F32) / MLA_ROPE_DIM))
    zeros = jnp.zeros((LANES - 2 * half,), F32)
    inv_t = jnp.concatenate([inv, inv, zeros])[None, :]
    sgn_t = jnp.concatenate([-jnp.ones((half,), F32), jnp.ones((half,), F32), zeros])[None, :]
    row = pl.BlockSpec((1, LANES), lambda i: (0, 0))
    return pl.pallas_call(
        _rope_kernel,
        out_shape=(jax.ShapeDtypeStruct((n, LANES), F32),) * 2,
        grid=(n // tm,),
        in_specs=[pl.BlockSpec((tm, 1), lambda i: (i, 0)), row, row],
        out_specs=(pl.BlockSpec((tm, LANES), lambda i: (i, 0)),) * 2,
        compiler_params=_params("parallel"),
        name="rope_tables",
    )(positions.reshape(n, 1), inv_t, sgn_t)


def _inproj_kernel(x_ref, w_ref, o_ref):
    o_ref[...] = jnp.dot(x_ref[...].astype(BF16), w_ref[0], preferred_element_type=F32)


def _inproj(x, w, l):
    n, d = x.shape
    tm = min(n, TM_INPROJ)
    return pl.pallas_call(
        _inproj_kernel,
        out_shape=jax.ShapeDtypeStruct((n, D_IN_PAD), F32),
        grid=(D_IN_PAD // TN_INPROJ, n // tm),
        in_specs=[pl.BlockSpec((tm, d), lambda j, i: (i, 0)),
                  pl.BlockSpec((1, d, TN_INPROJ), lambda j, i: (l, 0, j))],
        out_specs=pl.BlockSpec((tm, TN_INPROJ), lambda j, i: (i, j)),
        compiler_params=_params("parallel", "parallel"),
        name="in_proj",
    )(x, w)


def _sgu_kernel(u_ref, v_ref, g_ref, b_ref, ws_ref, bs_ref, o_ref):
    u = jax.nn.gelu(u_ref[...])
    v = jax.nn.gelu(v_ref[...])
    mu = jnp.mean(v, axis=-1, keepdims=True)
    vc = v - mu
    var = jnp.mean(vc * vc, axis=-1, keepdims=True)
    vn = (vc * lax.rsqrt(var + NORM_EPS) * g_ref[...] + b_ref[...]).astype(BF16)
    row = lax.broadcasted_iota(jnp.int32, (CHUNK, CHUNK), 0)
    col = lax.broadcasted_iota(jnp.int32, (CHUNK, CHUNK), 1)
    causal = row >= col
    bs = bs_ref[...]
    for g in range(SGU_GROUPS):
        w = jnp.where(causal, ws_ref[g], 0.0).astype(BF16)
        cs = slice(g * SGU_GROUP_DIM, (g + 1) * SGU_GROUP_DIM)
        for c in range(u.shape[0] // CHUNK):
            rs = slice(c * CHUNK, (c + 1) * CHUNK)
            mixed = jnp.dot(w, vn[rs, cs], preferred_element_type=F32) + bs[:, g:g + 1]
            o_ref[rs, cs] = (u[rs, cs] * mixed).astype(o_ref.dtype)


def _sgu(h, ln_g, ln_b, ws, bs):
    n = h.shape[0]
    tm = min(n, TM_SGU)
    full = lambda shape: pl.BlockSpec(shape, lambda i: (0,) * len(shape))
    return pl.pallas_call(
        _sgu_kernel,
        out_shape=jax.ShapeDtypeStruct((n, SGU_WIDTH), BF16),
        grid=(n // tm,),
        in_specs=[pl.BlockSpec((tm, SGU_WIDTH), lambda i: (i, 0)),
                  pl.BlockSpec((tm, SGU_WIDTH), lambda i: (i, 1)),
                  full((1, SGU_WIDTH)), full((1, SGU_WIDTH)),
                  full((SGU_GROUPS, CHUNK, CHUNK)), full((CHUNK, SGU_GROUPS))],
        out_specs=pl.BlockSpec((tm, SGU_WIDTH), lambda i: (i, 0)),
        compiler_params=_params("parallel"),
        name="sgu",
    )(h, h, ln_g[None, :], ln_b[None, :], ws, bs.T)


def _dot_nt(a, b):
    return lax.dot_general(a, b, (((1,), (1,)), ((), ())), preferred_element_type=F32)


def _hgrn_kernel(q_ref, f_ref, i_ref, g_ref, la_ref, lc_ref, om_ref, ng_ref, o_ref, state_ref):
    @pl.when(pl.program_id(1) == 0)
    def _():
        state_ref[...] = jnp.zeros_like(state_ref)

    T, K = CHUNK, HGRN_HEAD_DIM
    fx = f_ref[...]
    log_f = jnp.logaddexp(la_ref[...], lc_ref[...] + jax.nn.log_sigmoid(fx))
    key = om_ref[...] * jax.nn.sigmoid(-fx)
    qry = jax.nn.silu(q_ref[...])
    val = i_ref[...]
    gate = jax.nn.silu(g_ref[...])

    row = lax.broadcasted_iota(jnp.int32, (T, T), 0)
    col = lax.broadcasted_iota(jnp.int32, (T, T), 1)
    tril = (row >= col).astype(F32)
    G_all = jnp.dot(tril, log_f, preferred_element_type=F32, precision=lax.Precision.HIGHEST)

    SUB = 8
    for h in range(HGRN_HEADS):
        cs = slice(h * K, (h + 1) * K)
        G, q, k, v = G_all[:, cs], qry[:, cs], key[:, cs], val[:, cs]
        A = jnp.zeros((T, T), F32)
        m = T // 2
        while m >= SUB:
            blk = 2 * m
            g_mid = jnp.concatenate(
                [jnp.broadcast_to(G[b * blk + m - 1:b * blk + m, :], (blk, K)) for b in range(T // blk)], axis=0)
            lower = (row % blk) >= m
            qs = jnp.where(lower, q * jnp.exp(G - g_mid), 0.0).astype(BF16)
            ks = jnp.where(lower, 0.0, k * jnp.exp(g_mid - G)).astype(BF16)
            same = (row // blk) == (col // blk)
            A = A + jnp.where(same, _dot_nt(qs, ks), 0.0)
            m //= 2
        rmod = row % SUB
        for d in range(SUB):
            if d == 0:
                prod = q * k
            else:
                kd = pltpu.roll(k, d, axis=0)
                gd = pltpu.roll(G, d, axis=0)
                prod = jnp.where(rmod >= d, q * kd * jnp.exp(G - gd), 0.0)
            A = A + jnp.where(col == row - d, jnp.sum(prod, axis=-1, keepdims=True), 0.0)

        st = state_ref[h]
        o = jnp.dot(A.astype(BF16), v.astype(BF16), preferred_element_type=F32)
        o = o + _dot_nt((q * jnp.exp(G)).astype(BF16), st.astype(BF16))
        g_last = G[T - 1:T, :]
        k_dec = (k * jnp.exp(g_last - G)).astype(BF16)
        state_ref[h] = jnp.exp(g_last) * st + jnp.dot(v.T.astype(BF16), k_dec, preferred_element_type=F32)

        ms = jnp.mean(o * o, axis=-1, keepdims=True)
        o = o * lax.rsqrt(ms + NORM_EPS) * ng_ref[:, cs]
        o_ref[:, cs] = (o * gate[:, cs]).astype(o_ref.dtype)


def _hgrn(h, lb, norm_g, batch):
    n = h.shape[0]
    nc = n // batch // CHUNK
    W = HGRN_WIDTH
    col = lambda j: pl.BlockSpec((CHUNK, W), lambda b, c: (b * nc + c, j))
    vec = pl.BlockSpec((1, W), lambda b, c: (0, 0))
    base = 2 * SGU_WIDTH // W
    return pl.pallas_call(
        _hgrn_kernel,
        out_shape=jax.ShapeDtypeStruct((n, W), BF16),
        grid=(batch, nc),
        in_specs=[col(base), col(base + 1), col(base + 2), col(base + 3), vec, vec, vec, vec],
        out_specs=pl.BlockSpec((CHUNK, W), lambda b, c: (b * nc + c, 0)),
        scratch_shapes=[pltpu.VMEM((HGRN_HEADS, HGRN_HEAD_DIM, HGRN_HEAD_DIM), F32)],
        compiler_params=_params("parallel", "arbitrary"),
        name="hgrn2",
    )(h, h, h, h, jnp.log(lb)[None, :], jnp.log1p(-lb)[None, :], (1.0 - lb)[None, :], norm_g[None, :])


def _rms(x, g):
    ms = jnp.mean(x * x, axis=-1, keepdims=True)
    return x * lax.rsqrt(ms + NORM_EPS) * g


def _rope(x, c, s):
    half = MLA_ROPE_DIM // 2
    lane = lax.broadcasted_iota(jnp.int32, x.shape, 1)
    swapped = jnp.where(lane < half, pltpu.roll(x, LANES - half, axis=1), pltpu.roll(x, half, axis=1))
    return x * c + swapped * s


def _mla_up_kernel(cq_ref, ckv_ref, kr_ref, c_ref, s_ref, qg_ref, kvg_ref, wq_ref, wkv_ref,
                   q_ref, kn_ref, v_ref, kro_ref):
    c, s = c_ref[...], s_ref[...]
    scale = MLA_QK_DIM ** -0.5 * LOG2_E
    q = jnp.dot(_rms(cq_ref[...], qg_ref[...]).astype(BF16), wq_ref[0], preferred_element_type=F32)
    kv = jnp.dot(_rms(ckv_ref[...], kvg_ref[...]).astype(BF16), wkv_ref[0], preferred_element_type=F32)
    for h in range(MLA_HEADS):
        base = h * Q_HEAD_PAD
        q_ref[h, :, :LANES] = (q[:, base:base + LANES] * scale).astype(BF16)
        q_ref[h, :, LANES:] = (_rope(q[:, base + LANES:base + 2 * LANES], c, s) * scale).astype(BF16)
        kbase = h * (MLA_NOPE_DIM + MLA_V_DIM)
        kn_ref[h] = kv[:, kbase:kbase + MLA_NOPE_DIM].astype(BF16)
        v_ref[h] = kv[:, kbase + MLA_NOPE_DIM:kbase + MLA_NOPE_DIM + MLA_V_DIM].astype(BF16)
    kro_ref[...] = _rope(kr_ref[...], c, s).astype(BF16)


def _mla_up(h, tab_c, tab_s, qn_g, w_uq, kvn_g, w_ukv, l):
    n = h.shape[0]
    tm = min(n, TM_MLA)
    H = MLA_HEADS
    cq_blk = (2 * SGU_WIDTH + 4 * HGRN_WIDTH) // Q_LORA_RANK
    kr_blk = (2 * SGU_WIDTH + 4 * HGRN_WIDTH + Q_LORA_RANK + KV_LORA_RANK) // LANES
    full = lambda shape: pl.BlockSpec(shape, lambda i: (0,) * len(shape))
    heads = lambda w: pl.BlockSpec((H, tm, w), lambda i: (0, i, 0))
    return pl.pallas_call(
        _mla_up_kernel,
        out_shape=(jax.ShapeDtypeStruct((H, n, Q_HEAD_PAD), BF16),
                   jax.ShapeDtypeStruct((H, n, MLA_NOPE_DIM), BF16),
                   jax.ShapeDtypeStruct((H, n, MLA_V_DIM), BF16),
                   jax.ShapeDtypeStruct((n, LANES), BF16)),
        grid=(n // tm,),
        in_specs=[pl.BlockSpec((tm, Q_LORA_RANK), lambda i: (i, cq_blk)),
                  pl.BlockSpec((tm, KV_LORA_RANK), lambda i: (i, cq_blk + 1)),
                  pl.BlockSpec((tm, LANES), lambda i: (i, kr_blk)),
                  pl.BlockSpec((tm, LANES), lambda i: (i, 0)),
                  pl.BlockSpec((tm, LANES), lambda i: (i, 0)),
                  full((1, Q_LORA_RANK)), full((1, KV_LORA_RANK)),
                  pl.BlockSpec((1, Q_LORA_RANK, H * Q_HEAD_PAD), lambda i: (l, 0, 0)),
                  pl.BlockSpec((1, KV_LORA_RANK, H * (MLA_NOPE_DIM + MLA_V_DIM)), lambda i: (l, 0, 0))],
        out_specs=(heads(Q_HEAD_PAD), heads(MLA_NOPE_DIM), heads(MLA_V_DIM),
                   pl.BlockSpec((tm, LANES), lambda i: (i, 0))),
        compiler_params=_params("parallel"),
        name="mla_up",
    )(h, h, h, tab_c, tab_s, qn_g[None, :], kvn_g[None, :], w_uq, w_ukv)


def _attn_kernel(q_ref, kn_ref, kr_ref, v_ref, o_ref, m_ref, l_ref, acc_ref):
    i = pl.program_id(2)
    nch, tk, _ = acc_ref.shape
    m_ref[...] = jnp.full_like(m_ref, -jnp.inf)
    l_ref[...] = jnp.zeros_like(l_ref)
    acc_ref[...] = jnp.zeros_like(acc_ref)

    def chain(c, j, masked):
        start = pl.multiple_of(j * tk, tk)
        q = q_ref[0, c * tk:(c + 1) * tk, :]
        k = jnp.concatenate([kn_ref[0, pl.ds(start, tk), :], kr_ref[pl.ds(start, tk), :]], axis=-1)
        s = _dot_nt(q, k)
        if masked:
            row = lax.broadcasted_iota(jnp.int32, s.shape, 0)
            col = lax.broadcasted_iota(jnp.int32, s.shape, 1)
            s = jnp.where(col <= row, s, -jnp.inf)
        chunks = [s[:, t * LANES:(t + 1) * LANES] for t in range(tk // LANES)]
        smax = functools.reduce(jnp.maximum, chunks)
        m_prev = m_ref[c]
        m_new = jnp.maximum(m_prev, jnp.max(smax, axis=-1, keepdims=True))
        alpha = jnp.exp2(m_prev - m_new)
        ps = [jnp.exp2(ch - m_new) for ch in chunks]
        l_ref[c] = alpha * l_ref[c] + functools.reduce(jnp.add, ps)
        p = jnp.concatenate([x.astype(BF16) for x in ps], axis=-1)
        acc_ref[c] = alpha * acc_ref[c] + jnp.dot(p, v_ref[0, pl.ds(start, tk), :], preferred_element_type=F32)
        m_ref[c] = m_new

    def body(j, carry):
        for c in range(nch):
            chain(c, j, masked=False)
        return carry

    lax.fori_loop(0, nch * i, body, 0)
    for t in range(nch):
        for c in range(t, nch):
            chain(c, nch * i + t, masked=(c == t))
    for c in range(nch):
        l = jnp.sum(l_ref[c], axis=-1, keepdims=True)
        o_ref[c * tk:(c + 1) * tk, :] = (acc_ref[c] / l).astype(o_ref.dtype)


def _attention(q, kn, kr, v, batch):
    H, n, _ = q.shape
    S = n // batch
    tq = min(S, TQ_ATTN)
    tk = min(tq, TK_ATTN)
    nq = S // tq
    return pl.pallas_call(
        _attn_kernel,
        out_shape=jax.ShapeDtypeStruct((n, H * MLA_V_DIM), BF16),
        grid=(batch, H, nq),
        in_specs=[pl.BlockSpec((1, tq, Q_HEAD_PAD), lambda b, h, i: (h, b * nq + i, 0)),
                  pl.BlockSpec((1, S, MLA_NOPE_DIM), lambda b, h, i: (h, b, 0)),
                  pl.BlockSpec((S, LANES), lambda b, h, i: (b, 0)),
                  pl.BlockSpec((1, S, MLA_V_DIM), lambda b, h, i: (h, b, 0))],
        out_specs=pl.BlockSpec((tq, MLA_V_DIM), lambda b, h, i: (b * nq + i, h)),
        scratch_shapes=[pltpu.VMEM((tq // tk, tk, LANES), F32), pltpu.VMEM((tq // tk, tk, LANES), F32),
                        pltpu.VMEM((tq // tk, tk, MLA_V_DIM), F32)],
        compiler_params=_params("parallel", "parallel", "arbitrary"),
        name="mla_attention",
    )(q, kn, kr, v)


def _layer_norm(x, g, b):
    mu = jnp.mean(x, axis=-1, keepdims=True)
    xc = x - mu
    var = jnp.mean(xc * xc, axis=-1, keepdims=True)
    return xc * lax.rsqrt(var + NORM_EPS) * g + b


def _first_index(hit, lane):
    return jnp.min(jnp.where(hit, lane, LANES), axis=-1, keepdims=True)


def _route(logits):
    lane = lax.broadcasted_iota(jnp.int32, logits.shape, 1)
    neg = -jnp.inf
    gl = jnp.where((lane >= N_EXPERTS) & (lane < N_EXPERTS + N_GROUPS), logits, neg)
    ge = jnp.exp(gl - jnp.max(gl, axis=-1, keepdims=True))
    gp = ge / jnp.sum(ge, axis=-1, keepdims=True)
    g_val = jnp.max(gp, axis=-1, keepdims=True)
    g_idx = _first_index(gp == g_val, lane) - N_EXPERTS
    el = jnp.where((lane // EXPERTS_PER_GROUP == g_idx) & (lane < N_EXPERTS), logits, neg)
    m1 = jnp.max(el, axis=-1, keepdims=True)
    i1 = _first_index(el == m1, lane)
    el2 = jnp.where(lane == i1, neg, el)
    m2 = jnp.max(el2, axis=-1, keepdims=True)
    i2 = _first_index(el2 == m2, lane)
    e2 = jnp.exp(m2 - m1)
    den = 1.0 + e2
    w1 = g_val * (1.0 / den)
    w2 = g_val * (e2 / den)
    return w1, w2, i1, i2


def _rows_to_slabs(ref, x, stride):
    m, d = x.shape
    for s in range(d // LANES):
        ref[pl.ds(s, m, stride=stride), :] = x[:, s * LANES:(s + 1) * LANES]


def _slabs_to_rows(ref, m, d, stride):
    return jnp.concatenate([ref[pl.ds(s, m, stride=stride), :] for s in range(d // LANES)], axis=-1)


def _outproj_kernel(ya_ref, yb_ref, yc_ref, x_ref, w_ref, g_ref, b_ref, wr_ref, br_ref,
                    x1_ref, x1s_ref, rt_ref, cnt_ref):
    @pl.when(pl.program_id(0) == 0)
    def _():
        cnt_ref[...] = jnp.zeros_like(cnt_ref)

    wa = SGU_WIDTH
    wb = SGU_WIDTH + HGRN_WIDTH
    mix = jnp.dot(ya_ref[...], w_ref[0, :wa, :], preferred_element_type=F32)
    mix = mix + jnp.dot(yb_ref[...], w_ref[0, wa:wb, :], preferred_element_type=F32)
    mix = mix + jnp.dot(yc_ref[...], w_ref[0, wb:, :], preferred_element_type=F32)
    x1 = _layer_norm(DEEPNORM_ALPHA * x_ref[...] + mix, g_ref[...], b_ref[...])
    x1_ref[...] = x1
    _rows_to_slabs(x1s_ref, x1, x1.shape[1] // LANES)
    x_hi = x1.astype(BF16)
    x_lo = (x1 - x_hi.astype(F32)).astype(BF16)
    r = jnp.dot(x_hi, wr_ref[...], preferred_element_type=F32)
    logits = r[:, :LANES] + r[:, LANES:] + jnp.dot(x_lo, wr_ref[:, :LANES], preferred_element_type=F32) + br_ref[...]
    w1, w2, i1, i2 = _route(logits)

    tm = logits.shape[0]
    lane = lax.broadcasted_iota(jnp.int32, logits.shape, 1)
    hit1, hit2 = lane == i1, lane == i2
    oh1, oh2 = hit1.astype(F32), hit2.astype(F32)
    row = lax.broadcasted_iota(jnp.int32, (tm, tm), 0)
    col = lax.broadcasted_iota(jnp.int32, (tm, tm), 1)
    before = (col < row).astype(BF16)
    pre = jnp.dot(before, jnp.concatenate([oh1, oh2], axis=-1).astype(BF16), preferred_element_type=F32)
    tot1 = jnp.sum(oh1, axis=0, keepdims=True)
    tot2 = jnp.sum(oh2, axis=0, keepdims=True)
    carry = cnt_ref[...]
    rank1 = jnp.sum(jnp.where(hit1, pre[:, :LANES] + carry, 0.0), axis=-1, keepdims=True)
    rank2 = jnp.sum(jnp.where(hit2, pre[:, LANES:] + (carry + tot1), 0.0), axis=-1, keepdims=True)
    cnt_ref[...] = carry + tot1 + tot2

    out = jnp.where(lane == 0, w1, 0.0)
    out = jnp.where(lane == 1, w2, out)
    out = jnp.where(lane == 2, i1.astype(F32), out)
    out = jnp.where(lane == 3, i2.astype(F32), out)
    out = jnp.where(lane == 4, rank1, out)
    out = jnp.where(lane == 5, rank2, out)
    rt_ref[...] = out


def _outproj(ya, yb, yc, x, w_out, ln_g, ln_b, wr, br, l):
    n, d = x.shape
    tm = min(n, TM_OUT)
    slab = d // LANES
    full = lambda shape: pl.BlockSpec(shape, lambda i: (0,) * len(shape))
    rows = lambda w: pl.BlockSpec((tm, w), lambda i: (i, 0))
    wr_hi = wr.astype(BF16)
    wr_lo = (wr - wr_hi.astype(F32)).astype(BF16)
    return pl.pallas_call(
        _outproj_kernel,
        out_shape=(jax.ShapeDtypeStruct((n, d), F32), jax.ShapeDtypeStruct((n * slab, LANES), F32),
                   jax.ShapeDtypeStruct((n, LANES), F32), jax.ShapeDtypeStruct((1, LANES), F32)),
        grid=(n // tm,),
        in_specs=[rows(ya.shape[1]), rows(yb.shape[1]), rows(yc.shape[1]), rows(d),
                  pl.BlockSpec((1, d, d), lambda i: (l, 0, 0)), full((1, d)), full((1, d)),
                  full((d, 2 * LANES)), full((1, LANES))],
        out_specs=(rows(d), pl.BlockSpec((tm * slab, LANES), lambda i: (i, 0)), rows(LANES), full((1, LANES))),
        compiler_params=_params("arbitrary"),
        name="out_proj_ln_router",
    )(ya, yb, yc, x, w_out, ln_g[None, :], ln_b[None, :], jnp.concatenate([wr_hi, wr_lo], axis=1), br)


def _moe_kernel(src_ref, dst_ref, exp_ref, used_ref, x_hbm, wg_ref, wu_ref, wd_ref, y_hbm,
                xbuf0, xbuf1, ybuf0, ybuf1, wgb, wub, wdb, gsem, ssem):
    xbufs, ybufs = (xbuf0, xbuf1), (ybuf0, ybuf1)
    i = pl.program_id(0)
    nb = pl.num_programs(0) - 1
    d = wgb.shape[0]
    slab = d // LANES
    vs = MOE_VMEM_STRIDE
    rows = xbuf0.shape[0] // vs
    used = used_ref[0]

    def gather_copy(tok, r, slot):
        src = x_hbm.at[pl.ds(pl.multiple_of(tok * slab, slab), slab), :]
        return pltpu.make_async_copy(src, xbufs[slot].at[pl.ds(r * vs, slab), :], gsem.at[slot])

    def scatter_copy(dst, r, slot):
        out = y_hbm.at[pl.ds(pl.multiple_of(dst * slab, slab), slab), :]
        return pltpu.make_async_copy(ybufs[slot].at[pl.ds(r * vs, slab), :], out, ssem.at[slot])

    def wait_gather(slot):
        pltpu.make_async_copy(x_hbm.at[pl.ds(0, rows * slab), :], xbuf0.at[pl.ds(0, rows * slab), :],
                              gsem.at[slot]).wait()

    def wait_scatter(slot):
        pltpu.make_async_copy(ybuf0.at[pl.ds(0, rows * slab), :], y_hbm.at[pl.ds(0, rows * slab), :],
                              ssem.at[slot]).wait()

    @pl.when(i == 0)
    def _():
        ybuf1[...] = jnp.zeros_like(ybuf1)
        n_real = y_hbm.shape[0] // slab - 3 * rows
        fills = [pltpu.make_async_copy(ybuf1.at[pl.ds(0, rows * slab), :],
                                       y_hbm.at[pl.ds((n_real + (1 + t) * rows) * slab, rows * slab), :], ssem.at[2])
                 for t in range(2)]
        for f in fills:
            f.start()
        for f in fills:
            f.wait()

        def issue(r, carry):
            gather_copy(src_ref[r], r, 0).start()
            return carry
        lax.fori_loop(0, rows, issue, 0, unroll=8)

    @pl.when((i >= 1) & (i <= used))
    def _():
        wait_scatter(i % 2)

    cur = jnp.minimum(i, nb - 1)
    @pl.when((i < used) & ((i == 0) | (exp_ref[cur] != exp_ref[jnp.maximum(cur - 1, 0)])))
    def _():
        wgb[...] = wg_ref[0, 0].astype(BF16)
        wub[...] = wu_ref[0, 0].astype(BF16)
        wdb[...] = wd_ref[0, 0].astype(BF16)

    def compute_block(slot):
        other = 1 - slot
        wait_gather(slot)
        nxt = jnp.minimum(i + 1, nb - 1) * rows
        prv = i * rows
        for r in range(rows):
            gather_copy(src_ref[nxt + r], r, other).start()
            scatter_copy(dst_ref[prv + r], r, other).start()
        xb = _slabs_to_rows(xbufs[slot], rows, d, vs).astype(BF16)
        hg = jnp.dot(xb, wgb[...], preferred_element_type=F32)
        hu = jnp.dot(xb, wub[...], preferred_element_type=F32)
        hh = (jax.nn.silu(hg) * hu).astype(BF16)
        _rows_to_slabs(ybufs[slot], jnp.dot(hh, wdb[...], preferred_element_type=F32), vs)

    def drain_block(slot):
        def issue(r, carry):
            scatter_copy(dst_ref[i * rows + r], r, slot).start()
            return carry
        lax.fori_loop(0, rows, issue, 0, unroll=8)
        wait_scatter(slot)

    for parity in range(2):
        pl.when((i < used) & (i % 2 == parity))(functools.partial(compute_block, parity))

    @pl.when(i == used)
    def _():
        wait_gather(i % 2)

    for parity in range(2):
        pl.when((i == used) & (i >= 1) & ((i + 1) % 2 == parity))(functools.partial(drain_block, parity))


def _moe(slot_src, slot_dst, blk_exp, used, x1s, w_gate, w_up, w_down, n_out, l):
    d, de = w_gate.shape[2:]
    slab = d // LANES
    nb = blk_exp.shape[0]
    rows = slot_src.shape[0] // nb
    blk = lambda i, src, dst, exp, used: (l, exp[jnp.minimum(i, nb - 1)], 0, 0)
    return pl.pallas_call(
        _moe_kernel,
        out_shape=jax.ShapeDtypeStruct((n_out * slab, LANES), F32),
        grid_spec=pltpu.PrefetchScalarGridSpec(
            num_scalar_prefetch=4,
            grid=(nb + 1,),
            in_specs=[pl.BlockSpec(memory_space=pl.ANY),
                      pl.BlockSpec((1, 1, d, de), blk), pl.BlockSpec((1, 1, d, de), blk),
                      pl.BlockSpec((1, 1, de, d), blk)],
            out_specs=pl.BlockSpec(memory_space=pl.ANY),
            scratch_shapes=[pltpu.VMEM((rows * MOE_VMEM_STRIDE, LANES), F32)] * 4 + [
                            pltpu.VMEM((d, de), BF16), pltpu.VMEM((d, de), BF16), pltpu.VMEM((de, d), BF16),
                            pltpu.SemaphoreType.DMA((2,)), pltpu.SemaphoreType.DMA((3,))]),
        compiler_params=_params("arbitrary"),
        name="moe_experts",
    )(slot_src, slot_dst, blk_exp, used, x1s, w_gate, w_up, w_down)


def _combine_kernel(y0_ref, y1_ref, x_ref, rt_ref, g_ref, b_ref, o_ref):
    tm, d = x_ref.shape
    slab = d // LANES
    rt = rt_ref[...]
    ffn = _slabs_to_rows(y0_ref, tm, d, slab) * rt[:, 0:1] + _slabs_to_rows(y1_ref, tm, d, slab) * rt[:, 1:2]
    o_ref[...] = _layer_norm(DEEPNORM_ALPHA * x_ref[...] + ffn, g_ref[...], b_ref[...])


def _combine(y_tok, x1, route, ln_g, ln_b):
    n, d = x1.shape
    tm = min(n, TM_COMB)
    slab = d // LANES
    return pl.pallas_call(
        _combine_kernel,
        out_shape=jax.ShapeDtypeStruct((n, d), F32),
        grid=(n // tm,),
        in_specs=[pl.BlockSpec((tm * slab, LANES), lambda i: (i, 0)),
                  pl.BlockSpec((tm * slab, LANES), lambda i: (i + n // tm, 0)),
                  pl.BlockSpec((tm, d), lambda i: (i, 0)),
                  pl.BlockSpec((tm, LANES), lambda i: (i, 0)),
                  pl.BlockSpec((1, d), lambda i: (0, 0)),
                  pl.BlockSpec((1, d), lambda i: (0, 0))],
        out_specs=pl.BlockSpec((tm, d), lambda i: (i, 0)),
        compiler_params=_params("parallel"),
        name="moe_combine_ln",
    )(y_tok, y_tok, x1, route, ln_g[None, :], ln_b[None, :])


def _dispatch_plan(route, counts, rows):
    n = route.shape[0]
    nk = n * TOP_K
    nb = (nk + rows - 1) // rows + N_EXPERTS
    counts = counts[0, :N_EXPERTS].astype(jnp.int32)
    padded = ((counts + rows - 1) // rows) * rows
    pad_end = jnp.cumsum(padded)
    pad_start = pad_end - padded
    flat_e = route[:, 2:2 + TOP_K].astype(jnp.int32).reshape(nk)
    rank = route[:, 2 + TOP_K:2 + 2 * TOP_K].astype(jnp.int32).reshape(nk)
    dest = pad_start[flat_e] + rank
    asg = jnp.full((nb * rows,), -1, jnp.int32).at[dest].set(jnp.arange(nk, dtype=jnp.int32))
    slot = jnp.arange(nb * rows, dtype=jnp.int32)
    dump = nk + rows + ((slot // rows) % 2) * rows + slot % rows
    slot_src = jnp.where(asg >= 0, asg // TOP_K, 0)
    out_row = (asg % TOP_K) * n + asg // TOP_K
    slot_dst = jnp.concatenate([nk + jnp.arange(rows, dtype=jnp.int32), jnp.where(asg >= 0, out_row, dump)])
    blk_exp = jnp.minimum(
        jnp.sum((pad_end[None, :] <= (jnp.arange(nb, dtype=jnp.int32) * rows)[:, None]).astype(jnp.int32), axis=1),
        N_EXPERTS - 1).astype(jnp.int32)
    used = (pad_end[-1] // rows).astype(jnp.int32).reshape(1)
    return slot_src, slot_dst, blk_exp, used, nk + 3 * rows


def _pad_q_heads(w_uq):
    lead = w_uq.shape[:-1]
    w = w_uq.reshape(*lead, MLA_HEADS, MLA_QK_DIM)
    w = jnp.pad(w, [(0, 0)] * len(lead) + [(0, 0), (0, Q_HEAD_PAD - MLA_QK_DIM)])
    return w.reshape(*lead, MLA_HEADS * Q_HEAD_PAD)


def kernel(x, positions, w_in, sgu_ln_g, sgu_ln_b, sgu_ws, sgu_b, hgrn_lb_logits, hgrn_norm_g, mla_qn_g, mla_w_uq, mla_kvn_g, mla_w_ukv, w_out, ln1_g, ln1_b, router_group_w, router_group_b, router_expert_w, router_expert_b, expert_w_gate, expert_w_up, expert_w_down, ln2_g, ln2_b):
    B, S, D = x.shape
    n = B * S
    depth = w_in.shape[0]
    tab_c, tab_s = _rope_tables(positions)
    lb_cum = jnp.cumsum(jax.nn.softmax(hgrn_lb_logits.astype(F32), axis=0), axis=0)
    lower_bounds = lb_cum - lb_cum[0:1]
    xt = x.reshape(n, D)
    w_in_b = jnp.pad(w_in, ((0, 0), (0, 0), (0, D_IN_PAD - D_IN))).astype(BF16)
    w_uq_b = _pad_q_heads(mla_w_uq).astype(BF16)
    w_ukv_b = mla_w_ukv.astype(BF16)
    w_out_b = w_out.astype(BF16)
    for l in range(depth):
        h = _inproj(xt, w_in_b, l)
        y_a = _sgu(h, sgu_ln_g[l], sgu_ln_b[l], sgu_ws[l], sgu_b[l])
        y_b = _hgrn(h, lower_bounds[l], hgrn_norm_g[l], B)
        q, kn, v, kr = _mla_up(h, tab_c, tab_s, mla_qn_g[l], w_uq_b, mla_kvn_g[l], w_ukv_b, l)
        y_c = _attention(q, kn, kr, v, B)
        wr = jnp.pad(jnp.concatenate([router_expert_w[l], router_group_w[l]], axis=1),
                     ((0, 0), (0, LANES - N_EXPERTS - N_GROUPS)))
        br = jnp.pad(jnp.concatenate([router_expert_b[l], router_group_b[l]]), (0, LANES - N_EXPERTS - N_GROUPS))
        x1, x1s, route, counts = _outproj(y_a, y_b, y_c, xt, w_out_b, ln1_g[l], ln1_b[l], wr, br[None, :], l)
        slot_src, slot_dst, blk_exp, used, n_out = _dispatch_plan(route, counts, MOE_ROWS)
        y_tok = _moe(slot_src, slot_dst, blk_exp, used, x1s, expert_w_gate, expert_w_up, expert_w_down, n_out, l)
        xt = _combine(y_tok, x1, route, ln2_g[l], ln2_b[l])
    return xt.reshape(B, S, D)
```

```python
import functools

import jax
import jax.numpy as jnp
from jax import lax
from jax.experimental import pallas as pl
from jax.experimental.pallas import tpu as pltpu

D_MODEL = 2048
DEPTH = 4
CHUNK = 128
SGU_WIDTH = D_MODEL // 4
SGU_GROUPS = 4
SGU_GROUP_DIM = SGU_WIDTH // SGU_GROUPS
HGRN_WIDTH = D_MODEL // 4
HGRN_HEADS = 4
HGRN_HEAD_DIM = HGRN_WIDTH // HGRN_HEADS
MLA_HEADS = 8
MLA_V_DIM = (D_MODEL - SGU_WIDTH - HGRN_WIDTH) // MLA_HEADS
MLA_NOPE_DIM = 128
MLA_ROPE_DIM = 64
MLA_QK_DIM = MLA_NOPE_DIM + MLA_ROPE_DIM
Q_LORA_RANK = D_MODEL // 4
KV_LORA_RANK = D_MODEL // 4
ROPE_THETA = 10000.0
N_GROUPS = 4
EXPERTS_PER_GROUP = 8
N_EXPERTS = N_GROUPS * EXPERTS_PER_GROUP
TOP_K = 2
D_EXPERT = D_MODEL // 4
NORM_EPS = 1e-5
DEEPNORM_ALPHA = (2 * DEPTH) ** 0.25
D_IN = SGU_WIDTH * 2 + HGRN_WIDTH * 4 + Q_LORA_RANK + KV_LORA_RANK + MLA_ROPE_DIM

LANES = 128
D_IN_PAD = ((D_IN + LANES - 1) // LANES) * LANES
Q_HEAD_PAD = 2 * LANES
VMEM_LIMIT = 56 * 1024 * 1024

TM_INPROJ = 512
TN_INPROJ = D_IN_PAD // 3
TM_SGU = 512
TM_MLA = 512
TQ_ATTN = 2048
TK_ATTN = 512
TM_OUT = 512
OUT_CHAIN_ROWS = 256
MOE_ROWS = 256
MOE_RING = 4
MOE_VMEM_STRIDE = 20
TM_COMB = 256

BF16 = jnp.bfloat16
F32 = jnp.float32
LOG2_E = 1.4426950408889634


def _params(*sem):
    return pltpu.CompilerParams(dimension_semantics=sem, vmem_limit_bytes=VMEM_LIMIT)


def _rope_kernel(pos_ref, inv_ref, sgn_ref, c_ref, s_ref):
    ang = pos_ref[...].astype(F32) * inv_ref[...]
    live = sgn_ref[...] != 0.0
    c_ref[...] = jnp.where(live, jnp.cos(ang), 0.0)
    s_ref[...] = jnp.sin(ang) * sgn_ref[...]


def _rope_tables(positions):
    n = positions.size
    tm = min(n, 1024)
    half = MLA_ROPE_DIM // 2
    inv = 1.0 / (ROPE_THETA ** (jnp.arange(0, MLA_ROPE_DIM, 2, dtype=F32) / MLA_ROPE_DIM))
    zeros = jnp.zeros((LANES - 2 * half,), F32)
    inv_t = jnp.concatenate([inv, inv, zeros])[None, :]
    sgn_t = jnp.concatenate([-jnp.ones((half,), F32), jnp.ones((half,), F32), zeros])[None, :]
    row = pl.BlockSpec((1, LANES), lambda i: (0, 0))
    return pl.pallas_call(
        _rope_kernel,
        out_shape=(jax.ShapeDtypeStruct((n, LANES), F32),) * 2,
        grid=(n // tm,),
        in_specs=[pl.BlockSpec((tm, 1), lambda i: (i, 0)), row, row],
        out_specs=(pl.BlockSpec((tm, LANES), lambda i: (i, 0)),) * 2,
        compiler_params=_params("parallel"),
        name="rope_tables",
    )(positions.reshape(n, 1), inv_t, sgn_t)


def _inproj_kernel(x_ref, w_ref, o_ref):
    o_ref[...] = jnp.dot(x_ref[...].astype(BF16), w_ref[0], preferred_element_type=F32)


def _inproj(x, w, l):
    n, d = x.shape
    tm = min(n, TM_INPROJ)
    return pl.pallas_call(
        _inproj_kernel,
        out_shape=jax.ShapeDtypeStruct((n, D_IN_PAD), F32),
        grid=(D_IN_PAD // TN_INPROJ, n // tm),
        in_specs=[pl.BlockSpec((tm, d), lambda j, i: (i, 0)),
                  pl.BlockSpec((1, d, TN_INPROJ), lambda j, i: (l, 0, j))],
        out_specs=pl.BlockSpec((tm, TN_INPROJ), lambda j, i: (i, j)),
        compiler_params=_params("parallel", "parallel"),
        name="in_proj",
    )(x, w)


def _sgu_kernel(u_ref, v_ref, g_ref, b_ref, ws_ref, bs_ref, o_ref):
    u = jax.nn.gelu(u_ref[...])
    v = jax.nn.gelu(v_ref[...])
    mu = jnp.mean(v, axis=-1, keepdims=True)
    vc = v - mu
    var = jnp.mean(vc * vc, axis=-1, keepdims=True)
    vn = (vc * lax.rsqrt(var + NORM_EPS) * g_ref[...] + b_ref[...]).astype(BF16)
    row = lax.broadcasted_iota(jnp.int32, (CHUNK, CHUNK), 0)
    col = lax.broadcasted_iota(jnp.int32, (CHUNK, CHUNK), 1)
    causal = row >= col
    bs = bs_ref[...]
    for g in range(SGU_GROUPS):
        w = jnp.where(causal, ws_ref[g], 0.0).astype(BF16)
        cs = slice(g * SGU_GROUP_DIM, (g + 1) * SGU_GROUP_DIM)
        for c in range(u.shape[0] // CHUNK):
            rs = slice(c * CHUNK, (c + 1) * CHUNK)
            mixed = jnp.dot(w, vn[rs, cs], preferred_element_type=F32) + bs[:, g:g + 1]
            o_ref[rs, cs] = (u[rs, cs] * mixed).astype(o_ref.dtype)


def _sgu(h, ln_g, ln_b, ws, bs):
    n = h.shape[0]
    tm = min(n, TM_SGU)
    full = lambda shape: pl.BlockSpec(shape, lambda i: (0,) * len(shape))
    return pl.pallas_call(
        _sgu_kernel,
        out_shape=jax.ShapeDtypeStruct((n, SGU_WIDTH), BF16),
        grid=(n // tm,),
        in_specs=[pl.BlockSpec((tm, SGU_WIDTH), lambda i: (i, 0)),
                  pl.BlockSpec((tm, SGU_WIDTH), lambda i: (i, 1)),
                  full((1, SGU_WIDTH)), full((1, SGU_WIDTH)),
                  full((SGU_GROUPS, CHUNK, CHUNK)), full((CHUNK, SGU_GROUPS))],
        out_specs=pl.BlockSpec((tm, SGU_WIDTH), lambda i: (i, 0)),
        compiler_params=_params("parallel"),
        name="sgu",
    )(h, h, ln_g[None, :], ln_b[None, :], ws, bs.T)


def _dot_nt(a, b):
    return lax.dot_general(a, b, (((1,), (1,)), ((), ())), preferred_element_type=F32)


def _hgrn_kernel(q_ref, f_ref, i_ref, g_ref, la_ref, lc_ref, om_ref, ng_ref, o_ref, state_ref):
    @pl.when(pl.program_id(1) == 0)
    def _():
        state_ref[...] = jnp.zeros_like(state_ref)

    T, K = CHUNK, HGRN_HEAD_DIM
    fx = f_ref[...]
    log_f = jnp.logaddexp(la_ref[...], lc_ref[...] + jax.nn.log_sigmoid(fx))
    key = om_ref[...] * jax.nn.sigmoid(-fx)
    qry = jax.nn.silu(q_ref[...])
    val = i_ref[...]
    gate = jax.nn.silu(g_ref[...])

    row = lax.broadcasted_iota(jnp.int32, (T, T), 0)
    col = lax.broadcasted_iota(jnp.int32, (T, T), 1)
    tril = (row >= col).astype(F32)
    G_all = jnp.dot(tril, log_f, preferred_element_type=F32, precision=lax.Precision.HIGHEST)

    SUB = 8
    for h in range(HGRN_HEADS):
        cs = slice(h * K, (h + 1) * K)
        G, q, k, v = G_all[:, cs], qry[:, cs], key[:, cs], val[:, cs]
        A = jnp.zeros((T, T), F32)
        m = T // 2
        while m >= SUB:
            blk = 2 * m
            g_mid = jnp.concatenate(
                [jnp.broadcast_to(G[b * blk + m - 1:b * blk + m, :], (blk, K)) for b in range(T // blk)], axis=0)
            lower = (row % blk) >= m
            qs = jnp.where(lower, q * jnp.exp(G - g_mid), 0.0).astype(BF16)
            ks = jnp.where(lower, 0.0, k * jnp.exp(g_mid - G)).astype(BF16)
            same = (row // blk) == (col // blk)
            A = A + jnp.where(same, _dot_nt(qs, ks), 0.0)
            m //= 2
        rmod = row % SUB
        k3, G3 = k.reshape(T // SUB, SUB, K), G.reshape(T // SUB, SUB, K)
        for d in range(SUB):
            if d == 0:
                prod = q * k
            else:
                kd = pltpu.roll(k3, d, axis=1).reshape(T, K)
                gd = pltpu.roll(G3, d, axis=1).reshape(T, K)
                prod = jnp.where(rmod >= d, q * kd * jnp.exp(G - gd), 0.0)
            A = A + jnp.where(col == row - d, jnp.sum(prod, axis=-1, keepdims=True), 0.0)

        st = state_ref[h]
        o = jnp.dot(A.astype(BF16), v.astype(BF16), preferred_element_type=F32)
        o = o + _dot_nt((q * jnp.exp(G)).astype(BF16), st.astype(BF16))
        g_last = G[T - 1:T, :]
        k_dec = (k * jnp.exp(g_last - G)).astype(BF16)
        state_ref[h] = jnp.exp(g_last) * st + jnp.dot(v.T.astype(BF16), k_dec, preferred_element_type=F32)

        ms = jnp.mean(o * o, axis=-1, keepdims=True)
        o = o * lax.rsqrt(ms + NORM_EPS) * ng_ref[:, cs]
        o_ref[:, cs] = (o * gate[:, cs]).astype(o_ref.dtype)


def _hgrn(h, lb, norm_g, batch):
    n = h.shape[0]
    nc = n // batch // CHUNK
    W = HGRN_WIDTH
    col = lambda j: pl.BlockSpec((CHUNK, W), lambda b, c: (b * nc + c, j))
    vec = pl.BlockSpec((1, W), lambda b, c: (0, 0))
    base = 2 * SGU_WIDTH // W
    return pl.pallas_call(
        _hgrn_kernel,
        out_shape=jax.ShapeDtypeStruct((n, W), BF16),
        grid=(batch, nc),
        in_specs=[col(base), col(base + 1), col(base + 2), col(base + 3), vec, vec, vec, vec],
        out_specs=pl.BlockSpec((CHUNK, W), lambda b, c: (b * nc + c, 0)),
        scratch_shapes=[pltpu.VMEM((HGRN_HEADS, HGRN_HEAD_DIM, HGRN_HEAD_DIM), F32)],
        compiler_params=_params("parallel", "arbitrary"),
        name="hgrn2",
    )(h, h, h, h, jnp.log(lb)[None, :], jnp.log1p(-lb)[None, :], (1.0 - lb)[None, :], norm_g[None, :])


def _rms(x, g):
    ms = jnp.mean(x * x, axis=-1, keepdims=True)
    return x * lax.rsqrt(ms + NORM_EPS) * g


def _rope(x, c, s):
    half = MLA_ROPE_DIM // 2
    lane = lax.broadcasted_iota(jnp.int32, x.shape, 1)
    swapped = jnp.where(lane < half, pltpu.roll(x, LANES - half, axis=1), pltpu.roll(x, half, axis=1))
    return x * c + swapped * s


def _mla_up_kernel(cq_ref, ckv_ref, kr_ref, c_ref, s_ref, qg_ref, kvg_ref, wq_ref, wkv_ref,
                   q_ref, kn_ref, v_ref, kro_ref):
    c, s = c_ref[...], s_ref[...]
    scale = MLA_QK_DIM ** -0.5 * LOG2_E
    q = jnp.dot(_rms(cq_ref[...], qg_ref[...]).astype(BF16), wq_ref[0], preferred_element_type=F32)
    kv = jnp.dot(_rms(ckv_ref[...], kvg_ref[...]).astype(BF16), wkv_ref[0], preferred_element_type=F32)
    for h in range(MLA_HEADS):
        base = h * Q_HEAD_PAD
        q_ref[h, :, :LANES] = (q[:, base:base + LANES] * scale).astype(BF16)
        q_ref[h, :, LANES:] = (_rope(q[:, base + LANES:base + 2 * LANES], c, s) * scale).astype(BF16)
        kbase = h * (MLA_NOPE_DIM + MLA_V_DIM)
        kn_ref[h] = kv[:, kbase:kbase + MLA_NOPE_DIM].astype(BF16)
        v_ref[h] = kv[:, kbase + MLA_NOPE_DIM:kbase + MLA_NOPE_DIM + MLA_V_DIM].astype(BF16)
    kro_ref[...] = _rope(kr_ref[...], c, s).astype(BF16)


def _mla_up(h, tab_c, tab_s, qn_g, w_uq, kvn_g, w_ukv, l):
    n = h.shape[0]
    tm = min(n, TM_MLA)
    H = MLA_HEADS
    cq_blk = (2 * SGU_WIDTH + 4 * HGRN_WIDTH) // Q_LORA_RANK
    kr_blk = (2 * SGU_WIDTH + 4 * HGRN_WIDTH + Q_LORA_RANK + KV_LORA_RANK) // LANES
    full = lambda shape: pl.BlockSpec(shape, lambda i: (0,) * len(shape))
    heads = lambda w: pl.BlockSpec((H, tm, w), lambda i: (0, i, 0))
    return pl.pallas_call(
        _mla_up_kernel,
        out_shape=(jax.ShapeDtypeStruct((H, n, Q_HEAD_PAD), BF16),
                   jax.ShapeDtypeStruct((H, n, MLA_NOPE_DIM), BF16),
                   jax.ShapeDtypeStruct((H, n, MLA_V_DIM), BF16),
                   jax.ShapeDtypeStruct((n, LANES), BF16)),
        grid=(n // tm,),
        in_specs=[pl.BlockSpec((tm, Q_LORA_RANK), lambda i: (i, cq_blk)),
                  pl.BlockSpec((tm, KV_LORA_RANK), lambda i: (i, cq_blk + 1)),
                  pl.BlockSpec((tm, LANES), lambda i: (i, kr_blk)),
                  pl.BlockSpec((tm, LANES), lambda i: (i, 0)),
                  pl.BlockSpec((tm, LANES), lambda i: (i, 0)),
                  full((1, Q_LORA_RANK)), full((1, KV_LORA_RANK)),
                  pl.BlockSpec((1, Q_LORA_RANK, H * Q_HEAD_PAD), lambda i: (l, 0, 0)),
                  pl.BlockSpec((1, KV_LORA_RANK, H * (MLA_NOPE_DIM + MLA_V_DIM)), lambda i: (l, 0, 0))],
        out_specs=(heads(Q_HEAD_PAD), heads(MLA_NOPE_DIM), heads(MLA_V_DIM),
                   pl.BlockSpec((tm, LANES), lambda i: (i, 0))),
        compiler_params=_params("parallel"),
        name="mla_up",
    )(h, h, h, tab_c, tab_s, qn_g[None, :], kvn_g[None, :], w_uq, w_ukv)


def _attn_kernel(q_ref, kn_ref, kr_ref, v_ref, o_ref, m_ref, l_ref, acc_ref):
    i = pl.program_id(2)
    nch, tk, _ = acc_ref.shape
    m_ref[...] = jnp.full_like(m_ref, -jnp.inf)
    l_ref[...] = jnp.zeros_like(l_ref)
    acc_ref[...] = jnp.zeros_like(acc_ref)

    def chain(c, j, masked):
        start = pl.multiple_of(j * tk, tk)
        q = q_ref[0, c * tk:(c + 1) * tk, :]
        k = jnp.concatenate([kn_ref[0, pl.ds(start, tk), :], kr_ref[pl.ds(start, tk), :]], axis=-1)
        s = _dot_nt(q, k)
        if masked:
            row = lax.broadcasted_iota(jnp.int32, s.shape, 0)
            col = lax.broadcasted_iota(jnp.int32, s.shape, 1)
            s = jnp.where(col <= row, s, -jnp.inf)
        chunks = [s[:, t * LANES:(t + 1) * LANES] for t in range(tk // LANES)]
        smax = functools.reduce(jnp.maximum, chunks)
        m_prev = m_ref[c]
        m_new = jnp.maximum(m_prev, jnp.max(smax, axis=-1, keepdims=True))
        alpha = jnp.exp2(m_prev - m_new)
        ps = [jnp.exp2(ch - m_new) for ch in chunks]
        l_ref[c] = alpha * l_ref[c] + functools.reduce(jnp.add, ps)
        p = jnp.concatenate([x.astype(BF16) for x in ps], axis=-1)
        acc_ref[c] = alpha * acc_ref[c] + jnp.dot(p, v_ref[0, pl.ds(start, tk), :], preferred_element_type=F32)
        m_ref[c] = m_new

    def body(j, carry):
        for c in range(nch):
            chain(c, j, masked=False)
        return carry

    lax.fori_loop(0, nch * i, body, 0)
    for t in range(nch):
        for c in range(t, nch):
            chain(c, nch * i + t, masked=(c == t))
    for c in range(nch):
        l = jnp.sum(l_ref[c], axis=-1, keepdims=True)
        o_ref[c * tk:(c + 1) * tk, :] = (acc_ref[c] / l).astype(o_ref.dtype)


def _attention(q, kn, kr, v, batch):
    H, n, _ = q.shape
    S = n // batch
    tq = min(S, TQ_ATTN)
    tk = min(tq, TK_ATTN)
    nq = S // tq
    return pl.pallas_call(
        _attn_kernel,
        out_shape=jax.ShapeDtypeStruct((n, H * MLA_V_DIM), BF16),
        grid=(batch, H, nq),
        in_specs=[pl.BlockSpec((1, tq, Q_HEAD_PAD), lambda b, h, i: (h, b * nq + i, 0)),
                  pl.BlockSpec((1, S, MLA_NOPE_DIM), lambda b, h, i: (h, b, 0)),
                  pl.BlockSpec((S, LANES), lambda b, h, i: (b, 0)),
                  pl.BlockSpec((1, S, MLA_V_DIM), lambda b, h, i: (h, b, 0))],
        out_specs=pl.BlockSpec((tq, MLA_V_DIM), lambda b, h, i: (b * nq + i, h)),
        scratch_shapes=[pltpu.VMEM((tq // tk, tk, LANES), F32), pltpu.VMEM((tq // tk, tk, LANES), F32),
                        pltpu.VMEM((tq // tk, tk, MLA_V_DIM), F32)],
        compiler_params=_params("parallel", "parallel", "arbitrary"),
        name="mla_attention",
    )(q, kn, kr, v)


def _layer_norm(x, g, b):
    mu = jnp.mean(x, axis=-1, keepdims=True)
    xc = x - mu
    var = jnp.mean(xc * xc, axis=-1, keepdims=True)
    return xc * lax.rsqrt(var + NORM_EPS) * g + b


def _first_index(hit, lane):
    return jnp.min(jnp.where(hit, lane, LANES), axis=-1, keepdims=True)


def _route(logits):
    lane = lax.broadcasted_iota(jnp.int32, logits.shape, 1)
    neg = -jnp.inf
    gl = jnp.where((lane >= N_EXPERTS) & (lane < N_EXPERTS + N_GROUPS), logits, neg)
    ge = jnp.exp(gl - jnp.max(gl, axis=-1, keepdims=True))
    gp = ge / jnp.sum(ge, axis=-1, keepdims=True)
    g_val = jnp.max(gp, axis=-1, keepdims=True)
    g_idx = _first_index(gp == g_val, lane) - N_EXPERTS
    el = jnp.where((lane // EXPERTS_PER_GROUP == g_idx) & (lane < N_EXPERTS), logits, neg)
    m1 = jnp.max(el, axis=-1, keepdims=True)
    i1 = _first_index(el == m1, lane)
    el2 = jnp.where(lane == i1, neg, el)
    m2 = jnp.max(el2, axis=-1, keepdims=True)
    i2 = _first_index(el2 == m2, lane)
    e2 = jnp.exp(m2 - m1)
    den = 1.0 + e2
    w1 = g_val * (1.0 / den)
    w2 = g_val * (e2 / den)
    return w1, w2, i1, i2


def _rows_to_slabs(ref, x, stride):
    m, d = x.shape
    for s in range(d // LANES):
        ref[pl.ds(s, m, stride=stride), :] = x[:, s * LANES:(s + 1) * LANES]


def _slabs_to_rows(ref, m, d, stride):
    return jnp.concatenate([ref[pl.ds(s, m, stride=stride), :] for s in range(d // LANES)], axis=-1)


def _outproj_kernel(ya_ref, yb_ref, yc_ref, x_ref, w_ref, g_ref, b_ref, wr_ref, br_ref,
                    x1_ref, x1s_ref, rt_ref, cnt_ref):
    @pl.when(pl.program_id(0) == 0)
    def _():
        cnt_ref[...] = jnp.zeros_like(cnt_ref)

    wa = SGU_WIDTH
    wb = SGU_WIDTH + HGRN_WIDTH
    d = x_ref.shape[1]
    slab = d // LANES
    ch = OUT_CHAIN_ROWS
    lane = lax.broadcasted_iota(jnp.int32, (ch, LANES), 1)
    row = lax.broadcasted_iota(jnp.int32, (ch, ch), 0)
    col = lax.broadcasted_iota(jnp.int32, (ch, ch), 1)
    before = (col < row).astype(BF16)
    carry = cnt_ref[...]
    for c in range(x_ref.shape[0] // ch):
        rs = slice(c * ch, (c + 1) * ch)
        mix = jnp.dot(ya_ref[rs, :], w_ref[0, :wa, :], preferred_element_type=F32)
        mix = mix + jnp.dot(yb_ref[rs, :], w_ref[0, wa:wb, :], preferred_element_type=F32)
        mix = mix + jnp.dot(yc_ref[rs, :], w_ref[0, wb:, :], preferred_element_type=F32)
        x1 = _layer_norm(DEEPNORM_ALPHA * x_ref[rs, :] + mix, g_ref[...], b_ref[...])
        x1_ref[rs, :] = x1
        _rows_to_slabs(x1s_ref.at[pl.ds(c * ch * slab, ch * slab), :], x1, slab)
        x_hi = x1.astype(BF16)
        x_lo = (x1 - x_hi.astype(F32)).astype(BF16)
        r = jnp.dot(x_hi, wr_ref[...], preferred_element_type=F32)
        logits = (r[:, :LANES] + r[:, LANES:] + jnp.dot(x_lo, wr_ref[:, :LANES], preferred_element_type=F32)
                  + br_ref[...])
        w1, w2, i1, i2 = _route(logits)

        hit1, hit2 = lane == i1, lane == i2
        oh1, oh2 = hit1.astype(F32), hit2.astype(F32)
        pre = jnp.dot(before, jnp.concatenate([oh1, oh2], axis=-1).astype(BF16), preferred_element_type=F32)
        tot1 = jnp.sum(oh1, axis=0, keepdims=True)
        tot2 = jnp.sum(oh2, axis=0, keepdims=True)
        rank1 = jnp.sum(jnp.where(hit1, pre[:, :LANES] + carry, 0.0), axis=-1, keepdims=True)
        rank2 = jnp.sum(jnp.where(hit2, pre[:, LANES:] + (carry + tot1), 0.0), axis=-1, keepdims=True)
        carry = carry + tot1 + tot2

        out = jnp.where(lane == 0, w1, 0.0)
        out = jnp.where(lane == 1, w2, out)
        out = jnp.where(lane == 2, i1.astype(F32), out)
        out = jnp.where(lane == 3, i2.astype(F32), out)
        out = jnp.where(lane == 4, rank1, out)
        out = jnp.where(lane == 5, rank2, out)
        rt_ref[rs, :] = out
    cnt_ref[...] = carry


def _outproj(ya, yb, yc, x, w_out, ln_g, ln_b, wr, br, l):
    n, d = x.shape
    tm = min(n, TM_OUT)
    assert tm % OUT_CHAIN_ROWS == 0
    slab = d // LANES
    full = lambda shape: pl.BlockSpec(shape, lambda i: (0,) * len(shape))
    rows = lambda w: pl.BlockSpec((tm, w), lambda i: (i, 0))
    wr_hi = wr.astype(BF16)
    wr_lo = (wr - wr_hi.astype(F32)).astype(BF16)
    return pl.pallas_call(
        _outproj_kernel,
        out_shape=(jax.ShapeDtypeStruct((n, d), F32), jax.ShapeDtypeStruct((n * slab, LANES), F32),
                   jax.ShapeDtypeStruct((n, LANES), F32), jax.ShapeDtypeStruct((1, LANES), F32)),
        grid=(n // tm,),
        in_specs=[rows(ya.shape[1]), rows(yb.shape[1]), rows(yc.shape[1]), rows(d),
                  pl.BlockSpec((1, d, d), lambda i: (l, 0, 0), pipeline_mode=pl.Buffered(1)), full((1, d)), full((1, d)),
                  full((d, 2 * LANES)), full((1, LANES))],
        out_specs=(rows(d), pl.BlockSpec((tm * slab, LANES), lambda i: (i, 0)), rows(LANES), full((1, LANES))),
        compiler_params=_params("arbitrary"),
        name="out_proj_ln_router",
    )(ya, yb, yc, x, w_out, ln_g[None, :], ln_b[None, :], jnp.concatenate([wr_hi, wr_lo], axis=1), br)


def _moe_kernel(src_ref, dst_ref, exp_ref, used_ref, x_hbm, wg_ref, wu_ref, wd_ref, y_hbm,
                xbuf0, xbuf1, xbuf2, xbuf3, ybuf0, ybuf1, ybuf2, ybuf3, wgb, wub, wdb, gsem, ssem):
    xbufs, ybufs = (xbuf0, xbuf1, xbuf2, xbuf3), (ybuf0, ybuf1, ybuf2, ybuf3)
    nx = len(xbufs)
    i = pl.program_id(0)
    nb = pl.num_programs(0) - 1
    d = wgb.shape[0]
    slab = d // LANES
    vs = MOE_VMEM_STRIDE
    rows = xbuf0.shape[0] // vs
    used = used_ref[0]
    n_real = y_hbm.shape[0] // slab - (nx + 1) * rows

    def gather_copy(tok, r, slot):
        src = x_hbm.at[pl.ds(pl.multiple_of(tok * slab, slab), slab), :]
        return pltpu.make_async_copy(src, xbufs[slot].at[pl.ds(r * vs, slab), :], gsem.at[slot])

    def scatter_copy(dst, r, slot):
        out = y_hbm.at[pl.ds(pl.multiple_of(dst * slab, slab), slab), :]
        return pltpu.make_async_copy(ybufs[slot].at[pl.ds(r * vs, slab), :], out, ssem.at[slot])

    def fill_copy(ybuf, region, sem):
        return pltpu.make_async_copy(ybuf.at[pl.ds(0, rows * slab), :],
                                     y_hbm.at[pl.ds((n_real + region * rows) * slab, rows * slab), :], sem)

    def wait_gather(slot):
        pltpu.make_async_copy(x_hbm.at[pl.ds(0, rows * slab), :], xbuf0.at[pl.ds(0, rows * slab), :],
                              gsem.at[slot]).wait()

    def wait_scatter(slot):
        fill_copy(ybuf0, 0, ssem.at[slot]).wait()

    @pl.when(i == 0)
    def _():
        ybuf3[...] = jnp.zeros_like(ybuf3)
        for t in range(nx - 1):
            fill_copy(ybuf3, 1 + t, ssem.at[t]).start()
        last = fill_copy(ybuf3, nx, ssem.at[nx])
        last.start()
        last.wait()
        for blk in range(2):
            def issue(r, carry):
                gather_copy(src_ref[jnp.minimum(blk, nb - 1) * rows + r], r, blk).start()
                return carry
            lax.fori_loop(0, rows, issue, 0, unroll=8)

    cur = jnp.minimum(i, nb - 1)
    @pl.when((i < used) & ((i == 0) | (exp_ref[cur] != exp_ref[jnp.maximum(cur - 1, 0)])))
    def _():
        wgb[...] = wg_ref[0, 0].astype(BF16)
        wub[...] = wu_ref[0, 0].astype(BF16)
        wdb[...] = wd_ref[0, 0].astype(BF16)

    def compute_block(phase):
        xs = ys = phase
        wait_gather(xs)
        wait_scatter(ys)
        nxt = jnp.minimum(i + 2, nb - 1) * rows
        prv = i * rows
        for r in range(rows):
            gather_copy(src_ref[nxt + r], r, (xs + 2) % nx).start()
            scatter_copy(dst_ref[prv + r], r, (ys - 1) % nx).start()
        xb = _slabs_to_rows(xbufs[xs], rows, d, vs).astype(BF16)
        hg = jnp.dot(xb, wgb[...], preferred_element_type=F32)
        hu = jnp.dot(xb, wub[...], preferred_element_type=F32)
        hh = (jax.nn.silu(hg) * hu).astype(BF16)
        _rows_to_slabs(ybufs[ys], jnp.dot(hh, wdb[...], preferred_element_type=F32), vs)

    def drain_block(ys):
        def issue(r, carry):
            scatter_copy(dst_ref[i * rows + r], r, ys).start()
            return carry
        lax.fori_loop(0, rows, issue, 0, unroll=8)
        wait_scatter(ys)

    for phase in range(nx):
        pl.when((i < used) & (i % nx == phase))(functools.partial(compute_block, phase))

    @pl.when(i == used)
    def _():
        wait_gather(i % nx)
        wait_gather((i + 1) % nx)
        for t in range(nx - 1):
            wait_scatter((i + t) % nx)

    for phase in range(nx):
        pl.when((i == used) & (i >= 1) & ((i - 1) % nx == phase))(functools.partial(drain_block, phase))


def _moe(slot_src, slot_dst, blk_exp, used, x1s, w_gate, w_up, w_down, n_out, l):
    d, de = w_gate.shape[2:]
    slab = d // LANES
    nb = blk_exp.shape[0]
    rows = slot_src.shape[0] // nb
    blk = lambda i, src, dst, exp, used: (l, exp[jnp.minimum(i, nb - 1)], 0, 0)
    return pl.pallas_call(
        _moe_kernel,
        out_shape=jax.ShapeDtypeStruct((n_out * slab, LANES), F32),
        grid_spec=pltpu.PrefetchScalarGridSpec(
            num_scalar_prefetch=4,
            grid=(nb + 1,),
            in_specs=[pl.BlockSpec(memory_space=pl.ANY),
                      pl.BlockSpec((1, 1, d, de), blk), pl.BlockSpec((1, 1, d, de), blk),
                      pl.BlockSpec((1, 1, de, d), blk)],
            out_specs=pl.BlockSpec(memory_space=pl.ANY),
            scratch_shapes=[pltpu.VMEM((rows * MOE_VMEM_STRIDE, LANES), F32)] * (2 * MOE_RING) + [
                            pltpu.VMEM((d, de), BF16), pltpu.VMEM((d, de), BF16), pltpu.VMEM((de, d), BF16),
                            pltpu.SemaphoreType.DMA((MOE_RING,)), pltpu.SemaphoreType.DMA((MOE_RING + 1,))]),
        compiler_params=_params("arbitrary"),
        name="moe_experts",
    )(slot_src, slot_dst, blk_exp, used, x1s, w_gate, w_up, w_down)


def _combine_kernel(y0_ref, y1_ref, x_ref, rt_ref, g_ref, b_ref, o_ref):
    tm, d = x_ref.shape
    slab = d // LANES
    rt = rt_ref[...]
    ffn = _slabs_to_rows(y0_ref, tm, d, slab) * rt[:, 0:1] + _slabs_to_rows(y1_ref, tm, d, slab) * rt[:, 1:2]
    o_ref[...] = _layer_norm(DEEPNORM_ALPHA * x_ref[...] + ffn, g_ref[...], b_ref[...])


def _combine(y_tok, x1, route, ln_g, ln_b):
    n, d = x1.shape
    tm = min(n, TM_COMB)
    slab = d // LANES
    return pl.pallas_call(
        _combine_kernel,
        out_shape=jax.ShapeDtypeStruct((n, d), F32),
        grid=(n // tm,),
        in_specs=[pl.BlockSpec((tm * slab, LANES), lambda i: (i, 0)),
                  pl.BlockSpec((tm * slab, LANES), lambda i: (i + n // tm, 0)),
                  pl.BlockSpec((tm, d), lambda i: (i, 0)),
                  pl.BlockSpec((tm, LANES), lambda i: (i, 0)),
                  pl.BlockSpec((1, d), lambda i: (0, 0)),
                  pl.BlockSpec((1, d), lambda i: (0, 0))],
        out_specs=pl.BlockSpec((tm, d), lambda i: (i, 0)),
        compiler_params=_params("parallel"),
        name="moe_combine_ln",
    )(y_tok, y_tok, x1, route, ln_g[None, :], ln_b[None, :])


def _dispatch_plan(route, counts, rows):
    n = route.shape[0]
    nk = n * TOP_K
    nb = (nk + rows - 1) // rows + N_EXPERTS
    counts = counts[0, :N_EXPERTS].astype(jnp.int32)
    padded = ((counts + rows - 1) // rows) * rows
    pad_end = jnp.cumsum(padded)
    pad_start = pad_end - padded
    flat_e = route[:, 2:2 + TOP_K].astype(jnp.int32).reshape(nk)
    rank = route[:, 2 + TOP_K:2 + 2 * TOP_K].astype(jnp.int32).reshape(nk)
    dest = pad_start[flat_e] + rank
    asg = jnp.full((nb * rows,), -1, jnp.int32).at[dest].set(jnp.arange(nk, dtype=jnp.int32))
    slot = jnp.arange(nb * rows, dtype=jnp.int32)
    dump = nk + rows + ((slot // rows) % MOE_RING) * rows + slot % rows
    slot_src = jnp.where(asg >= 0, asg // TOP_K, 0)
    out_row = (asg % TOP_K) * n + asg // TOP_K
    slot_dst = jnp.concatenate([nk + jnp.arange(rows, dtype=jnp.int32), jnp.where(asg >= 0, out_row, dump)])
    blk_exp = jnp.minimum(
        jnp.sum((pad_end[None, :] <= (jnp.arange(nb, dtype=jnp.int32) * rows)[:, None]).astype(jnp.int32), axis=1),
        N_EXPERTS - 1).astype(jnp.int32)
    used = (pad_end[-1] // rows).astype(jnp.int32).reshape(1)
    return slot_src, slot_dst, blk_exp, used, nk + (MOE_RING + 1) * rows


def _pad_q_heads(w_uq):
    lead = w_uq.shape[:-1]
    w = w_uq.reshape(*lead, MLA_HEADS, MLA_QK_DIM)
    w = jnp.pad(w, [(0, 0)] * len(lead) + [(0, 0), (0, Q_HEAD_PAD - MLA_QK_DIM)])
    return w.reshape(*lead, MLA_HEADS * Q_HEAD_PAD)


def kernel(x, positions, w_in, sgu_ln_g, sgu_ln_b, sgu_ws, sgu_b, hgrn_lb_logits, hgrn_norm_g, mla_qn_g, mla_w_uq, mla_kvn_g, mla_w_ukv, w_out, ln1_g, ln1_b, router_group_w, router_group_b, router_expert_w, router_expert_b, expert_w_gate, expert_w_up, expert_w_down, ln2_g, ln2_b):
    B, S, D = x.shape
    n = B * S
    depth = w_in.shape[0]
    tab_c, tab_s = _rope_tables(positions)
    lb_cum = jnp.cumsum(jax.nn.softmax(hgrn_lb_logits.astype(F32), axis=0), axis=0)
    lower_bounds = lb_cum - lb_cum[0:1]
    xt = x.reshape(n, D)
    w_in_b = jnp.pad(w_in, ((0, 0), (0, 0), (0, D_IN_PAD - D_IN))).astype(BF16)
    w_uq_b = _pad_q_heads(mla_w_uq).astype(BF16)
    w_ukv_b = mla_w_ukv.astype(BF16)
    w_out_b = w_out.astype(BF16)
    for l in range(depth):
        h = _inproj(xt, w_in_b, l)
        y_a = _sgu(h, sgu_ln_g[l], sgu_ln_b[l], sgu_ws[l], sgu_b[l])
        y_b = _hgrn(h, lower_bounds[l], hgrn_norm_g[l], B)
        q, kn, v, kr = _mla_up(h, tab_c, tab_s, mla_qn_g[l], w_uq_b, mla_kvn_g[l], w_ukv_b, l)
        y_c = _attention(q, kn, kr, v, B)
        wr = jnp.pad(jnp.concatenate([router_expert_w[l], router_group_w[l]], axis=1),
                     ((0, 0), (0, LANES - N_EXPERTS - N_GROUPS)))
        br = jnp.pad(jnp.concatenate([router_expert_b[l], router_group_b[l]]), (0, LANES - N_EXPERTS - N_GROUPS))
        x1, x1s, route, counts = _outproj(y_a, y_b, y_c, xt, w_out_b, ln1_g[l], ln1_b[l], wr, br[None, :], l)
        slot_src, slot_dst, blk_exp, used, n_out = _dispatch_plan(route, counts, MOE_ROWS)
        y_tok = _moe(slot_src, slot_dst, blk_exp, used, x1s, expert_w_gate, expert_w_up, expert_w_down, n_out, l)
        xt = _combine(y_tok, x1, route, ln2_g[l], ln2_b[l])
    return xt.reshape(B, S, D)
```

```python
import functools

import jax
import jax.numpy as jnp
from jax import lax
from jax.experimental import pallas as pl
from jax.experimental.pallas import tpu as pltpu

D_MODEL = 2048
DEPTH = 4
CHUNK = 128
SGU_WIDTH = D_MODEL // 4
SGU_GROUPS = 4
SGU_GROUP_DIM = SGU_WIDTH // SGU_GROUPS
HGRN_WIDTH = D_MODEL // 4
HGRN_HEADS = 4
HGRN_HEAD_DIM = HGRN_WIDTH // HGRN_HEADS
MLA_HEADS = 8
MLA_V_DIM = (D_MODEL - SGU_WIDTH - HGRN_WIDTH) // MLA_HEADS
MLA_NOPE_DIM = 128
MLA_ROPE_DIM = 64
MLA_QK_DIM = MLA_NOPE_DIM + MLA_ROPE_DIM
Q_LORA_RANK = D_MODEL // 4
KV_LORA_RANK = D_MODEL // 4
ROPE_THETA = 10000.0
N_GROUPS = 4
EXPERTS_PER_GROUP = 8
N_EXPERTS = N_GROUPS * EXPERTS_PER_GROUP
TOP_K = 2
D_EXPERT = D_MODEL // 4
NORM_EPS = 1e-5
DEEPNORM_ALPHA = (2 * DEPTH) ** 0.25
D_IN = SGU_WIDTH * 2 + HGRN_WIDTH * 4 + Q_LORA_RANK + KV_LORA_RANK + MLA_ROPE_DIM

LANES = 128
D_IN_PAD = ((D_IN + LANES - 1) // LANES) * LANES
Q_HEAD_PAD = 2 * LANES
VMEM_LIMIT = 56 * 1024 * 1024

TM_INPROJ = 512
TN_INPROJ = D_IN_PAD // 3
TM_SGU = 512
TM_MLA = 512
TQ_ATTN = 2048
TK_ATTN = 512
TM_OUT = 512
OUT_CHAIN_ROWS = 256
MOE_ROWS = 256
MOE_RING = 4
MOE_VMEM_STRIDE = 20
TM_COMB = 256

BF16 = jnp.bfloat16
F32 = jnp.float32
LOG2_E = 1.4426950408889634


def _params(*sem):
    return pltpu.CompilerParams(dimension_semantics=sem, vmem_limit_bytes=VMEM_LIMIT)


def _rope_kernel(pos_ref, inv_ref, sgn_ref, c_ref, s_ref):
    ang = pos_ref[...].astype(F32) * inv_ref[...]
    live = sgn_ref[...] != 0.0
    c_ref[...] = jnp.where(live, jnp.cos(ang), 0.0)
    s_ref[...] = jnp.sin(ang) * sgn_ref[...]


def _rope_tables(positions):
    n = positions.size
    tm = min(n, 1024)
    half = MLA_ROPE_DIM // 2
    inv = 1.0 / (ROPE_THETA ** (jnp.arange(0, MLA_ROPE_DIM, 2, dtype=F32) / MLA_ROPE_DIM))
    zeros = jnp.zeros((LANES - 2 * half,), F32)
    inv_t = jnp.concatenate([inv, inv, zeros])[None, :]
    sgn_t = jnp.concatenate([-jnp.ones((half,), F32), jnp.ones((half,), F32), zeros])[None, :]
    row = pl.BlockSpec((1, LANES), lambda i: (0, 0))
    return pl.pallas_call(
        _rope_kernel,
        out_shape=(jax.ShapeDtypeStruct((n, LANES), F32),) * 2,
        grid=(n // tm,),
        in_specs=[pl.BlockSpec((tm, 1), lambda i: (i, 0)), row, row],
        out_specs=(pl.BlockSpec((tm, LANES), lambda i: (i, 0)),) * 2,
        compiler_params=_params("parallel"),
        name="rope_tables",
    )(positions.reshape(n, 1), inv_t, sgn_t)


def _inproj_kernel(x_ref, w_ref, o_ref):
    o_ref[...] = jnp.dot(x_ref[...].astype(BF16), w_ref[0], preferred_element_type=F32)


def _inproj(x, w, l):
    n, d = x.shape
    tm = min(n, TM_INPROJ)
    return pl.pallas_call(
        _inproj_kernel,
        out_shape=jax.ShapeDtypeStruct((n, D_IN_PAD), F32),
        grid=(D_IN_PAD // TN_INPROJ, n // tm),
        in_specs=[pl.BlockSpec((tm, d), lambda j, i: (i, 0)),
                  pl.BlockSpec((1, d, TN_INPROJ), lambda j, i: (l, 0, j))],
        out_specs=pl.BlockSpec((tm, TN_INPROJ), lambda j, i: (i, j)),
        compiler_params=_params("parallel", "parallel"),
        name="in_proj",
    )(x, w)


def _sgu_kernel(u_ref, v_ref, g_ref, b_ref, ws_ref, bs_ref, o_ref):
    u = jax.nn.gelu(u_ref[...])
    v = jax.nn.gelu(v_ref[...])
    mu = jnp.mean(v, axis=-1, keepdims=True)
    vc = v - mu
    var = jnp.mean(vc * vc, axis=-1, keepdims=True)
    vn = (vc * lax.rsqrt(var + NORM_EPS) * g_ref[...] + b_ref[...]).astype(BF16)
    row = lax.broadcasted_iota(jnp.int32, (CHUNK, CHUNK), 0)
    col = lax.broadcasted_iota(jnp.int32, (CHUNK, CHUNK), 1)
    causal = row >= col
    bs = bs_ref[...]
    for g in range(SGU_GROUPS):
        w = jnp.where(causal, ws_ref[g], 0.0).astype(BF16)
        cs = slice(g * SGU_GROUP_DIM, (g + 1) * SGU_GROUP_DIM)
        for c in range(u.shape[0] // CHUNK):
            rs = slice(c * CHUNK, (c + 1) * CHUNK)
            mixed = jnp.dot(w, vn[rs, cs], preferred_element_type=F32) + bs[:, g:g + 1]
            o_ref[rs, cs] = (u[rs, cs] * mixed).astype(o_ref.dtype)


def _sgu(h, ln_g, ln_b, ws, bs):
    n = h.shape[0]
    tm = min(n, TM_SGU)
    full = lambda shape: pl.BlockSpec(shape, lambda i: (0,) * len(shape))
    return pl.pallas_call(
        _sgu_kernel,
        out_shape=jax.ShapeDtypeStruct((n, SGU_WIDTH), BF16),
        grid=(n // tm,),
        in_specs=[pl.BlockSpec((tm, SGU_WIDTH), lambda i: (i, 0)),
                  pl.BlockSpec((tm, SGU_WIDTH), lambda i: (i, 1)),
                  full((1, SGU_WIDTH)), full((1, SGU_WIDTH)),
                  full((SGU_GROUPS, CHUNK, CHUNK)), full((CHUNK, SGU_GROUPS))],
        out_specs=pl.BlockSpec((tm, SGU_WIDTH), lambda i: (i, 0)),
        compiler_params=_params("parallel"),
        name="sgu",
    )(h, h, ln_g[None, :], ln_b[None, :], ws, bs.T)


def _dot_nt(a, b):
    return lax.dot_general(a, b, (((1,), (1,)), ((), ())), preferred_element_type=F32)


def _hgrn_kernel(q_ref, f_ref, i_ref, g_ref, la_ref, lc_ref, om_ref, ng_ref, o_ref, state_ref):
    @pl.when(pl.program_id(1) == 0)
    def _():
        state_ref[...] = jnp.zeros_like(state_ref)

    T, K = CHUNK, HGRN_HEAD_DIM
    fx = f_ref[...]
    log_f = jnp.logaddexp(la_ref[...], lc_ref[...] + jax.nn.log_sigmoid(fx))
    key = om_ref[...] * jax.nn.sigmoid(-fx)
    qry = jax.nn.silu(q_ref[...])
    val = i_ref[...]
    gate = jax.nn.silu(g_ref[...])

    row = lax.broadcasted_iota(jnp.int32, (T, T), 0)
    col = lax.broadcasted_iota(jnp.int32, (T, T), 1)
    tril = (row >= col).astype(F32)
    G_all = jnp.dot(tril, log_f, preferred_element_type=F32, precision=lax.Precision.HIGHEST)

    SUB = 8
    for h in range(HGRN_HEADS):
        cs = slice(h * K, (h + 1) * K)
        G, q, k, v = G_all[:, cs], qry[:, cs], key[:, cs], val[:, cs]
        A = jnp.zeros((T, T), F32)
        m = T // 2
        while m >= SUB:
            blk = 2 * m
            g_mid = jnp.concatenate(
                [jnp.broadcast_to(G[b * blk + m - 1:b * blk + m, :], (blk, K)) for b in range(T // blk)], axis=0)
            lower = (row % blk) >= m
            qs = jnp.where(lower, q * jnp.exp(G - g_mid), 0.0).astype(BF16)
            ks = jnp.where(lower, 0.0, k * jnp.exp(g_mid - G)).astype(BF16)
            same = (row // blk) == (col // blk)
            A = A + jnp.where(same, _dot_nt(qs, ks), 0.0)
            m //= 2
        rmod = row % SUB
        k3, G3 = k.reshape(T // SUB, SUB, K), G.reshape(T // SUB, SUB, K)
        for d in range(SUB):
            if d == 0:
                prod = q * k
            else:
                kd = pltpu.roll(k3, d, axis=1).reshape(T, K)
                gd = pltpu.roll(G3, d, axis=1).reshape(T, K)
                prod = jnp.where(rmod >= d, q * kd * jnp.exp(G - gd), 0.0)
            A = A + jnp.where(col == row - d, jnp.sum(prod, axis=-1, keepdims=True), 0.0)

        st = state_ref[h]
        o = jnp.dot(A.astype(BF16), v.astype(BF16), preferred_element_type=F32)
        o = o + _dot_nt((q * jnp.exp(G)).astype(BF16), st.astype(BF16))
        g_last = G[T - 1:T, :]
        k_dec = (k * jnp.exp(g_last - G)).astype(BF16)
        state_ref[h] = jnp.exp(g_last) * st + jnp.dot(v.T.astype(BF16), k_dec, preferred_element_type=F32)

        ms = jnp.mean(o * o, axis=-1, keepdims=True)
        o = o * lax.rsqrt(ms + NORM_EPS) * ng_ref[:, cs]
        o_ref[:, cs] = (o * gate[:, cs]).astype(o_ref.dtype)


def _hgrn(h, lb, norm_g, batch):
    n = h.shape[0]
    nc = n // batch // CHUNK
    W = HGRN_WIDTH
    col = lambda j: pl.BlockSpec((CHUNK, W), lambda b, c: (b * nc + c, j))
    vec = pl.BlockSpec((1, W), lambda b, c: (0, 0))
    base = 2 * SGU_WIDTH // W
    return pl.pallas_call(
        _hgrn_kernel,
        out_shape=jax.ShapeDtypeStruct((n, W), BF16),
        grid=(batch, nc),
        in_specs=[col(base), col(base + 1), col(base + 2), col(base + 3), vec, vec, vec, vec],
        out_specs=pl.BlockSpec((CHUNK, W), lambda b, c: (b * nc + c, 0)),
        scratch_shapes=[pltpu.VMEM((HGRN_HEADS, HGRN_HEAD_DIM, HGRN_HEAD_DIM), F32)],
        compiler_params=_params("parallel", "arbitrary"),
        name="hgrn2",
    )(h, h, h, h, jnp.log(lb)[None, :], jnp.log1p(-lb)[None, :], (1.0 - lb)[None, :], norm_g[None, :])


def _rms(x, g):
    ms = jnp.mean(x * x, axis=-1, keepdims=True)
    return x * lax.rsqrt(ms + NORM_EPS) * g


def _rope(x, c, s):
    half = MLA_ROPE_DIM // 2
    lane = lax.broadcasted_iota(jnp.int32, x.shape, 1)
    swapped = jnp.where(lane < half, pltpu.roll(x, LANES - half, axis=1), pltpu.roll(x, half, axis=1))
    return x * c + swapped * s


def _mla_up_kernel(cq_ref, ckv_ref, kr_ref, c_ref, s_ref, qg_ref, kvg_ref, wq_ref, wkv_ref,
                   q_ref, kn_ref, v_ref, kro_ref):
    c, s = c_ref[...], s_ref[...]
    scale = MLA_QK_DIM ** -0.5 * LOG2_E
    q = jnp.dot(_rms(cq_ref[...], qg_ref[...]).astype(BF16), wq_ref[0], preferred_element_type=F32)
    kv = jnp.dot(_rms(ckv_ref[...], kvg_ref[...]).astype(BF16), wkv_ref[0], preferred_element_type=F32)
    for h in range(MLA_HEADS):
        base = h * Q_HEAD_PAD
        q_ref[h, :, :LANES] = (q[:, base:base + LANES] * scale).astype(BF16)
        q_ref[h, :, LANES:] = (_rope(q[:, base + LANES:base + 2 * LANES], c, s) * scale).astype(BF16)
        kbase = h * (MLA_NOPE_DIM + MLA_V_DIM)
        kn_ref[h] = kv[:, kbase:kbase + MLA_NOPE_DIM].astype(BF16)
        v_ref[h] = kv[:, kbase + MLA_NOPE_DIM:kbase + MLA_NOPE_DIM + MLA_V_DIM].astype(BF16)
    kro_ref[...] = _rope(kr_ref[...], c, s).astype(BF16)


def _mla_up(h, tab_c, tab_s, qn_g, w_uq, kvn_g, w_ukv, l):
    n = h.shape[0]
    tm = min(n, TM_MLA)
    H = MLA_HEADS
    cq_blk = (2 * SGU_WIDTH + 4 * HGRN_WIDTH) // Q_LORA_RANK
    kr_blk = (2 * SGU_WIDTH + 4 * HGRN_WIDTH + Q_LORA_RANK + KV_LORA_RANK) // LANES
    full = lambda shape: pl.BlockSpec(shape, lambda i: (0,) * len(shape))
    heads = lambda w: pl.BlockSpec((H, tm, w), lambda i: (0, i, 0))
    return pl.pallas_call(
        _mla_up_kernel,
        out_shape=(jax.ShapeDtypeStruct((H, n, Q_HEAD_PAD), BF16),
                   jax.ShapeDtypeStruct((H, n, MLA_NOPE_DIM), BF16),
                   jax.ShapeDtypeStruct((H, n, MLA_V_DIM), BF16),
                   jax.ShapeDtypeStruct((n, LANES), BF16)),
        grid=(n // tm,),
        in_specs=[pl.BlockSpec((tm, Q_LORA_RANK), lambda i: (i, cq_blk)),
                  pl.BlockSpec((tm, KV_LORA_RANK), lambda i: (i, cq_blk + 1)),
                  pl.BlockSpec((tm, LANES), lambda i: (i, kr_blk)),
                  pl.BlockSpec((tm, LANES), lambda i: (i, 0)),
                  pl.BlockSpec((tm, LANES), lambda i: (i, 0)),
                  full((1, Q_LORA_RANK)), full((1, KV_LORA_RANK)),
                  pl.BlockSpec((1, Q_LORA_RANK, H * Q_HEAD_PAD), lambda i: (l, 0, 0)),
                  pl.BlockSpec((1, KV_LORA_RANK, H * (MLA_NOPE_DIM + MLA_V_DIM)), lambda i: (l, 0, 0))],
        out_specs=(heads(Q_HEAD_PAD), heads(MLA_NOPE_DIM), heads(MLA_V_DIM),
                   pl.BlockSpec((tm, LANES), lambda i: (i, 0))),
        compiler_params=_params("parallel"),
        name="mla_up",
    )(h, h, h, tab_c, tab_s, qn_g[None, :], kvn_g[None, :], w_uq, w_ukv)


def _attn_kernel(q_ref, kn_ref, kr_ref, v_ref, o_ref, m_ref, l_ref, acc_ref):
    i = pl.program_id(2)
    nch, tk, _ = acc_ref.shape
    m_ref[...] = jnp.full_like(m_ref, -jnp.inf)
    l_ref[...] = jnp.zeros_like(l_ref)
    acc_ref[...] = jnp.zeros_like(acc_ref)

    def chain(c, j, masked):
        start = pl.multiple_of(j * tk, tk)
        q = q_ref[0, c * tk:(c + 1) * tk, :]
        k = jnp.concatenate([kn_ref[0, pl.ds(start, tk), :], kr_ref[pl.ds(start, tk), :]], axis=-1)
        s = _dot_nt(q, k)
        if masked:
            row = lax.broadcasted_iota(jnp.int32, s.shape, 0)
            col = lax.broadcasted_iota(jnp.int32, s.shape, 1)
            s = jnp.where(col <= row, s, -jnp.inf)
        chunks = [s[:, t * LANES:(t + 1) * LANES] for t in range(tk // LANES)]
        smax = functools.reduce(jnp.maximum, chunks)
        m_prev = m_ref[c]
        m_new = jnp.maximum(m_prev, jnp.max(smax, axis=-1, keepdims=True))
        alpha = jnp.exp2(m_prev - m_new)
        ps = [jnp.exp2(ch - m_new) for ch in chunks]
        l_ref[c] = alpha * l_ref[c] + functools.reduce(jnp.add, ps)
        p = jnp.concatenate([x.astype(BF16) for x in ps], axis=-1)
        acc_ref[c] = alpha * acc_ref[c] + jnp.dot(p, v_ref[0, pl.ds(start, tk), :], preferred_element_type=F32)
        m_ref[c] = m_new

    def body(j, carry):
        for c in range(nch):
            chain(c, j, masked=False)
        return carry

    lax.fori_loop(0, nch * i, body, 0)
    for t in range(nch):
        for c in range(t, nch):
            chain(c, nch * i + t, masked=(c == t))
    for c in range(nch):
        l = jnp.sum(l_ref[c], axis=-1, keepdims=True)
        o_ref[c * tk:(c + 1) * tk, :] = (acc_ref[c] / l).astype(o_ref.dtype)


def _attention(q, kn, kr, v, batch):
    H, n, _ = q.shape
    S = n // batch
    tq = min(S, TQ_ATTN)
    tk = min(tq, TK_ATTN)
    nq = S // tq
    return pl.pallas_call(
        _attn_kernel,
        out_shape=jax.ShapeDtypeStruct((n, H * MLA_V_DIM), BF16),
        grid=(batch, H, nq),
        in_specs=[pl.BlockSpec((1, tq, Q_HEAD_PAD), lambda b, h, i: (h, b * nq + i, 0)),
                  pl.BlockSpec((1, S, MLA_NOPE_DIM), lambda b, h, i: (h, b, 0)),
                  pl.BlockSpec((S, LANES), lambda b, h, i: (b, 0)),
                  pl.BlockSpec((1, S, MLA_V_DIM), lambda b, h, i: (h, b, 0))],
        out_specs=pl.BlockSpec((tq, MLA_V_DIM), lambda b, h, i: (b * nq + i, h)),
        scratch_shapes=[pltpu.VMEM((tq // tk, tk, LANES), F32), pltpu.VMEM((tq // tk, tk, LANES), F32),
                        pltpu.VMEM((tq // tk, tk, MLA_V_DIM), F32)],
        compiler_params=_params("parallel", "parallel", "arbitrary"),
        name="mla_attention",
    )(q, kn, kr, v)


def _layer_norm(x, g, b):
    mu = jnp.mean(x, axis=-1, keepdims=True)
    xc = x - mu
    var = jnp.mean(xc * xc, axis=-1, keepdims=True)
    return xc * lax.rsqrt(var + NORM_EPS) * g + b


def _first_index(hit, lane):
    return jnp.min(jnp.where(hit, lane, LANES), axis=-1, keepdims=True)


def _route(logits):
    lane = lax.broadcasted_iota(jnp.int32, logits.shape, 1)
    neg = -jnp.inf
    gl = jnp.where((lane >= N_EXPERTS) & (lane < N_EXPERTS + N_GROUPS), logits, neg)
    ge = jnp.exp(gl - jnp.max(gl, axis=-1, keepdims=True))
    gp = ge / jnp.sum(ge, axis=-1, keepdims=True)
    g_val = jnp.max(gp, axis=-1, keepdims=True)
    g_idx = _first_index(gp == g_val, lane) - N_EXPERTS
    el = jnp.where((lane // EXPERTS_PER_GROUP == g_idx) & (lane < N_EXPERTS), logits, neg)
    m1 = jnp.max(el, axis=-1, keepdims=True)
    i1 = _first_index(el == m1, lane)
    el2 = jnp.where(lane == i1, neg, el)
    m2 = jnp.max(el2, axis=-1, keepdims=True)
    i2 = _first_index(el2 == m2, lane)
    e2 = jnp.exp(m2 - m1)
    den = 1.0 + e2
    w1 = g_val * (1.0 / den)
    w2 = g_val * (e2 / den)
    return w1, w2, i1, i2


def _rows_to_slabs(ref, x, stride):
    m, d = x.shape
    for s in range(d // LANES):
        ref[pl.ds(s, m, stride=stride), :] = x[:, s * LANES:(s + 1) * LANES]


def _slabs_to_rows(ref, m, d, stride):
    return jnp.concatenate([ref[pl.ds(s, m, stride=stride), :] for s in range(d // LANES)], axis=-1)


def _outproj_kernel(ya_ref, yb_ref, yc_ref, x_ref, w_ref, g_ref, b_ref, wr_ref, br_ref,
                    x1_ref, x1s_ref, rt_ref, cnt_ref):
    @pl.when(pl.program_id(0) == 0)
    def _():
        cnt_ref[...] = jnp.zeros_like(cnt_ref)

    wa = SGU_WIDTH
    wb = SGU_WIDTH + HGRN_WIDTH
    d = x_ref.shape[1]
    slab = d // LANES
    ch = OUT_CHAIN_ROWS
    lane = lax.broadcasted_iota(jnp.int32, (ch, LANES), 1)
    row = lax.broadcasted_iota(jnp.int32, (ch, ch), 0)
    col = lax.broadcasted_iota(jnp.int32, (ch, ch), 1)
    before = (col < row).astype(BF16)
    carry = cnt_ref[...]
    for c in range(x_ref.shape[0] // ch):
        rs = slice(c * ch, (c + 1) * ch)
        mix = jnp.dot(ya_ref[rs, :], w_ref[0, :wa, :], preferred_element_type=F32)
        mix = mix + jnp.dot(yb_ref[rs, :], w_ref[0, wa:wb, :], preferred_element_type=F32)
        mix = mix + jnp.dot(yc_ref[rs, :], w_ref[0, wb:, :], preferred_element_type=F32)
        x1 = _layer_norm(DEEPNORM_ALPHA * x_ref[rs, :] + mix, g_ref[...], b_ref[...])
        x1_ref[rs, :] = x1
        _rows_to_slabs(x1s_ref.at[pl.ds(c * ch * slab, ch * slab), :], x1, slab)
        x_hi = x1.astype(BF16)
        x_lo = (x1 - x_hi.astype(F32)).astype(BF16)
        r = jnp.dot(x_hi, wr_ref[...], preferred_element_type=F32)
        logits = (r[:, :LANES] + r[:, LANES:] + jnp.dot(x_lo, wr_ref[:, :LANES], preferred_element_type=F32)
                  + br_ref[...])
        w1, w2, i1, i2 = _route(logits)

        hit1, hit2 = lane == i1, lane == i2
        oh1, oh2 = hit1.astype(F32), hit2.astype(F32)
        pre = jnp.dot(before, jnp.concatenate([oh1, oh2], axis=-1).astype(BF16), preferred_element_type=F32)
        tot1 = jnp.sum(oh1, axis=0, keepdims=True)
        tot2 = jnp.sum(oh2, axis=0, keepdims=True)
        rank1 = jnp.sum(jnp.where(hit1, pre[:, :LANES] + carry, 0.0), axis=-1, keepdims=True)
        rank2 = jnp.sum(jnp.where(hit2, pre[:, LANES:] + (carry + tot1), 0.0), axis=-1, keepdims=True)
        carry = carry + tot1 + tot2

        out = jnp.where(lane == 0, w1, 0.0)
        out = jnp.where(lane == 1, w2, out)
        out = jnp.where(lane == 2, i1.astype(F32), out)
        out = jnp.where(lane == 3, i2.astype(F32), out)
        out = jnp.where(lane == 4, rank1, out)
        out = jnp.where(lane == 5, rank2, out)
        rt_ref[rs, :] = out
    cnt_ref[...] = carry


def _outproj(ya, yb, yc, x, w_out, ln_g, ln_b, wr, br, l):
    n, d = x.shape
    tm = min(n, TM_OUT)
    assert tm % OUT_CHAIN_ROWS == 0
    slab = d // LANES
    full = lambda shape: pl.BlockSpec(shape, lambda i: (0,) * len(shape))
    rows = lambda w: pl.BlockSpec((tm, w), lambda i: (i, 0))
    wr_hi = wr.astype(BF16)
    wr_lo = (wr - wr_hi.astype(F32)).astype(BF16)
    return pl.pallas_call(
        _outproj_kernel,
        out_shape=(jax.ShapeDtypeStruct((n, d), F32), jax.ShapeDtypeStruct((n * slab, LANES), F32),
                   jax.ShapeDtypeStruct((n, LANES), F32), jax.ShapeDtypeStruct((1, LANES), F32)),
        grid=(n // tm,),
        in_specs=[rows(ya.shape[1]), rows(yb.shape[1]), rows(yc.shape[1]), rows(d),
                  pl.BlockSpec((1, d, d), lambda i: (l, 0, 0), pipeline_mode=pl.Buffered(1)), full((1, d)), full((1, d)),
                  full((d, 2 * LANES)), full((1, LANES))],
        out_specs=(rows(d), pl.BlockSpec((tm * slab, LANES), lambda i: (i, 0)), rows(LANES), full((1, LANES))),
        compiler_params=_params("arbitrary"),
        name="out_proj_ln_router",
    )(ya, yb, yc, x, w_out, ln_g[None, :], ln_b[None, :], jnp.concatenate([wr_hi, wr_lo], axis=1), br)


def _moe_kernel(src_ref, exp_ref, used_ref, x_hbm, wg_ref, wu_ref, wd_ref, y_ref,
                xbuf0, xbuf1, xbuf2, xbuf3, wgb, wub, wdb, gsem):
    xbufs = (xbuf0, xbuf1, xbuf2, xbuf3)
    nx = len(xbufs)
    i = pl.program_id(0)
    nb = pl.num_programs(0) - 1
    d = wgb.shape[0]
    slab = d // LANES
    vs = MOE_VMEM_STRIDE
    rows = xbuf0.shape[0] // vs
    used = used_ref[0]

    def gather_copy(tok, r, slot):
        src = x_hbm.at[pl.ds(pl.multiple_of(tok * slab, slab), slab), :]
        return pltpu.make_async_copy(src, xbufs[slot].at[pl.ds(r * vs, slab), :], gsem.at[slot])

    def wait_gather(slot):
        pltpu.make_async_copy(x_hbm.at[pl.ds(0, rows * slab), :], xbuf0.at[pl.ds(0, rows * slab), :],
                              gsem.at[slot]).wait()

    @pl.when(i == 0)
    def _():
        for blk in range(2):
            def issue(r, carry):
                gather_copy(src_ref[jnp.minimum(blk, nb - 1) * rows + r], r, blk).start()
                return carry
            lax.fori_loop(0, rows, issue, 0, unroll=8)

    cur = jnp.minimum(i, nb - 1)
    @pl.when((i < used) & ((i == 0) | (exp_ref[cur] != exp_ref[jnp.maximum(cur - 1, 0)])))
    def _():
        wgb[...] = wg_ref[0, 0].astype(BF16)
        wub[...] = wu_ref[0, 0].astype(BF16)
        wdb[...] = wd_ref[0, 0].astype(BF16)

    def compute_block(phase):
        wait_gather(phase)
        nxt = jnp.minimum(i + 2, nb - 1) * rows
        for r in range(rows):
            gather_copy(src_ref[nxt + r], r, (phase + 2) % nx).start()
        xb = _slabs_to_rows(xbufs[phase], rows, d, vs).astype(BF16)
        hg = jnp.dot(xb, wgb[...], preferred_element_type=F32)
        hu = jnp.dot(xb, wub[...], preferred_element_type=F32)
        hh = (jax.nn.silu(hg) * hu).astype(BF16)
        _rows_to_slabs(y_ref, jnp.dot(hh, wdb[...], preferred_element_type=F32), slab)

    for phase in range(nx):
        pl.when((i < used) & (i % nx == phase))(functools.partial(compute_block, phase))

    @pl.when((i >= used) & (i < nb))
    def _():
        y_ref[...] = jnp.zeros_like(y_ref)

    @pl.when(i == used)
    def _():
        wait_gather(i % nx)
        wait_gather((i + 1) % nx)


def _moe(slot_src, blk_exp, used, x1s, w_gate, w_up, w_down, l):
    d, de = w_gate.shape[2:]
    slab = d // LANES
    nb = blk_exp.shape[0]
    rows = slot_src.shape[0] // nb
    blk = lambda i, src, exp, used: (l, exp[jnp.minimum(i, nb - 1)], 0, 0)
    return pl.pallas_call(
        _moe_kernel,
        out_shape=jax.ShapeDtypeStruct((nb * rows * slab, LANES), F32),
        grid_spec=pltpu.PrefetchScalarGridSpec(
            num_scalar_prefetch=3,
            grid=(nb + 1,),
            in_specs=[pl.BlockSpec(memory_space=pl.ANY),
                      pl.BlockSpec((1, 1, d, de), blk), pl.BlockSpec((1, 1, d, de), blk),
                      pl.BlockSpec((1, 1, de, d), blk)],
            out_specs=pl.BlockSpec((rows * slab, LANES), lambda i, src, exp, used: (jnp.minimum(i, nb - 1), 0)),
            scratch_shapes=[pltpu.VMEM((rows * MOE_VMEM_STRIDE, LANES), F32)] * MOE_RING + [
                            pltpu.VMEM((d, de), BF16), pltpu.VMEM((d, de), BF16), pltpu.VMEM((de, d), BF16),
                            pltpu.SemaphoreType.DMA((MOE_RING,))]),
        compiler_params=_params("arbitrary"),
        name="moe_experts",
    )(slot_src, blk_exp, used, x1s, w_gate, w_up, w_down)


def _combine_kernel(pos_ref, y_hbm, x_ref, rt_ref, g_ref, b_ref, o_ref, yb00, yb01, yb10, yb11, sem):
    ybufs = ((yb00, yb01), (yb10, yb11))
    i = pl.program_id(0)
    nt = pl.num_programs(0)
    tm, d = x_ref.shape
    slab = d // LANES
    vs = MOE_VMEM_STRIDE

    def row_copy(p, r, k, slot):
        src = y_hbm.at[pl.ds(pl.multiple_of(p * slab, slab), slab), :]
        return pltpu.make_async_copy(src, ybufs[slot][k].at[pl.ds(r * vs, slab), :], sem.at[slot])

    def wait_rows(slot):
        for k in range(TOP_K):
            pltpu.make_async_copy(y_hbm.at[pl.ds(0, tm * slab), :], yb00.at[pl.ds(0, tm * slab), :],
                                  sem.at[slot]).wait()

    @pl.when(i == 0)
    def _():
        def issue(r, carry):
            for k in range(TOP_K):
                row_copy(pos_ref[r * TOP_K + k], r, k, 0).start()
            return carry
        lax.fori_loop(0, tm, issue, 0, unroll=4)

    def tile(slot):
        wait_rows(slot)
        nxt = jnp.minimum(i + 1, nt - 1) * (tm * TOP_K)
        for r in range(tm):
            for k in range(TOP_K):
                row_copy(pos_ref[nxt + r * TOP_K + k], r, k, 1 - slot).start()
        rt = rt_ref[...]
        ffn = (_slabs_to_rows(ybufs[slot][0], tm, d, vs) * rt[:, 0:1]
               + _slabs_to_rows(ybufs[slot][1], tm, d, vs) * rt[:, 1:2])
        o_ref[...] = _layer_norm(DEEPNORM_ALPHA * x_ref[...] + ffn, g_ref[...], b_ref[...])

    for parity in range(2):
        pl.when(i % 2 == parity)(functools.partial(tile, parity))

    for parity in range(2):
        pl.when((i == nt - 1) & (i % 2 == parity))(functools.partial(wait_rows, 1 - parity))


def _combine(pos, ys, x1, route, ln_g, ln_b):
    n, d = x1.shape
    tm = min(n, TM_COMB)
    return pl.pallas_call(
        _combine_kernel,
        out_shape=jax.ShapeDtypeStruct((n, d), F32),
        grid_spec=pltpu.PrefetchScalarGridSpec(
            num_scalar_prefetch=1,
            grid=(n // tm,),
            in_specs=[pl.BlockSpec(memory_space=pl.ANY),
                      pl.BlockSpec((tm, d), lambda i, pos: (i, 0)),
                      pl.BlockSpec((tm, LANES), lambda i, pos: (i, 0)),
                      pl.BlockSpec((1, d), lambda i, pos: (0, 0)),
                      pl.BlockSpec((1, d), lambda i, pos: (0, 0))],
            out_specs=pl.BlockSpec((tm, d), lambda i, pos: (i, 0)),
            scratch_shapes=[pltpu.VMEM((tm * MOE_VMEM_STRIDE, LANES), F32)] * (2 * TOP_K)
                           + [pltpu.SemaphoreType.DMA((2,))]),
        compiler_params=_params("arbitrary"),
        name="moe_combine_ln",
    )(pos, ys, x1, route, ln_g[None, :], ln_b[None, :])


def _dispatch_plan(route, counts, rows):
    n = route.shape[0]
    nk = n * TOP_K
    nb = (nk + rows - 1) // rows + N_EXPERTS
    counts = counts[0, :N_EXPERTS].astype(jnp.int32)
    padded = ((counts + rows - 1) // rows) * rows
    pad_end = jnp.cumsum(padded)
    pad_start = pad_end - padded
    flat_e = route[:, 2:2 + TOP_K].astype(jnp.int32).reshape(nk)
    rank = route[:, 2 + TOP_K:2 + 2 * TOP_K].astype(jnp.int32).reshape(nk)
    dest = pad_start[flat_e] + rank
    asg = jnp.full((nb * rows,), -1, jnp.int32).at[dest].set(jnp.arange(nk, dtype=jnp.int32))
    slot_src = jnp.where(asg >= 0, asg // TOP_K, 0)
    blk_exp = jnp.minimum(
        jnp.sum((pad_end[None, :] <= (jnp.arange(nb, dtype=jnp.int32) * rows)[:, None]).astype(jnp.int32), axis=1),
        N_EXPERTS - 1).astype(jnp.int32)
    used = (pad_end[-1] // rows).astype(jnp.int32).reshape(1)
    return slot_src, blk_exp, used, dest.astype(jnp.int32)


def _pad_q_heads(w_uq):
    lead = w_uq.shape[:-1]
    w = w_uq.reshape(*lead, MLA_HEADS, MLA_QK_DIM)
    w = jnp.pad(w, [(0, 0)] * len(lead) + [(0, 0), (0, Q_HEAD_PAD - MLA_QK_DIM)])
    return w.reshape(*lead, MLA_HEADS * Q_HEAD_PAD)


def kernel(x, positions, w_in, sgu_ln_g, sgu_ln_b, sgu_ws, sgu_b, hgrn_lb_logits, hgrn_norm_g, mla_qn_g, mla_w_uq, mla_kvn_g, mla_w_ukv, w_out, ln1_g, ln1_b, router_group_w, router_group_b, router_expert_w, router_expert_b, expert_w_gate, expert_w_up, expert_w_down, ln2_g, ln2_b):
    B, S, D = x.shape
    n = B * S
    depth = w_in.shape[0]
    tab_c, tab_s = _rope_tables(positions)
    lb_cum = jnp.cumsum(jax.nn.softmax(hgrn_lb_logits.astype(F32), axis=0), axis=0)
    lower_bounds = lb_cum - lb_cum[0:1]
    xt = x.reshape(n, D)
    w_in_b = jnp.pad(w_in, ((0, 0), (0, 0), (0, D_IN_PAD - D_IN))).astype(BF16)
    w_uq_b = _pad_q_heads(mla_w_uq).astype(BF16)
    w_ukv_b = mla_w_ukv.astype(BF16)
    w_out_b = w_out.astype(BF16)
    for l in range(depth):
        h = _inproj(xt, w_in_b, l)
        y_a = _sgu(h, sgu_ln_g[l], sgu_ln_b[l], sgu_ws[l], sgu_b[l])
        y_b = _hgrn(h, lower_bounds[l], hgrn_norm_g[l], B)
        q, kn, v, kr = _mla_up(h, tab_c, tab_s, mla_qn_g[l], w_uq_b, mla_kvn_g[l], w_ukv_b, l)
        y_c = _attention(q, kn, kr, v, B)
        wr = jnp.pad(jnp.concatenate([router_expert_w[l], router_group_w[l]], axis=1),
                     ((0, 0), (0, LANES - N_EXPERTS - N_GROUPS)))
        br = jnp.pad(jnp.concatenate([router_expert_b[l], router_group_b[l]]), (0, LANES - N_EXPERTS - N_GROUPS))
        x1, x1s, route, counts = _outproj(y_a, y_b, y_c, xt, w_out_b, ln1_g[l], ln1_b[l], wr, br[None, :], l)
        slot_src, blk_exp, used, pos = _dispatch_plan(route, counts, MOE_ROWS)
        ys = _moe(slot_src, blk_exp, used, x1s, expert_w_gate, expert_w_up, expert_w_down, l)
        xt = _combine(pos, ys, x1, route, ln2_g[l], ln2_b[l])
    return xt.reshape(B, S, D)
```

```python
import functools

import jax
import jax.numpy as jnp
from jax import lax
from jax.experimental import pallas as pl
from jax.experimental.pallas import tpu as pltpu

D_MODEL = 2048
DEPTH = 4
CHUNK = 128
SGU_WIDTH = D_MODEL // 4
SGU_GROUPS = 4
SGU_GROUP_DIM = SGU_WIDTH // SGU_GROUPS
HGRN_WIDTH = D_MODEL // 4
HGRN_HEADS = 4
HGRN_HEAD_DIM = HGRN_WIDTH // HGRN_HEADS
MLA_HEADS = 8
MLA_V_DIM = (D_MODEL - SGU_WIDTH - HGRN_WIDTH) // MLA_HEADS
MLA_NOPE_DIM = 128
MLA_ROPE_DIM = 64
MLA_QK_DIM = MLA_NOPE_DIM + MLA_ROPE_DIM
Q_LORA_RANK = D_MODEL // 4
KV_LORA_RANK = D_MODEL // 4
ROPE_THETA = 10000.0
N_GROUPS = 4
EXPERTS_PER_GROUP = 8
N_EXPERTS = N_GROUPS * EXPERTS_PER_GROUP
TOP_K = 2
D_EXPERT = D_MODEL // 4
NORM_EPS = 1e-5
DEEPNORM_ALPHA = (2 * DEPTH) ** 0.25
D_IN = SGU_WIDTH * 2 + HGRN_WIDTH * 4 + Q_LORA_RANK + KV_LORA_RANK + MLA_ROPE_DIM

LANES = 128
D_IN_PAD = ((D_IN + LANES - 1) // LANES) * LANES
Q_HEAD_PAD = 2 * LANES
VMEM_LIMIT = 56 * 1024 * 1024

TM_INPROJ = 512
TN_INPROJ = D_IN_PAD // 3
TM_SGU = 512
HGRN_CHUNKS_PER_STEP = 4
TM_MLA = 512
TQ_ATTN = 2048
TK_ATTN = 512
TM_OUT = 512
OUT_CHAIN_ROWS = 256
MOE_ROWS = 256
MOE_RING = 4
MOE_VMEM_STRIDE = 20
TM_COMB = 512

BF16 = jnp.bfloat16
F32 = jnp.float32
LOG2_E = 1.4426950408889634


def _params(*sem):
    return pltpu.CompilerParams(dimension_semantics=sem, vmem_limit_bytes=VMEM_LIMIT)


def _rope_kernel(pos_ref, inv_ref, sgn_ref, c_ref, s_ref):
    ang = pos_ref[...].astype(F32) * inv_ref[...]
    live = sgn_ref[...] != 0.0
    c_ref[...] = jnp.where(live, jnp.cos(ang), 0.0)
    s_ref[...] = jnp.sin(ang) * sgn_ref[...]


def _rope_tables(positions):
    n = positions.size
    tm = min(n, 1024)
    half = MLA_ROPE_DIM // 2
    inv = 1.0 / (ROPE_THETA ** (jnp.arange(0, MLA_ROPE_DIM, 2, dtype=F32) / MLA_ROPE_DIM))
    zeros = jnp.zeros((LANES - 2 * half,), F32)
    inv_t = jnp.concatenate([inv, inv, zeros])[None, :]
    sgn_t = jnp.concatenate([-jnp.ones((half,), F32), jnp.ones((half,), F32), zeros])[None, :]
    row = pl.BlockSpec((1, LANES), lambda i: (0, 0))
    return pl.pallas_call(
        _rope_kernel,
        out_shape=(jax.ShapeDtypeStruct((n, LANES), F32),) * 2,
        grid=(n // tm,),
        in_specs=[pl.BlockSpec((tm, 1), lambda i: (i, 0)), row, row],
        out_specs=(pl.BlockSpec((tm, LANES), lambda i: (i, 0)),) * 2,
        compiler_params=_params("parallel"),
        name="rope_tables",
    )(positions.reshape(n, 1), inv_t, sgn_t)


def _inproj_kernel(x_ref, w_ref, o_ref):
    o_ref[...] = jnp.dot(x_ref[...].astype(BF16), w_ref[0], preferred_element_type=F32)


def _inproj(x, w, l):
    n, d = x.shape
    tm = min(n, TM_INPROJ)
    return pl.pallas_call(
        _inproj_kernel,
        out_shape=jax.ShapeDtypeStruct((n, D_IN_PAD), F32),
        grid=(D_IN_PAD // TN_INPROJ, n // tm),
        in_specs=[pl.BlockSpec((tm, d), lambda j, i: (i, 0)),
                  pl.BlockSpec((1, d, TN_INPROJ), lambda j, i: (l, 0, j))],
        out_specs=pl.BlockSpec((tm, TN_INPROJ), lambda j, i: (i, j)),
        compiler_params=_params("parallel", "parallel"),
        name="in_proj",
    )(x, w)


def _sgu_kernel(u_ref, v_ref, g_ref, b_ref, ws_ref, bs_ref, o_ref):
    u = jax.nn.gelu(u_ref[...])
    v = jax.nn.gelu(v_ref[...])
    mu = jnp.mean(v, axis=-1, keepdims=True)
    vc = v - mu
    var = jnp.mean(vc * vc, axis=-1, keepdims=True)
    vn = (vc * lax.rsqrt(var + NORM_EPS) * g_ref[...] + b_ref[...]).astype(BF16)
    row = lax.broadcasted_iota(jnp.int32, (CHUNK, CHUNK), 0)
    col = lax.broadcasted_iota(jnp.int32, (CHUNK, CHUNK), 1)
    causal = row >= col
    bs = bs_ref[...]
    for g in range(SGU_GROUPS):
        w = jnp.where(causal, ws_ref[g], 0.0).astype(BF16)
        cs = slice(g * SGU_GROUP_DIM, (g + 1) * SGU_GROUP_DIM)
        for c in range(u.shape[0] // CHUNK):
            rs = slice(c * CHUNK, (c + 1) * CHUNK)
            mixed = jnp.dot(w, vn[rs, cs], preferred_element_type=F32) + bs[:, g:g + 1]
            o_ref[rs, cs] = (u[rs, cs] * mixed).astype(o_ref.dtype)


def _sgu(h, ln_g, ln_b, ws, bs):
    n = h.shape[0]
    tm = min(n, TM_SGU)
    full = lambda shape: pl.BlockSpec(shape, lambda i: (0,) * len(shape))
    return pl.pallas_call(
        _sgu_kernel,
        out_shape=jax.ShapeDtypeStruct((n, SGU_WIDTH), BF16),
        grid=(n // tm,),
        in_specs=[pl.BlockSpec((tm, SGU_WIDTH), lambda i: (i, 0)),
                  pl.BlockSpec((tm, SGU_WIDTH), lambda i: (i, 1)),
                  full((1, SGU_WIDTH)), full((1, SGU_WIDTH)),
                  full((SGU_GROUPS, CHUNK, CHUNK)), full((CHUNK, SGU_GROUPS))],
        out_specs=pl.BlockSpec((tm, SGU_WIDTH), lambda i: (i, 0)),
        compiler_params=_params("parallel"),
        name="sgu",
    )(h, h, ln_g[None, :], ln_b[None, :], ws, bs.T)


def _dot_nt(a, b):
    return lax.dot_general(a, b, (((1,), (1,)), ((), ())), preferred_element_type=F32)


def _hgrn_kernel(q_ref, f_ref, i_ref, g_ref, la_ref, lc_ref, om_ref, ng_ref, o_ref, state_ref):
    @pl.when(pl.program_id(1) == 0)
    def _():
        state_ref[...] = jnp.zeros_like(state_ref)

    T, K = CHUNK, HGRN_HEAD_DIM
    row = lax.broadcasted_iota(jnp.int32, (T, T), 0)
    col = lax.broadcasted_iota(jnp.int32, (T, T), 1)
    tril = (row >= col).astype(F32)
    SUB = 8
    wide = {}
    for c, h in [(c, h) for c in range(q_ref.shape[0] // T) for h in range(HGRN_HEADS)]:
        rs = slice(c * T, (c + 1) * T)
        cs = slice(h * K, (h + 1) * K)
        if c not in wide:
            fx = f_ref[rs, :]
            log_f = jnp.logaddexp(la_ref[...], lc_ref[...] + jax.nn.log_sigmoid(fx))
            G_all = jnp.dot(tril, log_f, preferred_element_type=F32, precision=lax.Precision.HIGHEST)
            wide = {c: (G_all, om_ref[...] * jax.nn.sigmoid(-fx), jax.nn.silu(q_ref[rs, :]))}
        G, k, q = (a[:, cs] for a in wide[c])
        v = i_ref[rs, cs]
        A = jnp.zeros((T, T), F32)
        m = T // 2
        while m >= SUB:
            blk = 2 * m
            g_mid = jnp.concatenate(
                [jnp.broadcast_to(G[b * blk + m - 1:b * blk + m, :], (blk, K)) for b in range(T // blk)], axis=0)
            lower = (row % blk) >= m
            qs = jnp.where(lower, q * jnp.exp(G - g_mid), 0.0).astype(BF16)
            ks = jnp.where(lower, 0.0, k * jnp.exp(g_mid - G)).astype(BF16)
            same = (row // blk) == (col // blk)
            A = A + jnp.where(same, _dot_nt(qs, ks), 0.0)
            m //= 2
        rmod = row % SUB
        k3, G3 = k.reshape(T // SUB, SUB, K), G.reshape(T // SUB, SUB, K)
        for d in range(SUB):
            if d == 0:
                prod = q * k
            else:
                kd = pltpu.roll(k3, d, axis=1).reshape(T, K)
                gd = pltpu.roll(G3, d, axis=1).reshape(T, K)
                prod = jnp.where(rmod >= d, q * kd * jnp.exp(G - gd), 0.0)
            A = A + jnp.where(col == row - d, jnp.sum(prod, axis=-1, keepdims=True), 0.0)

        st = state_ref[h]
        o = jnp.dot(A.astype(BF16), v.astype(BF16), preferred_element_type=F32)
        o = o + _dot_nt((q * jnp.exp(G)).astype(BF16), st.astype(BF16))
        g_last = G[T - 1:T, :]
        k_dec = (k * jnp.exp(g_last - G)).astype(BF16)
        state_ref[h] = jnp.exp(g_last) * st + jnp.dot(v.T.astype(BF16), k_dec, preferred_element_type=F32)

        ms = jnp.mean(o * o, axis=-1, keepdims=True)
        o = o * lax.rsqrt(ms + NORM_EPS) * ng_ref[:, cs]
        o_ref[rs, cs] = (o * jax.nn.silu(g_ref[rs, cs])).astype(o_ref.dtype)


def _hgrn(h, lb, norm_g, batch):
    n = h.shape[0]
    tb = CHUNK * HGRN_CHUNKS_PER_STEP
    nc = n // batch // tb
    W = HGRN_WIDTH
    col = lambda j: pl.BlockSpec((tb, W), lambda b, c: (b * nc + c, j))
    vec = pl.BlockSpec((1, W), lambda b, c: (0, 0))
    base = 2 * SGU_WIDTH // W
    return pl.pallas_call(
        _hgrn_kernel,
        out_shape=jax.ShapeDtypeStruct((n, W), BF16),
        grid=(batch, nc),
        in_specs=[col(base), col(base + 1), col(base + 2), col(base + 3), vec, vec, vec, vec],
        out_specs=pl.BlockSpec((tb, W), lambda b, c: (b * nc + c, 0)),
        scratch_shapes=[pltpu.VMEM((HGRN_HEADS, HGRN_HEAD_DIM, HGRN_HEAD_DIM), F32)],
        compiler_params=_params("parallel", "arbitrary"),
        name="hgrn2",
    )(h, h, h, h, jnp.log(lb)[None, :], jnp.log1p(-lb)[None, :], (1.0 - lb)[None, :], norm_g[None, :])


def _rms(x, g):
    ms = jnp.mean(x * x, axis=-1, keepdims=True)
    return x * lax.rsqrt(ms + NORM_EPS) * g


def _rope(x, c, s):
    half = MLA_ROPE_DIM // 2
    lane = lax.broadcasted_iota(jnp.int32, x.shape, 1)
    swapped = jnp.where(lane < half, pltpu.roll(x, LANES - half, axis=1), pltpu.roll(x, half, axis=1))
    return x * c + swapped * s


def _mla_up_kernel(cq_ref, ckv_ref, kr_ref, c_ref, s_ref, qg_ref, kvg_ref, wq_ref, wkv_ref,
                   q_ref, kn_ref, v_ref, kro_ref):
    c, s = c_ref[...], s_ref[...]
    scale = MLA_QK_DIM ** -0.5 * LOG2_E
    q = jnp.dot(_rms(cq_ref[...], qg_ref[...]).astype(BF16), wq_ref[0], preferred_element_type=F32)
    kv = jnp.dot(_rms(ckv_ref[...], kvg_ref[...]).astype(BF16), wkv_ref[0], preferred_element_type=F32)
    for h in range(MLA_HEADS):
        base = h * Q_HEAD_PAD
        q_ref[h, :, :LANES] = (q[:, base:base + LANES] * scale).astype(BF16)
        q_ref[h, :, LANES:] = (_rope(q[:, base + LANES:base + 2 * LANES], c, s) * scale).astype(BF16)
        kbase = h * (MLA_NOPE_DIM + MLA_V_DIM)
        kn_ref[h] = kv[:, kbase:kbase + MLA_NOPE_DIM].astype(BF16)
        v_ref[h] = kv[:, kbase + MLA_NOPE_DIM:kbase + MLA_NOPE_DIM + MLA_V_DIM].astype(BF16)
    kro_ref[...] = _rope(kr_ref[...], c, s).astype(BF16)


def _mla_up(h, tab_c, tab_s, qn_g, w_uq, kvn_g, w_ukv, l):
    n = h.shape[0]
    tm = min(n, TM_MLA)
    H = MLA_HEADS
    cq_blk = (2 * SGU_WIDTH + 4 * HGRN_WIDTH) // Q_LORA_RANK
    kr_blk = (2 * SGU_WIDTH + 4 * HGRN_WIDTH + Q_LORA_RANK + KV_LORA_RANK) // LANES
    full = lambda shape: pl.BlockSpec(shape, lambda i: (0,) * len(shape))
    heads = lambda w: pl.BlockSpec((H, tm, w), lambda i: (0, i, 0))
    return pl.pallas_call(
        _mla_up_kernel,
        out_shape=(jax.ShapeDtypeStruct((H, n, Q_HEAD_PAD), BF16),
                   jax.ShapeDtypeStruct((H, n, MLA_NOPE_DIM), BF16),
                   jax.ShapeDtypeStruct((H, n, MLA_V_DIM), BF16),
                   jax.ShapeDtypeStruct((n, LANES), BF16)),
        grid=(n // tm,),
        in_specs=[pl.BlockSpec((tm, Q_LORA_RANK), lambda i: (i, cq_blk)),
                  pl.BlockSpec((tm, KV_LORA_RANK), lambda i: (i, cq_blk + 1)),
                  pl.BlockSpec((tm, LANES), lambda i: (i, kr_blk)),
                  pl.BlockSpec((tm, LANES), lambda i: (i, 0)),
                  pl.BlockSpec((tm, LANES), lambda i: (i, 0)),
                  full((1, Q_LORA_RANK)), full((1, KV_LORA_RANK)),
                  pl.BlockSpec((1, Q_LORA_RANK, H * Q_HEAD_PAD), lambda i: (l, 0, 0)),
                  pl.BlockSpec((1, KV_LORA_RANK, H * (MLA_NOPE_DIM + MLA_V_DIM)), lambda i: (l, 0, 0))],
        out_specs=(heads(Q_HEAD_PAD), heads(MLA_NOPE_DIM), heads(MLA_V_DIM),
                   pl.BlockSpec((tm, LANES), lambda i: (i, 0))),
        compiler_params=_params("parallel"),
        name="mla_up",
    )(h, h, h, tab_c, tab_s, qn_g[None, :], kvn_g[None, :], w_uq, w_ukv)


def _attn_kernel(q_ref, kn_ref, kr_ref, v_ref, o_ref, m_ref, l_ref, acc_ref):
    i = pl.program_id(2)
    nch, tk, _ = acc_ref.shape
    m_ref[...] = jnp.full_like(m_ref, -jnp.inf)
    l_ref[...] = jnp.zeros_like(l_ref)
    acc_ref[...] = jnp.zeros_like(acc_ref)

    def chain(c, j, masked):
        start = pl.multiple_of(j * tk, tk)
        q = q_ref[0, c * tk:(c + 1) * tk, :]
        k = jnp.concatenate([kn_ref[0, pl.ds(start, tk), :], kr_ref[pl.ds(start, tk), :]], axis=-1)
        s = _dot_nt(q, k)
        if masked:
            row = lax.broadcasted_iota(jnp.int32, s.shape, 0)
            col = lax.broadcasted_iota(jnp.int32, s.shape, 1)
            s = jnp.where(col <= row, s, -jnp.inf)
        chunks = [s[:, t * LANES:(t + 1) * LANES] for t in range(tk // LANES)]
        smax = functools.reduce(jnp.maximum, chunks)
        m_prev = m_ref[c]
        m_new = jnp.maximum(m_prev, jnp.max(smax, axis=-1, keepdims=True))
        alpha = jnp.exp2(m_prev - m_new)
        ps = [jnp.exp2(ch - m_new) for ch in chunks]
        l_ref[c] = alpha * l_ref[c] + functools.reduce(jnp.add, ps)
        p = jnp.concatenate([x.astype(BF16) for x in ps], axis=-1)
        acc_ref[c] = alpha * acc_ref[c] + jnp.dot(p, v_ref[0, pl.ds(start, tk), :], preferred_element_type=F32)
        m_ref[c] = m_new

    def body(j, carry):
        for c in range(nch):
            chain(c, j, masked=False)
        return carry

    lax.fori_loop(0, nch * i, body, 0)
    for t in range(nch):
        for c in range(t, nch):
            chain(c, nch * i + t, masked=(c == t))
    for c in range(nch):
        l = jnp.sum(l_ref[c], axis=-1, keepdims=True)
        o_ref[c * tk:(c + 1) * tk, :] = (acc_ref[c] / l).astype(o_ref.dtype)


def _attention(q, kn, kr, v, batch):
    H, n, _ = q.shape
    S = n // batch
    tq = min(S, TQ_ATTN)
    tk = min(tq, TK_ATTN)
    nq = S // tq
    return pl.pallas_call(
        _attn_kernel,
        out_shape=jax.ShapeDtypeStruct((n, H * MLA_V_DIM), BF16),
        grid=(batch, H, nq),
        in_specs=[pl.BlockSpec((1, tq, Q_HEAD_PAD), lambda b, h, i: (h, b * nq + i, 0)),
                  pl.BlockSpec((1, S, MLA_NOPE_DIM), lambda b, h, i: (h, b, 0)),
                  pl.BlockSpec((S, LANES), lambda b, h, i: (b, 0)),
                  pl.BlockSpec((1, S, MLA_V_DIM), lambda b, h, i: (h, b, 0))],
        out_specs=pl.BlockSpec((tq, MLA_V_DIM), lambda b, h, i: (b * nq + i, h)),
        scratch_shapes=[pltpu.VMEM((tq // tk, tk, LANES), F32), pltpu.VMEM((tq // tk, tk, LANES), F32),
                        pltpu.VMEM((tq // tk, tk, MLA_V_DIM), F32)],
        compiler_params=_params("parallel", "parallel", "arbitrary"),
        name="mla_attention",
    )(q, kn, kr, v)


def _layer_norm(x, g, b):
    mu = jnp.mean(x, axis=-1, keepdims=True)
    xc = x - mu
    var = jnp.mean(xc * xc, axis=-1, keepdims=True)
    return xc * lax.rsqrt(var + NORM_EPS) * g + b


def _first_index(hit, lane):
    return jnp.min(jnp.where(hit, lane, LANES), axis=-1, keepdims=True)


def _route(logits):
    lane = lax.broadcasted_iota(jnp.int32, logits.shape, 1)
    neg = -jnp.inf
    gl = jnp.where((lane >= N_EXPERTS) & (lane < N_EXPERTS + N_GROUPS), logits, neg)
    ge = jnp.exp(gl - jnp.max(gl, axis=-1, keepdims=True))
    gp = ge / jnp.sum(ge, axis=-1, keepdims=True)
    g_val = jnp.max(gp, axis=-1, keepdims=True)
    g_idx = _first_index(gp == g_val, lane) - N_EXPERTS
    el = jnp.where((lane // EXPERTS_PER_GROUP == g_idx) & (lane < N_EXPERTS), logits, neg)
    m1 = jnp.max(el, axis=-1, keepdims=True)
    i1 = _first_index(el == m1, lane)
    el2 = jnp.where(lane == i1, neg, el)
    m2 = jnp.max(el2, axis=-1, keepdims=True)
    i2 = _first_index(el2 == m2, lane)
    e2 = jnp.exp(m2 - m1)
    den = 1.0 + e2
    w1 = g_val * (1.0 / den)
    w2 = g_val * (e2 / den)
    return w1, w2, i1, i2


def _rows_to_slabs(ref, x, stride):
    m, d = x.shape
    for s in range(d // LANES):
        ref[pl.ds(s, m, stride=stride), :] = x[:, s * LANES:(s + 1) * LANES]


def _slabs_to_rows(ref, m, d, stride):
    return jnp.concatenate([ref[pl.ds(s, m, stride=stride), :] for s in range(d // LANES)], axis=-1)


def _outproj_kernel(ya_ref, yb_ref, yc_ref, x_ref, w_ref, g_ref, b_ref, wr_ref, br_ref,
                    x1_ref, x1s_ref, rt_ref, cnt_ref):
    @pl.when(pl.program_id(0) == 0)
    def _():
        cnt_ref[...] = jnp.zeros_like(cnt_ref)

    wa = SGU_WIDTH
    wb = SGU_WIDTH + HGRN_WIDTH
    d = x_ref.shape[1]
    slab = d // LANES
    ch = OUT_CHAIN_ROWS
    lane = lax.broadcasted_iota(jnp.int32, (ch, LANES), 1)
    row = lax.broadcasted_iota(jnp.int32, (ch, ch), 0)
    col = lax.broadcasted_iota(jnp.int32, (ch, ch), 1)
    before = (col < row).astype(BF16)
    carry = cnt_ref[...]
    for c in range(x_ref.shape[0] // ch):
        rs = slice(c * ch, (c + 1) * ch)
        mix = jnp.dot(ya_ref[rs, :], w_ref[0, :wa, :], preferred_element_type=F32)
        mix = mix + jnp.dot(yb_ref[rs, :], w_ref[0, wa:wb, :], preferred_element_type=F32)
        mix = mix + jnp.dot(yc_ref[rs, :], w_ref[0, wb:, :], preferred_element_type=F32)
        x1 = _layer_norm(DEEPNORM_ALPHA * x_ref[rs, :] + mix, g_ref[...], b_ref[...])
        x1_ref[rs, :] = x1
        _rows_to_slabs(x1s_ref.at[pl.ds(c * ch * slab, ch * slab), :], x1, slab)
        x_hi = x1.astype(BF16)
        x_lo = (x1 - x_hi.astype(F32)).astype(BF16)
        r = jnp.dot(x_hi, wr_ref[...], preferred_element_type=F32)
        logits = (r[:, :LANES] + r[:, LANES:] + jnp.dot(x_lo, wr_ref[:, :LANES], preferred_element_type=F32)
                  + br_ref[...])
        w1, w2, i1, i2 = _route(logits)

        hit1, hit2 = lane == i1, lane == i2
        oh1, oh2 = hit1.astype(F32), hit2.astype(F32)
        pre = jnp.dot(before, jnp.concatenate([oh1, oh2], axis=-1).astype(BF16), preferred_element_type=F32)
        tot1 = jnp.sum(oh1, axis=0, keepdims=True)
        tot2 = jnp.sum(oh2, axis=0, keepdims=True)
        rank1 = jnp.sum(jnp.where(hit1, pre[:, :LANES] + carry, 0.0), axis=-1, keepdims=True)
        rank2 = jnp.sum(jnp.where(hit2, pre[:, LANES:] + (carry + tot1), 0.0), axis=-1, keepdims=True)
        carry = carry + tot1 + tot2

        out = jnp.where(lane == 0, w1, 0.0)
        out = jnp.where(lane == 1, w2, out)
        out = jnp.where(lane == 2, i1.astype(F32), out)
        out = jnp.where(lane == 3, i2.astype(F32), out)
        out = jnp.where(lane == 4, rank1, out)
        out = jnp.where(lane == 5, rank2, out)
        rt_ref[rs, :] = out
    cnt_ref[...] = carry


def _outproj(ya, yb, yc, x, w_out, ln_g, ln_b, wr, br, l):
    n, d = x.shape
    tm = min(n, TM_OUT)
    assert tm % OUT_CHAIN_ROWS == 0
    slab = d // LANES
    full = lambda shape: pl.BlockSpec(shape, lambda i: (0,) * len(shape))
    rows = lambda w: pl.BlockSpec((tm, w), lambda i: (i, 0))
    wr_hi = wr.astype(BF16)
    wr_lo = (wr - wr_hi.astype(F32)).astype(BF16)
    return pl.pallas_call(
        _outproj_kernel,
        out_shape=(jax.ShapeDtypeStruct((n, d), F32), jax.ShapeDtypeStruct((n * slab, LANES), F32),
                   jax.ShapeDtypeStruct((n, LANES), F32), jax.ShapeDtypeStruct((1, LANES), F32)),
        grid=(n // tm,),
        in_specs=[rows(ya.shape[1]), rows(yb.shape[1]), rows(yc.shape[1]), rows(d),
                  pl.BlockSpec((1, d, d), lambda i: (l, 0, 0), pipeline_mode=pl.Buffered(1)), full((1, d)), full((1, d)),
                  full((d, 2 * LANES)), full((1, LANES))],
        out_specs=(rows(d), pl.BlockSpec((tm * slab, LANES), lambda i: (i, 0)), rows(LANES), full((1, LANES))),
        compiler_params=_params("arbitrary"),
        name="out_proj_ln_router",
    )(ya, yb, yc, x, w_out, ln_g[None, :], ln_b[None, :], jnp.concatenate([wr_hi, wr_lo], axis=1), br)


def _moe_kernel(src_ref, dst_ref, exp_ref, used_ref, x_hbm, wg_ref, wu_ref, wd_ref, y_hbm,
                xbuf0, xbuf1, xbuf2, xbuf3, ybuf0, ybuf1, ybuf2, ybuf3, wgb, wub, wdb, gsem, ssem):
    xbufs, ybufs = (xbuf0, xbuf1, xbuf2, xbuf3), (ybuf0, ybuf1, ybuf2, ybuf3)
    nx = len(xbufs)
    i = pl.program_id(0)
    nb = pl.num_programs(0) - 1
    d = wgb.shape[0]
    slab = d // LANES
    vs = MOE_VMEM_STRIDE
    rows = xbuf0.shape[0] // vs
    used = used_ref[0]
    n_real = y_hbm.shape[0] // slab - (nx + 1) * rows

    def gather_copy(tok, r, slot):
        src = x_hbm.at[pl.ds(pl.multiple_of(tok * slab, slab), slab), :]
        return pltpu.make_async_copy(src, xbufs[slot].at[pl.ds(r * vs, slab), :], gsem.at[slot])

    def scatter_copy(dst, r, slot):
        out = y_hbm.at[pl.ds(pl.multiple_of(dst * slab, slab), slab), :]
        return pltpu.make_async_copy(ybufs[slot].at[pl.ds(r * vs, slab), :], out, ssem.at[slot])

    def fill_copy(ybuf, region, sem):
        return pltpu.make_async_copy(ybuf.at[pl.ds(0, rows * slab), :],
                                     y_hbm.at[pl.ds((n_real + region * rows) * slab, rows * slab), :], sem)

    def wait_gather(slot):
        pltpu.make_async_copy(x_hbm.at[pl.ds(0, rows * slab), :], xbuf0.at[pl.ds(0, rows * slab), :],
                              gsem.at[slot]).wait()

    def wait_scatter(slot):
        fill_copy(ybuf0, 0, ssem.at[slot]).wait()

    @pl.when(i == 0)
    def _():
        ybuf3[...] = jnp.zeros_like(ybuf3)
        for t in range(nx - 1):
            fill_copy(ybuf3, 1 + t, ssem.at[t]).start()
        last = fill_copy(ybuf3, nx, ssem.at[nx])
        last.start()
        last.wait()
        for blk in range(2):
            def issue(r, carry):
                gather_copy(src_ref[jnp.minimum(blk, nb - 1) * rows + r], r, blk).start()
                return carry
            lax.fori_loop(0, rows, issue, 0, unroll=8)

    cur = jnp.minimum(i, nb - 1)
    @pl.when((i < used) & ((i == 0) | (exp_ref[cur] != exp_ref[jnp.maximum(cur - 1, 0)])))
    def _():
        wgb[...] = wg_ref[0, 0].astype(BF16)
        wub[...] = wu_ref[0, 0].astype(BF16)
        wdb[...] = wd_ref[0, 0].astype(BF16)

    def compute_block(phase):
        xs = ys = phase
        wait_gather(xs)
        wait_scatter(ys)
        nxt = jnp.minimum(i + 2, nb - 1) * rows
        prv = i * rows
        for r in range(rows):
            gather_copy(src_ref[nxt + r], r, (xs + 2) % nx).start()
            scatter_copy(dst_ref[prv + r], r, (ys - 1) % nx).start()
        xb = _slabs_to_rows(xbufs[xs], rows, d, vs).astype(BF16)
        hg = jnp.dot(xb, wgb[...], preferred_element_type=F32)
        hu = jnp.dot(xb, wub[...], preferred_element_type=F32)
        hh = (jax.nn.silu(hg) * hu).astype(BF16)
        _rows_to_slabs(ybufs[ys], jnp.dot(hh, wdb[...], preferred_element_type=F32), vs)

    def drain_block(ys):
        def issue(r, carry):
            scatter_copy(dst_ref[i * rows + r], r, ys).start()
            return carry
        lax.fori_loop(0, rows, issue, 0, unroll=8)
        wait_scatter(ys)

    for phase in range(nx):
        pl.when((i < used) & (i % nx == phase))(functools.partial(compute_block, phase))

    @pl.when(i == used)
    def _():
        wait_gather(i % nx)
        wait_gather((i + 1) % nx)
        for t in range(nx - 1):
            wait_scatter((i + t) % nx)

    for phase in range(nx):
        pl.when((i == used) & (i >= 1) & ((i - 1) % nx == phase))(functools.partial(drain_block, phase))


def _moe(slot_src, slot_dst, blk_exp, used, x1s, w_gate, w_up, w_down, n_out, l):
    d, de = w_gate.shape[2:]
    slab = d // LANES
    nb = blk_exp.shape[0]
    rows = slot_src.shape[0] // nb
    blk = lambda i, src, dst, exp, used: (l, exp[jnp.minimum(i, nb - 1)], 0, 0)
    return pl.pallas_call(
        _moe_kernel,
        out_shape=jax.ShapeDtypeStruct((n_out * slab, LANES), F32),
        grid_spec=pltpu.PrefetchScalarGridSpec(
            num_scalar_prefetch=4,
            grid=(nb + 1,),
            in_specs=[pl.BlockSpec(memory_space=pl.ANY),
                      pl.BlockSpec((1, 1, d, de), blk), pl.BlockSpec((1, 1, d, de), blk),
                      pl.BlockSpec((1, 1, de, d), blk)],
            out_specs=pl.BlockSpec(memory_space=pl.ANY),
            scratch_shapes=[pltpu.VMEM((rows * MOE_VMEM_STRIDE, LANES), F32)] * (2 * MOE_RING) + [
                            pltpu.VMEM((d, de), BF16), pltpu.VMEM((d, de), BF16), pltpu.VMEM((de, d), BF16),
                            pltpu.SemaphoreType.DMA((MOE_RING,)), pltpu.SemaphoreType.DMA((MOE_RING + 1,))]),
        compiler_params=_params("arbitrary"),
        name="moe_experts",
    )(slot_src, slot_dst, blk_exp, used, x1s, w_gate, w_up, w_down)


def _combine_kernel(y0_ref, y1_ref, x_ref, rt_ref, g_ref, b_ref, o_ref):
    tm, d = x_ref.shape
    slab = d // LANES
    rt = rt_ref[...]
    ffn = _slabs_to_rows(y0_ref, tm, d, slab) * rt[:, 0:1] + _slabs_to_rows(y1_ref, tm, d, slab) * rt[:, 1:2]
    o_ref[...] = _layer_norm(DEEPNORM_ALPHA * x_ref[...] + ffn, g_ref[...], b_ref[...])


def _combine(y_tok, x1, route, ln_g, ln_b):
    n, d = x1.shape
    tm = min(n, TM_COMB)
    slab = d // LANES
    return pl.pallas_call(
        _combine_kernel,
        out_shape=jax.ShapeDtypeStruct((n, d), F32),
        grid=(n // tm,),
        in_specs=[pl.BlockSpec((tm * slab, LANES), lambda i: (i, 0)),
                  pl.BlockSpec((tm * slab, LANES), lambda i: (i + n // tm, 0)),
                  pl.BlockSpec((tm, d), lambda i: (i, 0)),
                  pl.BlockSpec((tm, LANES), lambda i: (i, 0)),
                  pl.BlockSpec((1, d), lambda i: (0, 0)),
                  pl.BlockSpec((1, d), lambda i: (0, 0))],
        out_specs=pl.BlockSpec((tm, d), lambda i: (i, 0)),
        compiler_params=_params("parallel"),
        name="moe_combine_ln",
    )(y_tok, y_tok, x1, route, ln_g[None, :], ln_b[None, :])


def _slots_kernel(dest_ref, src0_hbm, dst0_hbm, src_ref, dst_ref, sem):
    fills = [pltpu.make_async_copy(src0_hbm, src_ref, sem.at[0]), pltpu.make_async_copy(dst0_hbm, dst_ref, sem.at[1])]
    for f in fills:
        f.start()
    for f in fills:
        f.wait()
    nk = dest_ref.shape[0]
    n = nk // TOP_K
    lead = dst_ref.shape[0] - src_ref.shape[0]

    def place(j, carry):
        d = dest_ref[j]
        t = j // TOP_K
        src_ref[d] = t
        dst_ref[lead + d] = (j % TOP_K) * n + t
        return carry

    lax.fori_loop(0, nk, place, 0, unroll=8)


def _slot_tables(dest, src0, dst0):
    smem = pl.BlockSpec(memory_space=pltpu.SMEM)
    hbm = pl.BlockSpec(memory_space=pl.ANY)
    return pl.pallas_call(
        _slots_kernel,
        out_shape=(jax.ShapeDtypeStruct(src0.shape, jnp.int32), jax.ShapeDtypeStruct(dst0.shape, jnp.int32)),
        grid_spec=pltpu.PrefetchScalarGridSpec(
            num_scalar_prefetch=1, grid=(1,), in_specs=[hbm, hbm], out_specs=(smem, smem),
            scratch_shapes=[pltpu.SemaphoreType.DMA((2,))]),
        name="moe_slot_tables",
    )(dest, src0, dst0)


def _dispatch_plan(route, counts, rows):
    n = route.shape[0]
    nk = n * TOP_K
    nb = (nk + rows - 1) // rows + N_EXPERTS
    counts = counts[0, :N_EXPERTS].astype(jnp.int32)
    padded = ((counts + rows - 1) // rows) * rows
    pad_end = jnp.cumsum(padded)
    pad_start = pad_end - padded
    flat_e = route[:, 2:2 + TOP_K].astype(jnp.int32).reshape(nk)
    rank = route[:, 2 + TOP_K:2 + 2 * TOP_K].astype(jnp.int32).reshape(nk)
    dest = pad_start[flat_e] + rank
    slot = jnp.arange(nb * rows, dtype=jnp.int32)
    dump = nk + rows + ((slot // rows) % MOE_RING) * rows + slot % rows
    slot_src, slot_dst = _slot_tables(dest, jnp.zeros((nb * rows,), jnp.int32),
                                      jnp.concatenate([nk + jnp.arange(rows, dtype=jnp.int32), dump]))
    blk_exp = jnp.minimum(
        jnp.sum((pad_end[None, :] <= (jnp.arange(nb, dtype=jnp.int32) * rows)[:, None]).astype(jnp.int32), axis=1),
        N_EXPERTS - 1).astype(jnp.int32)
    used = (pad_end[-1] // rows).astype(jnp.int32).reshape(1)
    return slot_src, slot_dst, blk_exp, used, nk + (MOE_RING + 1) * rows


def _pad_q_heads(w_uq):
    lead = w_uq.shape[:-1]
    w = w_uq.reshape(*lead, MLA_HEADS, MLA_QK_DIM)
    w = jnp.pad(w, [(0, 0)] * len(lead) + [(0, 0), (0, Q_HEAD_PAD - MLA_QK_DIM)])
    return w.reshape(*lead, MLA_HEADS * Q_HEAD_PAD)


def kernel(x, positions, w_in, sgu_ln_g, sgu_ln_b, sgu_ws, sgu_b, hgrn_lb_logits, hgrn_norm_g, mla_qn_g, mla_w_uq, mla_kvn_g, mla_w_ukv, w_out, ln1_g, ln1_b, router_group_w, router_group_b, router_expert_w, router_expert_b, expert_w_gate, expert_w_up, expert_w_down, ln2_g, ln2_b):
    B, S, D = x.shape
    n = B * S
    depth = w_in.shape[0]
    tab_c, tab_s = _rope_tables(positions)
    lb_cum = jnp.cumsum(jax.nn.softmax(hgrn_lb_logits.astype(F32), axis=0), axis=0)
    lower_bounds = lb_cum - lb_cum[0:1]
    xt = x.reshape(n, D)
    w_in_b = jnp.pad(w_in, ((0, 0), (0, 0), (0, D_IN_PAD - D_IN))).astype(BF16)
    w_uq_b = _pad_q_heads(mla_w_uq).astype(BF16)
    w_ukv_b = mla_w_ukv.astype(BF16)
    w_out_b = w_out.astype(BF16)
    for l in range(depth):
        h = _inproj(xt, w_in_b, l)
        y_a = _sgu(h, sgu_ln_g[l], sgu_ln_b[l], sgu_ws[l], sgu_b[l])
        y_b = _hgrn(h, lower_bounds[l], hgrn_norm_g[l], B)
        q, kn, v, kr = _mla_up(h, tab_c, tab_s, mla_qn_g[l], w_uq_b, mla_kvn_g[l], w_ukv_b, l)
        y_c = _attention(q, kn, kr, v, B)
        wr = jnp.pad(jnp.concatenate([router_expert_w[l], router_group_w[l]], axis=1),
                     ((0, 0), (0, LANES - N_EXPERTS - N_GROUPS)))
        br = jnp.pad(jnp.concatenate([router_expert_b[l], router_group_b[l]]), (0, LANES - N_EXPERTS - N_GROUPS))
        x1, x1s, route, counts = _outproj(y_a, y_b, y_c, xt, w_out_b, ln1_g[l], ln1_b[l], wr, br[None, :], l)
        slot_src, slot_dst, blk_exp, used, n_out = _dispatch_plan(route, counts, MOE_ROWS)
        y_tok = _moe(slot_src, slot_dst, blk_exp, used, x1s, expert_w_gate, expert_w_up, expert_w_down, n_out, l)
        xt = _combine(y_tok, x1, route, ln2_g[l], ln2_b[l])
    return xt.reshape(B, S, D)
```

```python
import functools

import jax
import jax.numpy as jnp
from jax import lax
from jax.experimental import pallas as pl
from jax.experimental.pallas import tpu as pltpu

D_MODEL = 2048
DEPTH = 4
CHUNK = 128
SGU_WIDTH = D_MODEL // 4
SGU_GROUPS = 4
SGU_GROUP_DIM = SGU_WIDTH // SGU_GROUPS
HGRN_WIDTH = D_MODEL // 4
HGRN_HEADS = 4
HGRN_HEAD_DIM = HGRN_WIDTH // HGRN_HEADS
MLA_HEADS = 8
MLA_V_DIM = (D_MODEL - SGU_WIDTH - HGRN_WIDTH) // MLA_HEADS
MLA_NOPE_DIM = 128
MLA_ROPE_DIM = 64
MLA_QK_DIM = MLA_NOPE_DIM + MLA_ROPE_DIM
Q_LORA_RANK = D_MODEL // 4
KV_LORA_RANK = D_MODEL // 4
ROPE_THETA = 10000.0
N_GROUPS = 4
EXPERTS_PER_GROUP = 8
N_EXPERTS = N_GROUPS * EXPERTS_PER_GROUP
TOP_K = 2
D_EXPERT = D_MODEL // 4
NORM_EPS = 1e-5
DEEPNORM_ALPHA = (2 * DEPTH) ** 0.25
D_IN = SGU_WIDTH * 2 + HGRN_WIDTH * 4 + Q_LORA_RANK + KV_LORA_RANK + MLA_ROPE_DIM

LANES = 128
D_IN_PAD = ((D_IN + LANES - 1) // LANES) * LANES
Q_HEAD_PAD = 2 * LANES
VMEM_LIMIT = 56 * 1024 * 1024

TM_INPROJ = 512
TN_INPROJ = D_IN_PAD // 3
TM_SGU = 512
HGRN_CHUNKS_PER_STEP = 4
TM_MLA = 512
TQ_ATTN = 2048
TK_ATTN = 512
TM_OUT = 512
OUT_CHAIN_ROWS = 256
MOE_ROWS = 256
MOE_RING = 4
MOE_VMEM_STRIDE = 20
TM_COMB = 512

BF16 = jnp.bfloat16
F32 = jnp.float32
LOG2_E = 1.4426950408889634


def _params(*sem):
    return pltpu.CompilerParams(dimension_semantics=sem, vmem_limit_bytes=VMEM_LIMIT)


def _rope_kernel(pos_ref, inv_ref, sgn_ref, c_ref, s_ref):
    ang = pos_ref[...].astype(F32) * inv_ref[...]
    live = sgn_ref[...] != 0.0
    c_ref[...] = jnp.where(live, jnp.cos(ang), 0.0)
    s_ref[...] = jnp.sin(ang) * sgn_ref[...]


def _rope_tables(positions):
    n = positions.size
    tm = min(n, 1024)
    half = MLA_ROPE_DIM // 2
    inv = 1.0 / (ROPE_THETA ** (jnp.arange(0, MLA_ROPE_DIM, 2, dtype=F32) / MLA_ROPE_DIM))
    zeros = jnp.zeros((LANES - 2 * half,), F32)
    inv_t = jnp.concatenate([inv, inv, zeros])[None, :]
    sgn_t = jnp.concatenate([-jnp.ones((half,), F32), jnp.ones((half,), F32), zeros])[None, :]
    row = pl.BlockSpec((1, LANES), lambda i: (0, 0))
    return pl.pallas_call(
        _rope_kernel,
        out_shape=(jax.ShapeDtypeStruct((n, LANES), F32),) * 2,
        grid=(n // tm,),
        in_specs=[pl.BlockSpec((tm, 1), lambda i: (i, 0)), row, row],
        out_specs=(pl.BlockSpec((tm, LANES), lambda i: (i, 0)),) * 2,
        compiler_params=_params("parallel"),
        name="rope_tables",
    )(positions.reshape(n, 1), inv_t, sgn_t)


def _inproj_kernel(x_ref, w_ref, o_ref):
    o_ref[...] = jnp.dot(x_ref[...].astype(BF16), w_ref[0], preferred_element_type=F32)


def _inproj(x, w, l):
    n, d = x.shape
    tm = min(n, TM_INPROJ)
    return pl.pallas_call(
        _inproj_kernel,
        out_shape=jax.ShapeDtypeStruct((n, D_IN_PAD), F32),
        grid=(D_IN_PAD // TN_INPROJ, n // tm),
        in_specs=[pl.BlockSpec((tm, d), lambda j, i: (i, 0)),
                  pl.BlockSpec((1, d, TN_INPROJ), lambda j, i: (l, 0, j))],
        out_specs=pl.BlockSpec((tm, TN_INPROJ), lambda j, i: (i, j)),
        compiler_params=_params("parallel", "parallel"),
        name="in_proj",
    )(x, w)


def _sgu_kernel(u_ref, v_ref, g_ref, b_ref, ws_ref, bs_ref, o_ref):
    u = jax.nn.gelu(u_ref[...])
    v = jax.nn.gelu(v_ref[...])
    mu = jnp.mean(v, axis=-1, keepdims=True)
    vc = v - mu
    var = jnp.mean(vc * vc, axis=-1, keepdims=True)
    vn = (vc * lax.rsqrt(var + NORM_EPS) * g_ref[...] + b_ref[...]).astype(BF16)
    row = lax.broadcasted_iota(jnp.int32, (CHUNK, CHUNK), 0)
    col = lax.broadcasted_iota(jnp.int32, (CHUNK, CHUNK), 1)
    causal = row >= col
    bs = bs_ref[...]
    for g in range(SGU_GROUPS):
        w = jnp.where(causal, ws_ref[g], 0.0).astype(BF16)
        cs = slice(g * SGU_GROUP_DIM, (g + 1) * SGU_GROUP_DIM)
        for c in range(u.shape[0] // CHUNK):
            rs = slice(c * CHUNK, (c + 1) * CHUNK)
            mixed = jnp.dot(w, vn[rs, cs], preferred_element_type=F32) + bs[:, g:g + 1]
            o_ref[rs, cs] = (u[rs, cs] * mixed).astype(o_ref.dtype)


def _sgu(h, ln_g, ln_b, ws, bs):
    n = h.shape[0]
    tm = min(n, TM_SGU)
    full = lambda shape: pl.BlockSpec(shape, lambda i: (0,) * len(shape))
    return pl.pallas_call(
        _sgu_kernel,
        out_shape=jax.ShapeDtypeStruct((n, SGU_WIDTH), BF16),
        grid=(n // tm,),
        in_specs=[pl.BlockSpec((tm, SGU_WIDTH), lambda i: (i, 0)),
                  pl.BlockSpec((tm, SGU_WIDTH), lambda i: (i, 1)),
                  full((1, SGU_WIDTH)), full((1, SGU_WIDTH)),
                  full((SGU_GROUPS, CHUNK, CHUNK)), full((CHUNK, SGU_GROUPS))],
        out_specs=pl.BlockSpec((tm, SGU_WIDTH), lambda i: (i, 0)),
        compiler_params=_params("parallel"),
        name="sgu",
    )(h, h, ln_g[None, :], ln_b[None, :], ws, bs.T)


def _dot_nt(a, b):
    return lax.dot_general(a, b, (((1,), (1,)), ((), ())), preferred_element_type=F32)


def _hgrn_kernel(q_ref, f_ref, i_ref, g_ref, la_ref, lc_ref, om_ref, ng_ref, o_ref, state_ref):
    @pl.when(pl.program_id(1) == 0)
    def _():
        state_ref[...] = jnp.zeros_like(state_ref)

    T, K = CHUNK, HGRN_HEAD_DIM
    row = lax.broadcasted_iota(jnp.int32, (T, T), 0)
    col = lax.broadcasted_iota(jnp.int32, (T, T), 1)
    tril = (row >= col).astype(F32)
    SUB = 8
    wide = {}
    for c, h in [(c, h) for c in range(q_ref.shape[0] // T) for h in range(HGRN_HEADS)]:
        rs = slice(c * T, (c + 1) * T)
        cs = slice(h * K, (h + 1) * K)
        if c not in wide:
            fx = f_ref[rs, :]
            log_f = jnp.logaddexp(la_ref[...], lc_ref[...] + jax.nn.log_sigmoid(fx))
            G_all = jnp.dot(tril, log_f, preferred_element_type=F32, precision=lax.Precision.HIGHEST)
            wide = {c: (G_all, om_ref[...] * jax.nn.sigmoid(-fx), jax.nn.silu(q_ref[rs, :]))}
        G, k, q = (a[:, cs] for a in wide[c])
        v = i_ref[rs, cs]
        A = jnp.zeros((T, T), F32)
        m = T // 2
        while m >= SUB:
            blk = 2 * m
            g_mid = jnp.concatenate(
                [jnp.broadcast_to(G[b * blk + m - 1:b * blk + m, :], (blk, K)) for b in range(T // blk)], axis=0)
            lower = (row % blk) >= m
            qs = jnp.where(lower, q * jnp.exp(G - g_mid), 0.0).astype(BF16)
            ks = jnp.where(lower, 0.0, k * jnp.exp(g_mid - G)).astype(BF16)
            same = (row // blk) == (col // blk)
            A = A + jnp.where(same, _dot_nt(qs, ks), 0.0)
            m //= 2
        rmod = row % SUB
        k3, G3 = k.reshape(T // SUB, SUB, K), G.reshape(T // SUB, SUB, K)
        for d in range(SUB):
            if d == 0:
                prod = q * k
            else:
                kd = pltpu.roll(k3, d, axis=1).reshape(T, K)
                gd = pltpu.roll(G3, d, axis=1).reshape(T, K)
                prod = jnp.where(rmod >= d, q * kd * jnp.exp(G - gd), 0.0)
            A = A + jnp.where(col == row - d, jnp.sum(prod, axis=-1, keepdims=True), 0.0)

        st = state_ref[h]
        o = jnp.dot(A.astype(BF16), v.astype(BF16), preferred_element_type=F32)
        o = o + _dot_nt((q * jnp.exp(G)).astype(BF16), st.astype(BF16))
        g_last = G[T - 1:T, :]
        k_dec = (k * jnp.exp(g_last - G)).astype(BF16)
        state_ref[h] = jnp.exp(g_last) * st + jnp.dot(v.T.astype(BF16), k_dec, preferred_element_type=F32)

        ms = jnp.mean(o * o, axis=-1, keepdims=True)
        o = o * lax.rsqrt(ms + NORM_EPS) * ng_ref[:, cs]
        o_ref[rs, cs] = (o * jax.nn.silu(g_ref[rs, cs])).astype(o_ref.dtype)


def _hgrn(h, lb, norm_g, batch):
    n = h.shape[0]
    tb = CHUNK * HGRN_CHUNKS_PER_STEP
    nc = n // batch // tb
    W = HGRN_WIDTH
    col = lambda j: pl.BlockSpec((tb, W), lambda b, c: (b * nc + c, j))
    vec = pl.BlockSpec((1, W), lambda b, c: (0, 0))
    base = 2 * SGU_WIDTH // W
    return pl.pallas_call(
        _hgrn_kernel,
        out_shape=jax.ShapeDtypeStruct((n, W), BF16),
        grid=(batch, nc),
        in_specs=[col(base), col(base + 1), col(base + 2), col(base + 3), vec, vec, vec, vec],
        out_specs=pl.BlockSpec((tb, W), lambda b, c: (b * nc + c, 0)),
        scratch_shapes=[pltpu.VMEM((HGRN_HEADS, HGRN_HEAD_DIM, HGRN_HEAD_DIM), F32)],
        compiler_params=_params("parallel", "arbitrary"),
        name="hgrn2",
    )(h, h, h, h, jnp.log(lb)[None, :], jnp.log1p(-lb)[None, :], (1.0 - lb)[None, :], norm_g[None, :])


def _rms(x, g):
    ms = jnp.mean(x * x, axis=-1, keepdims=True)
    return x * lax.rsqrt(ms + NORM_EPS) * g


def _rope(x, c, s):
    half = MLA_ROPE_DIM // 2
    lane = lax.broadcasted_iota(jnp.int32, x.shape, 1)
    swapped = jnp.where(lane < half, pltpu.roll(x, LANES - half, axis=1), pltpu.roll(x, half, axis=1))
    return x * c + swapped * s


def _mla_up_kernel(cq_ref, ckv_ref, kr_ref, c_ref, s_ref, qg_ref, kvg_ref, wq_ref, wkv_ref,
                   q_ref, kn_ref, v_ref, kro_ref):
    c, s = c_ref[...], s_ref[...]
    scale = MLA_QK_DIM ** -0.5 * LOG2_E
    q = jnp.dot(_rms(cq_ref[...], qg_ref[...]).astype(BF16), wq_ref[0], preferred_element_type=F32)
    kv = jnp.dot(_rms(ckv_ref[...], kvg_ref[...]).astype(BF16), wkv_ref[0], preferred_element_type=F32)
    for h in range(MLA_HEADS):
        base = h * Q_HEAD_PAD
        q_ref[h, :, :LANES] = (q[:, base:base + LANES] * scale).astype(BF16)
        q_ref[h, :, LANES:] = (_rope(q[:, base + LANES:base + 2 * LANES], c, s) * scale).astype(BF16)
        kbase = h * (MLA_NOPE_DIM + MLA_V_DIM)
        kn_ref[h] = kv[:, kbase:kbase + MLA_NOPE_DIM].astype(BF16)
        v_ref[h] = kv[:, kbase + MLA_NOPE_DIM:kbase + MLA_NOPE_DIM + MLA_V_DIM].astype(BF16)
    kro_ref[...] = _rope(kr_ref[...], c, s).astype(BF16)


def _mla_up(h, tab_c, tab_s, qn_g, w_uq, kvn_g, w_ukv, l):
    n = h.shape[0]
    tm = min(n, TM_MLA)
    H = MLA_HEADS
    cq_blk = (2 * SGU_WIDTH + 4 * HGRN_WIDTH) // Q_LORA_RANK
    kr_blk = (2 * SGU_WIDTH + 4 * HGRN_WIDTH + Q_LORA_RANK + KV_LORA_RANK) // LANES
    full = lambda shape: pl.BlockSpec(shape, lambda i: (0,) * len(shape))
    heads = lambda w: pl.BlockSpec((H, tm, w), lambda i: (0, i, 0))
    return pl.pallas_call(
        _mla_up_kernel,
        out_shape=(jax.ShapeDtypeStruct((H, n, Q_HEAD_PAD), BF16),
                   jax.ShapeDtypeStruct((H, n, MLA_NOPE_DIM), BF16),
                   jax.ShapeDtypeStruct((H, n, MLA_V_DIM), BF16),
                   jax.ShapeDtypeStruct((n, LANES), BF16)),
        grid=(n // tm,),
        in_specs=[pl.BlockSpec((tm, Q_LORA_RANK), lambda i: (i, cq_blk)),
                  pl.BlockSpec((tm, KV_LORA_RANK), lambda i: (i, cq_blk + 1)),
                  pl.BlockSpec((tm, LANES), lambda i: (i, kr_blk)),
                  pl.BlockSpec((tm, LANES), lambda i: (i, 0)),
                  pl.BlockSpec((tm, LANES), lambda i: (i, 0)),
                  full((1, Q_LORA_RANK)), full((1, KV_LORA_RANK)),
                  pl.BlockSpec((1, Q_LORA_RANK, H * Q_HEAD_PAD), lambda i: (l, 0, 0)),
                  pl.BlockSpec((1, KV_LORA_RANK, H * (MLA_NOPE_DIM + MLA_V_DIM)), lambda i: (l, 0, 0))],
        out_specs=(heads(Q_HEAD_PAD), heads(MLA_NOPE_DIM), heads(MLA_V_DIM),
                   pl.BlockSpec((tm, LANES), lambda i: (i, 0))),
        compiler_params=_params("parallel"),
        name="mla_up",
    )(h, h, h, tab_c, tab_s, qn_g[None, :], kvn_g[None, :], w_uq, w_ukv)


def _attn_kernel(q_ref, kn_ref, kr_ref, v_ref, o_ref, m_ref, l_ref, acc_ref):
    i = pl.program_id(2)
    nch, tk, _ = acc_ref.shape
    m_ref[...] = jnp.full_like(m_ref, -jnp.inf)
    l_ref[...] = jnp.zeros_like(l_ref)
    acc_ref[...] = jnp.zeros_like(acc_ref)

    def chain(c, j, masked):
        start = pl.multiple_of(j * tk, tk)
        q = q_ref[0, c * tk:(c + 1) * tk, :]
        k = jnp.concatenate([kn_ref[0, pl.ds(start, tk), :], kr_ref[pl.ds(start, tk), :]], axis=-1)
        s = _dot_nt(q, k)
        if masked:
            row = lax.broadcasted_iota(jnp.int32, s.shape, 0)
            col = lax.broadcasted_iota(jnp.int32, s.shape, 1)
            s = jnp.where(col <= row, s, -jnp.inf)
        chunks = [s[:, t * LANES:(t + 1) * LANES] for t in range(tk // LANES)]
        smax = functools.reduce(jnp.maximum, chunks)
        m_prev = m_ref[c]
        m_new = jnp.maximum(m_prev, jnp.max(smax, axis=-1, keepdims=True))
        alpha = jnp.exp2(m_prev - m_new)
        ps = [jnp.exp2(ch - m_new) for ch in chunks]
        l_ref[c] = alpha * l_ref[c] + functools.reduce(jnp.add, ps)
        p = jnp.concatenate([x.astype(BF16) for x in ps], axis=-1)
        acc_ref[c] = alpha * acc_ref[c] + jnp.dot(p, v_ref[0, pl.ds(start, tk), :], preferred_element_type=F32)
        m_ref[c] = m_new

    def body(j, carry):
        for c in range(nch):
            chain(c, j, masked=False)
        return carry

    lax.fori_loop(0, nch * i, body, 0)
    for t in range(nch):
        for c in range(t, nch):
            chain(c, nch * i + t, masked=(c == t))
    for c in range(nch):
        l = jnp.sum(l_ref[c], axis=-1, keepdims=True)
        o_ref[c * tk:(c + 1) * tk, :] = (acc_ref[c] / l).astype(o_ref.dtype)


def _attention(q, kn, kr, v, batch):
    H, n, _ = q.shape
    S = n // batch
    tq = min(S, TQ_ATTN)
    tk = min(tq, TK_ATTN)
    nq = S // tq
    return pl.pallas_call(
        _attn_kernel,
        out_shape=jax.ShapeDtypeStruct((n, H * MLA_V_DIM), BF16),
        grid=(batch, H, nq),
        in_specs=[pl.BlockSpec((1, tq, Q_HEAD_PAD), lambda b, h, i: (h, b * nq + i, 0)),
                  pl.BlockSpec((1, S, MLA_NOPE_DIM), lambda b, h, i: (h, b, 0)),
                  pl.BlockSpec((S, LANES), lambda b, h, i: (b, 0)),
                  pl.BlockSpec((1, S, MLA_V_DIM), lambda b, h, i: (h, b, 0))],
        out_specs=pl.BlockSpec((tq, MLA_V_DIM), lambda b, h, i: (b * nq + i, h)),
        scratch_shapes=[pltpu.VMEM((tq // tk, tk, LANES), F32), pltpu.VMEM((tq // tk, tk, LANES), F32),
                        pltpu.VMEM((tq // tk, tk, MLA_V_DIM), F32)],
        compiler_params=_params("parallel", "parallel", "arbitrary"),
        name="mla_attention",
    )(q, kn, kr, v)


def _layer_norm(x, g, b):
    mu = jnp.mean(x, axis=-1, keepdims=True)
    xc = x - mu
    var = jnp.mean(xc * xc, axis=-1, keepdims=True)
    return xc * lax.rsqrt(var + NORM_EPS) * g + b


def _first_index(hit, lane):
    return jnp.min(jnp.where(hit, lane, LANES), axis=-1, keepdims=True)


def _route(logits):
    lane = lax.broadcasted_iota(jnp.int32, logits.shape, 1)
    neg = -jnp.inf
    gl = jnp.where((lane >= N_EXPERTS) & (lane < N_EXPERTS + N_GROUPS), logits, neg)
    ge = jnp.exp(gl - jnp.max(gl, axis=-1, keepdims=True))
    gp = ge / jnp.sum(ge, axis=-1, keepdims=True)
    g_val = jnp.max(gp, axis=-1, keepdims=True)
    g_idx = _first_index(gp == g_val, lane) - N_EXPERTS
    el = jnp.where((lane // EXPERTS_PER_GROUP == g_idx) & (lane < N_EXPERTS), logits, neg)
    m1 = jnp.max(el, axis=-1, keepdims=True)
    i1 = _first_index(el == m1, lane)
    el2 = jnp.where(lane == i1, neg, el)
    m2 = jnp.max(el2, axis=-1, keepdims=True)
    i2 = _first_index(el2 == m2, lane)
    e2 = jnp.exp(m2 - m1)
    den = 1.0 + e2
    w1 = g_val * (1.0 / den)
    w2 = g_val * (e2 / den)
    return w1, w2, i1, i2


def _rows_to_slabs(ref, x, stride):
    m, d = x.shape
    for s in range(d // LANES):
        ref[pl.ds(s, m, stride=stride), :] = x[:, s * LANES:(s + 1) * LANES]


def _slabs_to_rows(ref, m, d, stride):
    return jnp.concatenate([ref[pl.ds(s, m, stride=stride), :] for s in range(d // LANES)], axis=-1)


def _outproj_kernel(ya_ref, yb_ref, yc_ref, x_ref, w_ref, g_ref, b_ref, wr_ref, br_ref,
                    x1_ref, x1s_ref, rt_ref, cnt_ref):
    @pl.when(pl.program_id(0) == 0)
    def _():
        cnt_ref[...] = jnp.zeros_like(cnt_ref)

    wa = SGU_WIDTH
    wb = SGU_WIDTH + HGRN_WIDTH
    d = x_ref.shape[1]
    slab = d // LANES
    ch = OUT_CHAIN_ROWS
    lane = lax.broadcasted_iota(jnp.int32, (ch, LANES), 1)
    row = lax.broadcasted_iota(jnp.int32, (ch, ch), 0)
    col = lax.broadcasted_iota(jnp.int32, (ch, ch), 1)
    before = (col < row).astype(BF16)
    carry = cnt_ref[...]
    for c in range(x_ref.shape[0] // ch):
        rs = slice(c * ch, (c + 1) * ch)
        mix = jnp.dot(ya_ref[rs, :], w_ref[0, :wa, :], preferred_element_type=F32)
        mix = mix + jnp.dot(yb_ref[rs, :], w_ref[0, wa:wb, :], preferred_element_type=F32)
        mix = mix + jnp.dot(yc_ref[rs, :], w_ref[0, wb:, :], preferred_element_type=F32)
        x1 = _layer_norm(DEEPNORM_ALPHA * x_ref[rs, :] + mix, g_ref[...], b_ref[...])
        x1_ref[rs, :] = x1
        _rows_to_slabs(x1s_ref.at[pl.ds(c * ch * slab, ch * slab), :], x1, slab)
        x_hi = x1.astype(BF16)
        x_lo = (x1 - x_hi.astype(F32)).astype(BF16)
        r = jnp.dot(x_hi, wr_ref[...], preferred_element_type=F32)
        logits = (r[:, :LANES] + r[:, LANES:] + jnp.dot(x_lo, wr_ref[:, :LANES], preferred_element_type=F32)
                  + br_ref[...])
        w1, w2, i1, i2 = _route(logits)

        hit1, hit2 = lane == i1, lane == i2
        oh1, oh2 = hit1.astype(F32), hit2.astype(F32)
        pre = jnp.dot(before, jnp.concatenate([oh1, oh2], axis=-1).astype(BF16), preferred_element_type=F32)
        tot1 = jnp.sum(oh1, axis=0, keepdims=True)
        tot2 = jnp.sum(oh2, axis=0, keepdims=True)
        rank1 = jnp.sum(jnp.where(hit1, pre[:, :LANES] + carry, 0.0), axis=-1, keepdims=True)
        rank2 = jnp.sum(jnp.where(hit2, pre[:, LANES:] + (carry + tot1), 0.0), axis=-1, keepdims=True)
        carry = carry + tot1 + tot2

        out = jnp.where(lane == 0, w1, 0.0)
        out = jnp.where(lane == 1, w2, out)
        out = jnp.where(lane == 2, i1.astype(F32), out)
        out = jnp.where(lane == 3, i2.astype(F32), out)
        out = jnp.where(lane == 4, rank1, out)
        out = jnp.where(lane == 5, rank2, out)
        rt_ref[rs, :] = out
    cnt_ref[...] = carry


def _outproj(ya, yb, yc, x, w_out, ln_g, ln_b, wr, br, l):
    n, d = x.shape
    tm = min(n, TM_OUT)
    assert tm % OUT_CHAIN_ROWS == 0
    slab = d // LANES
    full = lambda shape: pl.BlockSpec(shape, lambda i: (0,) * len(shape))
    rows = lambda w: pl.BlockSpec((tm, w), lambda i: (i, 0))
    wr_hi = wr.astype(BF16)
    wr_lo = (wr - wr_hi.astype(F32)).astype(BF16)
    return pl.pallas_call(
        _outproj_kernel,
        out_shape=(jax.ShapeDtypeStruct((n, d), F32), jax.ShapeDtypeStruct((n * slab, LANES), F32),
                   jax.ShapeDtypeStruct((n, LANES), F32), jax.ShapeDtypeStruct((1, LANES), F32)),
        grid=(n // tm,),
        in_specs=[rows(ya.shape[1]), rows(yb.shape[1]), rows(yc.shape[1]), rows(d),
                  pl.BlockSpec((1, d, d), lambda i: (l, 0, 0), pipeline_mode=pl.Buffered(1)), full((1, d)), full((1, d)),
                  full((d, 2 * LANES)), full((1, LANES))],
        out_specs=(rows(d), pl.BlockSpec((tm * slab, LANES), lambda i: (i, 0)), rows(LANES), full((1, LANES))),
        compiler_params=_params("arbitrary"),
        name="out_proj_ln_router",
    )(ya, yb, yc, x, w_out, ln_g[None, :], ln_b[None, :], jnp.concatenate([wr_hi, wr_lo], axis=1), br)


def _moe_kernel(src_ref, dst_ref, exp_ref, used_ref, x_hbm, wg_ref, wu_ref, wd_ref, y_hbm,
                xbuf0, xbuf1, xbuf2, xbuf3, ybuf0, ybuf1, ybuf2, ybuf3, wgb, wub, wdb, gsem, ssem):
    xbufs, ybufs = (xbuf0, xbuf1, xbuf2, xbuf3), (ybuf0, ybuf1, ybuf2, ybuf3)
    nx = len(xbufs)
    i = pl.program_id(0)
    nb = pl.num_programs(0) - 1
    d = wgb.shape[0]
    slab = d // LANES
    vs = MOE_VMEM_STRIDE
    rows = xbuf0.shape[0] // vs
    used = used_ref[0]
    n_real = y_hbm.shape[0] // slab - (nx + 1) * rows

    def gather_copy(tok, r, slot):
        src = x_hbm.at[pl.ds(pl.multiple_of(tok * slab, slab), slab), :]
        return pltpu.make_async_copy(src, xbufs[slot].at[pl.ds(r * vs, slab), :], gsem.at[slot])

    def scatter_copy(dst, r, slot):
        out = y_hbm.at[pl.ds(pl.multiple_of(dst * slab, slab), slab), :]
        return pltpu.make_async_copy(ybufs[slot].at[pl.ds(r * vs, slab), :], out, ssem.at[slot])

    def fill_copy(ybuf, region, sem):
        return pltpu.make_async_copy(ybuf.at[pl.ds(0, rows * slab), :],
                                     y_hbm.at[pl.ds((n_real + region * rows) * slab, rows * slab), :], sem)

    def wait_gather(slot):
        pltpu.make_async_copy(x_hbm.at[pl.ds(0, rows * slab), :], xbuf0.at[pl.ds(0, rows * slab), :],
                              gsem.at[slot]).wait()

    def wait_scatter(slot):
        fill_copy(ybuf0, 0, ssem.at[slot]).wait()

    @pl.when(i == 0)
    def _():
        ybuf3[...] = jnp.zeros_like(ybuf3)
        for t in range(nx - 1):
            fill_copy(ybuf3, 1 + t, ssem.at[t]).start()
        last = fill_copy(ybuf3, nx, ssem.at[nx])
        last.start()
        last.wait()
        for blk in range(2):
            def issue(r, carry):
                gather_copy(src_ref[jnp.minimum(blk, nb - 1) * rows + r], r, blk).start()
                return carry
            lax.fori_loop(0, rows, issue, 0, unroll=8)

    cur = jnp.minimum(i, nb - 1)
    @pl.when((i < used) & ((i == 0) | (exp_ref[cur] != exp_ref[jnp.maximum(cur - 1, 0)])))
    def _():
        wgb[...] = wg_ref[0, 0].astype(BF16)
        wub[...] = wu_ref[0, 0].astype(BF16)
        wdb[...] = wd_ref[0, 0].astype(BF16)

    def compute_block(phase):
        xs = ys = phase
        wait_gather(xs)
        wait_scatter(ys)
        nxt = jnp.minimum(i + 2, nb - 1) * rows
        prv = i * rows
        for r in range(rows):
            gather_copy(src_ref[nxt + r], r, (xs + 2) % nx).start()
            scatter_copy(dst_ref[prv + r], r, (ys - 1) % nx).start()
        xb = _slabs_to_rows(xbufs[xs], rows, d, vs).astype(BF16)
        hg = jnp.dot(xb, wgb[...], preferred_element_type=F32)
        hu = jnp.dot(xb, wub[...], preferred_element_type=F32)
        hh = (jax.nn.silu(hg) * hu).astype(BF16)
        _rows_to_slabs(ybufs[ys], jnp.dot(hh, wdb[...], preferred_element_type=F32), vs)

    def drain_block(ys):
        def issue(r, carry):
            scatter_copy(dst_ref[i * rows + r], r, ys).start()
            return carry
        lax.fori_loop(0, rows, issue, 0, unroll=8)
        wait_scatter(ys)

    for phase in range(nx):
        pl.when((i < used) & (i % nx == phase))(functools.partial(compute_block, phase))

    @pl.when(i == used)
    def _():
        wait_gather(i % nx)
        wait_gather((i + 1) % nx)
        for t in range(nx - 1):
            wait_scatter((i + t) % nx)

    for phase in range(nx):
        pl.when((i == used) & (i >= 1) & ((i - 1) % nx == phase))(functools.partial(drain_block, phase))


def _moe(slot_src, slot_dst, blk_exp, used, x1s, w_gate, w_up, w_down, n_out, l):
    d, de = w_gate.shape[2:]
    slab = d // LANES
    nb = blk_exp.shape[0]
    rows = slot_src.shape[0] // nb
    blk = lambda i, src, dst, exp, used: (l, exp[jnp.minimum(i, nb - 1)], 0, 0)
    return pl.pallas_call(
        _moe_kernel,
        out_shape=jax.ShapeDtypeStruct((n_out * slab, LANES), F32),
        grid_spec=pltpu.PrefetchScalarGridSpec(
            num_scalar_prefetch=4,
            grid=(nb + 1,),
            in_specs=[pl.BlockSpec(memory_space=pl.ANY),
                      pl.BlockSpec((1, 1, d, de), blk), pl.BlockSpec((1, 1, d, de), blk),
                      pl.BlockSpec((1, 1, de, d), blk)],
            out_specs=pl.BlockSpec(memory_space=pl.ANY),
            scratch_shapes=[pltpu.VMEM((rows * MOE_VMEM_STRIDE, LANES), F32)] * (2 * MOE_RING) + [
                            pltpu.VMEM((d, de), BF16), pltpu.VMEM((d, de), BF16), pltpu.VMEM((de, d), BF16),
                            pltpu.SemaphoreType.DMA((MOE_RING,)), pltpu.SemaphoreType.DMA((MOE_RING + 1,))]),
        compiler_params=_params("arbitrary"),
        name="moe_experts",
    )(slot_src, slot_dst, blk_exp, used, x1s, w_gate, w_up, w_down)


def _combine_kernel(y0_ref, y1_ref, x_ref, rt_ref, g_ref, b_ref, o_ref):
    tm, d = x_ref.shape
    slab = d // LANES
    rt = rt_ref[...]
    ffn = _slabs_to_rows(y0_ref, tm, d, slab) * rt[:, 0:1] + _slabs_to_rows(y1_ref, tm, d, slab) * rt[:, 1:2]
    o_ref[...] = _layer_norm(DEEPNORM_ALPHA * x_ref[...] + ffn, g_ref[...], b_ref[...])


def _combine(y_tok, x1, route, ln_g, ln_b):
    n, d = x1.shape
    tm = min(n, TM_COMB)
    slab = d // LANES
    return pl.pallas_call(
        _combine_kernel,
        out_shape=jax.ShapeDtypeStruct((n, d), F32),
        grid=(n // tm,),
        in_specs=[pl.BlockSpec((tm * slab, LANES), lambda i: (i, 0)),
                  pl.BlockSpec((tm * slab, LANES), lambda i: (i + n // tm, 0)),
                  pl.BlockSpec((tm, d), lambda i: (i, 0)),
                  pl.BlockSpec((tm, LANES), lambda i: (i, 0)),
                  pl.BlockSpec((1, d), lambda i: (0, 0)),
                  pl.BlockSpec((1, d), lambda i: (0, 0))],
        out_specs=pl.BlockSpec((tm, d), lambda i: (i, 0)),
        compiler_params=_params("parallel"),
        name="moe_combine_ln",
    )(y_tok, y_tok, x1, route, ln_g[None, :], ln_b[None, :])


def _slots_kernel(dest_ref, src0_hbm, dst0_hbm, src_ref, dst_ref, sem):
    fills = [pltpu.make_async_copy(src0_hbm, src_ref, sem.at[0]), pltpu.make_async_copy(dst0_hbm, dst_ref, sem.at[1])]
    for f in fills:
        f.start()
    for f in fills:
        f.wait()
    nk = dest_ref.shape[0]
    n = nk // TOP_K
    lead = dst_ref.shape[0] - src_ref.shape[0]

    batch = 16

    def place(b, carry):
        t0 = b * (batch // TOP_K)
        slots = [dest_ref[b * batch + u] for u in range(batch)]
        for u, d in enumerate(slots):
            t = t0 + u // TOP_K
            src_ref[d] = t
            dst_ref[lead + d] = (u % TOP_K) * n + t
        return carry

    lax.fori_loop(0, nk // batch, place, 0)


def _slot_tables(dest, src0, dst0):
    smem = pl.BlockSpec(memory_space=pltpu.SMEM)
    hbm = pl.BlockSpec(memory_space=pl.ANY)
    return pl.pallas_call(
        _slots_kernel,
        out_shape=(jax.ShapeDtypeStruct(src0.shape, jnp.int32), jax.ShapeDtypeStruct(dst0.shape, jnp.int32)),
        grid_spec=pltpu.PrefetchScalarGridSpec(
            num_scalar_prefetch=1, grid=(1,), in_specs=[hbm, hbm], out_specs=(smem, smem),
            scratch_shapes=[pltpu.SemaphoreType.DMA((2,))]),
        name="moe_slot_tables",
    )(dest, src0, dst0)


def _dispatch_plan(route, counts, rows):
    n = route.shape[0]
    nk = n * TOP_K
    nb = (nk + rows - 1) // rows + N_EXPERTS
    counts = counts[0, :N_EXPERTS].astype(jnp.int32)
    padded = ((counts + rows - 1) // rows) * rows
    pad_end = jnp.cumsum(padded)
    pad_start = pad_end - padded
    flat_e = route[:, 2:2 + TOP_K].astype(jnp.int32).reshape(nk)
    rank = route[:, 2 + TOP_K:2 + 2 * TOP_K].astype(jnp.int32).reshape(nk)
    dest = pad_start[flat_e] + rank
    slot = jnp.arange(nb * rows, dtype=jnp.int32)
    dump = nk + rows + ((slot // rows) % MOE_RING) * rows + slot % rows
    slot_src, slot_dst = _slot_tables(dest, jnp.zeros((nb * rows,), jnp.int32),
                                      jnp.concatenate([nk + jnp.arange(rows, dtype=jnp.int32), dump]))
    blk_exp = jnp.minimum(
        jnp.sum((pad_end[None, :] <= (jnp.arange(nb, dtype=jnp.int32) * rows)[:, None]).astype(jnp.int32), axis=1),
        N_EXPERTS - 1).astype(jnp.int32)
    used = (pad_end[-1] // rows).astype(jnp.int32).reshape(1)
    return slot_src, slot_dst, blk_exp, used, nk + (MOE_RING + 1) * rows


def _pad_q_heads(w_uq):
    lead = w_uq.shape[:-1]
    w = w_uq.reshape(*lead, MLA_HEADS, MLA_QK_DIM)
    w = jnp.pad(w, [(0, 0)] * len(lead) + [(0, 0), (0, Q_HEAD_PAD - MLA_QK_DIM)])
    return w.reshape(*lead, MLA_HEADS * Q_HEAD_PAD)


def kernel(x, positions, w_in, sgu_ln_g, sgu_ln_b, sgu_ws, sgu_b, hgrn_lb_logits, hgrn_norm_g, mla_qn_g, mla_w_uq, mla_kvn_g, mla_w_ukv, w_out, ln1_g, ln1_b, router_group_w, router_group_b, router_expert_w, router_expert_b, expert_w_gate, expert_w_up, expert_w_down, ln2_g, ln2_b):
    B, S, D = x.shape
    n = B * S
    depth = w_in.shape[0]
    tab_c, tab_s = _rope_tables(positions)
    lb_cum = jnp.cumsum(jax.nn.softmax(hgrn_lb_logits.astype(F32), axis=0), axis=0)
    lower_bounds = lb_cum - lb_cum[0:1]
    xt = x.reshape(n, D)
    w_in_b = jnp.pad(w_in, ((0, 0), (0, 0), (0, D_IN_PAD - D_IN))).astype(BF16)
    w_uq_b = _pad_q_heads(mla_w_uq).astype(BF16)
    w_ukv_b = mla_w_ukv.astype(BF16)
    w_out_b = w_out.astype(BF16)
    for l in range(depth):
        h = _inproj(xt, w_in_b, l)
        y_a = _sgu(h, sgu_ln_g[l], sgu_ln_b[l], sgu_ws[l], sgu_b[l])
        y_b = _hgrn(h, lower_bounds[l], hgrn_norm_g[l], B)
        q, kn, v, kr = _mla_up(h, tab_c, tab_s, mla_qn_g[l], w_uq_b, mla_kvn_g[l], w_ukv_b, l)
        y_c = _attention(q, kn, kr, v, B)
        wr = jnp.pad(jnp.concatenate([router_expert_w[l], router_group_w[l]], axis=1),
                     ((0, 0), (0, LANES - N_EXPERTS - N_GROUPS)))
        br = jnp.pad(jnp.concatenate([router_expert_b[l], router_group_b[l]]), (0, LANES - N_EXPERTS - N_GROUPS))
        x1, x1s, route, counts = _outproj(y_a, y_b, y_c, xt, w_out_b, ln1_g[l], ln1_b[l], wr, br[None, :], l)
        slot_src, slot_dst, blk_exp, used, n_out = _dispatch_plan(route, counts, MOE_ROWS)
        y_tok = _moe(slot_src, slot_dst, blk_exp, used, x1s, expert_w_gate, expert_w_up, expert_w_down, n_out, l)
        xt = _combine(y_tok, x1, route, ln2_g[l], ln2_b[l])
    return xt.reshape(B, S, D)
```

```python
import functools

import jax
import jax.numpy as jnp
from jax import lax
from jax.experimental import pallas as pl
from jax.experimental.pallas import tpu as pltpu

D_MODEL = 2048
DEPTH = 4
CHUNK = 128
SGU_WIDTH = D_MODEL // 4
SGU_GROUPS = 4
SGU_GROUP_DIM = SGU_WIDTH // SGU_GROUPS
HGRN_WIDTH = D_MODEL // 4
HGRN_HEADS = 4
HGRN_HEAD_DIM = HGRN_WIDTH // HGRN_HEADS
MLA_HEADS = 8
MLA_V_DIM = (D_MODEL - SGU_WIDTH - HGRN_WIDTH) // MLA_HEADS
MLA_NOPE_DIM = 128
MLA_ROPE_DIM = 64
MLA_QK_DIM = MLA_NOPE_DIM + MLA_ROPE_DIM
Q_LORA_RANK = D_MODEL // 4
KV_LORA_RANK = D_MODEL // 4
ROPE_THETA = 10000.0
N_GROUPS = 4
EXPERTS_PER_GROUP = 8
N_EXPERTS = N_GROUPS * EXPERTS_PER_GROUP
TOP_K = 2
D_EXPERT = D_MODEL // 4
NORM_EPS = 1e-5
DEEPNORM_ALPHA = (2 * DEPTH) ** 0.25
D_IN = SGU_WIDTH * 2 + HGRN_WIDTH * 4 + Q_LORA_RANK + KV_LORA_RANK + MLA_ROPE_DIM

LANES = 128
D_IN_PAD = ((D_IN + LANES - 1) // LANES) * LANES
Q_HEAD_PAD = 2 * LANES
VMEM_LIMIT = 56 * 1024 * 1024

TM_INPROJ = 1024
TN_INPROJ = D_IN_PAD // 3
TM_SGU = 1024
HGRN_CHUNKS_PER_STEP = 4
TM_MLA = 512
TQ_ATTN = 2048
TK_ATTN = 1024
TM_OUT = 512
OUT_CHAIN_ROWS = 256
MOE_ROWS = 256
MOE_RING = 4
MOE_VMEM_STRIDE = 20
TM_COMB = 512

BF16 = jnp.bfloat16
F32 = jnp.float32
LOG2_E = 1.4426950408889634


def _params(*sem):
    return pltpu.CompilerParams(dimension_semantics=sem, vmem_limit_bytes=VMEM_LIMIT)


def _rope_kernel(pos_ref, inv_ref, sgn_ref, c_ref, s_ref):
    ang = pos_ref[...].astype(F32) * inv_ref[...]
    live = sgn_ref[...] != 0.0
    c_ref[...] = jnp.where(live, jnp.cos(ang), 0.0)
    s_ref[...] = jnp.sin(ang) * sgn_ref[...]


def _rope_tables(positions):
    n = positions.size
    tm = min(n, 1024)
    half = MLA_ROPE_DIM // 2
    inv = 1.0 / (ROPE_THETA ** (jnp.arange(0, MLA_ROPE_DIM, 2, dtype=F32) / MLA_ROPE_DIM))
    zeros = jnp.zeros((LANES - 2 * half,), F32)
    inv_t = jnp.concatenate([inv, inv, zeros])[None, :]
    sgn_t = jnp.concatenate([-jnp.ones((half,), F32), jnp.ones((half,), F32), zeros])[None, :]
    row = pl.BlockSpec((1, LANES), lambda i: (0, 0))
    return pl.pallas_call(
        _rope_kernel,
        out_shape=(jax.ShapeDtypeStruct((n, LANES), F32),) * 2,
        grid=(n // tm,),
        in_specs=[pl.BlockSpec((tm, 1), lambda i: (i, 0)), row, row],
        out_specs=(pl.BlockSpec((tm, LANES), lambda i: (i, 0)),) * 2,
        compiler_params=_params("parallel"),
        name="rope_tables",
    )(positions.reshape(n, 1), inv_t, sgn_t)


def _inproj_kernel(x_ref, w_ref, o_ref):
    o_ref[...] = jnp.dot(x_ref[...].astype(BF16), w_ref[0], preferred_element_type=F32)


def _inproj(x, w, l):
    n, d = x.shape
    tm = min(n, TM_INPROJ)
    return pl.pallas_call(
        _inproj_kernel,
        out_shape=jax.ShapeDtypeStruct((n, D_IN_PAD), F32),
        grid=(D_IN_PAD // TN_INPROJ, n // tm),
        in_specs=[pl.BlockSpec((tm, d), lambda j, i: (i, 0)),
                  pl.BlockSpec((1, d, TN_INPROJ), lambda j, i: (l, 0, j))],
        out_specs=pl.BlockSpec((tm, TN_INPROJ), lambda j, i: (i, j)),
        compiler_params=_params("parallel", "parallel"),
        name="in_proj",
    )(x, w)


def _sgu_kernel(u_ref, v_ref, g_ref, b_ref, ws_ref, bs_ref, o_ref):
    u = jax.nn.gelu(u_ref[...])
    v = jax.nn.gelu(v_ref[...])
    mu = jnp.mean(v, axis=-1, keepdims=True)
    vc = v - mu
    var = jnp.mean(vc * vc, axis=-1, keepdims=True)
    vn = (vc * lax.rsqrt(var + NORM_EPS) * g_ref[...] + b_ref[...]).astype(BF16)
    row = lax.broadcasted_iota(jnp.int32, (CHUNK, CHUNK), 0)
    col = lax.broadcasted_iota(jnp.int32, (CHUNK, CHUNK), 1)
    causal = row >= col
    bs = bs_ref[...]
    for g in range(SGU_GROUPS):
        w = jnp.where(causal, ws_ref[g], 0.0).astype(BF16)
        cs = slice(g * SGU_GROUP_DIM, (g + 1) * SGU_GROUP_DIM)
        for c in range(u.shape[0] // CHUNK):
            rs = slice(c * CHUNK, (c + 1) * CHUNK)
            mixed = jnp.dot(w, vn[rs, cs], preferred_element_type=F32) + bs[:, g:g + 1]
            o_ref[rs, cs] = (u[rs, cs] * mixed).astype(o_ref.dtype)


def _sgu(h, ln_g, ln_b, ws, bs):
    n = h.shape[0]
    tm = min(n, TM_SGU)
    full = lambda shape: pl.BlockSpec(shape, lambda i: (0,) * len(shape))
    return pl.pallas_call(
        _sgu_kernel,
        out_shape=jax.ShapeDtypeStruct((n, SGU_WIDTH), BF16),
        grid=(n // tm,),
        in_specs=[pl.BlockSpec((tm, SGU_WIDTH), lambda i: (i, 0)),
                  pl.BlockSpec((tm, SGU_WIDTH), lambda i: (i, 1)),
                  full((1, SGU_WIDTH)), full((1, SGU_WIDTH)),
                  full((SGU_GROUPS, CHUNK, CHUNK)), full((CHUNK, SGU_GROUPS))],
        out_specs=pl.BlockSpec((tm, SGU_WIDTH), lambda i: (i, 0)),
        compiler_params=_params("parallel"),
        name="sgu",
    )(h, h, ln_g[None, :], ln_b[None, :], ws, bs.T)


def _dot_nt(a, b):
    return lax.dot_general(a, b, (((1,), (1,)), ((), ())), preferred_element_type=F32)


def _hgrn_kernel(q_ref, f_ref, i_ref, g_ref, la_ref, lc_ref, om_ref, ng_ref, o_ref, state_ref):
    @pl.when(pl.program_id(1) == 0)
    def _():
        state_ref[...] = jnp.zeros_like(state_ref)

    T, K = CHUNK, HGRN_HEAD_DIM
    row = lax.broadcasted_iota(jnp.int32, (T, T), 0)
    col = lax.broadcasted_iota(jnp.int32, (T, T), 1)
    tril = (row >= col).astype(F32)
    SUB = 8
    wide = {}
    for c, h in [(c, h) for c in range(q_ref.shape[0] // T) for h in range(HGRN_HEADS)]:
        rs = slice(c * T, (c + 1) * T)
        cs = slice(h * K, (h + 1) * K)
        if c not in wide:
            fx = f_ref[rs, :]
            log_f = jnp.logaddexp(la_ref[...], lc_ref[...] + jax.nn.log_sigmoid(fx))
            G_all = jnp.dot(tril, log_f, preferred_element_type=F32, precision=lax.Precision.HIGHEST)
            wide = {c: (G_all, om_ref[...] * jax.nn.sigmoid(-fx), jax.nn.silu(q_ref[rs, :]))}
        G, k, q = (a[:, cs] for a in wide[c])
        v = i_ref[rs, cs]
        A = jnp.zeros((T, T), F32)
        m = T // 2
        while m >= SUB:
            blk = 2 * m
            g_mid = jnp.concatenate(
                [jnp.broadcast_to(G[b * blk + m - 1:b * blk + m, :], (blk, K)) for b in range(T // blk)], axis=0)
            lower = (row % blk) >= m
            qs = jnp.where(lower, q * jnp.exp(G - g_mid), 0.0).astype(BF16)
            ks = jnp.where(lower, 0.0, k * jnp.exp(g_mid - G)).astype(BF16)
            same = (row // blk) == (col // blk)
            A = A + jnp.where(same, _dot_nt(qs, ks), 0.0)
            m //= 2
        rmod = row % SUB
        k3, G3 = k.reshape(T // SUB, SUB, K), G.reshape(T // SUB, SUB, K)
        for d in range(SUB):
            if d == 0:
                prod = q * k
            else:
                kd = pltpu.roll(k3, d, axis=1).reshape(T, K)
                gd = pltpu.roll(G3, d, axis=1).reshape(T, K)
                prod = jnp.where(rmod >= d, q * kd * jnp.exp(G - gd), 0.0)
            A = A + jnp.where(col == row - d, jnp.sum(prod, axis=-1, keepdims=True), 0.0)

        st = state_ref[h]
        o = jnp.dot(A.astype(BF16), v.astype(BF16), preferred_element_type=F32)
        o = o + _dot_nt((q * jnp.exp(G)).astype(BF16), st.astype(BF16))
        g_last = G[T - 1:T, :]
        k_dec = (k * jnp.exp(g_last - G)).astype(BF16)
        state_ref[h] = jnp.exp(g_last) * st + jnp.dot(v.T.astype(BF16), k_dec, preferred_element_type=F32)

        ms = jnp.mean(o * o, axis=-1, keepdims=True)
        o = o * lax.rsqrt(ms + NORM_EPS) * ng_ref[:, cs]
        o_ref[rs, cs] = (o * jax.nn.silu(g_ref[rs, cs])).astype(o_ref.dtype)


def _hgrn(h, lb, norm_g, batch):
    n = h.shape[0]
    tb = CHUNK * HGRN_CHUNKS_PER_STEP
    nc = n // batch // tb
    W = HGRN_WIDTH
    col = lambda j: pl.BlockSpec((tb, W), lambda b, c: (b * nc + c, j))
    vec = pl.BlockSpec((1, W), lambda b, c: (0, 0))
    base = 2 * SGU_WIDTH // W
    return pl.pallas_call(
        _hgrn_kernel,
        out_shape=jax.ShapeDtypeStruct((n, W), BF16),
        grid=(batch, nc),
        in_specs=[col(base), col(base + 1), col(base + 2), col(base + 3), vec, vec, vec, vec],
        out_specs=pl.BlockSpec((tb, W), lambda b, c: (b * nc + c, 0)),
        scratch_shapes=[pltpu.VMEM((HGRN_HEADS, HGRN_HEAD_DIM, HGRN_HEAD_DIM), F32)],
        compiler_params=_params("parallel", "arbitrary"),
        name="hgrn2",
    )(h, h, h, h, jnp.log(lb)[None, :], jnp.log1p(-lb)[None, :], (1.0 - lb)[None, :], norm_g[None, :])


def _rms(x, g):
    ms = jnp.mean(x * x, axis=-1, keepdims=True)
    return x * lax.rsqrt(ms + NORM_EPS) * g


def _rope(x, c, s):
    half = MLA_ROPE_DIM // 2
    lane = lax.broadcasted_iota(jnp.int32, x.shape, 1)
    swapped = jnp.where(lane < half, pltpu.roll(x, LANES - half, axis=1), pltpu.roll(x, half, axis=1))
    return x * c + swapped * s


def _mla_up_kernel(cq_ref, ckv_ref, kr_ref, c_ref, s_ref, qg_ref, kvg_ref, wq_ref, wkv_ref,
                   q_ref, kn_ref, v_ref, kro_ref):
    c, s = c_ref[...], s_ref[...]
    scale = MLA_QK_DIM ** -0.5 * LOG2_E
    q = jnp.dot(_rms(cq_ref[...], qg_ref[...]).astype(BF16), wq_ref[0], preferred_element_type=F32)
    kv = jnp.dot(_rms(ckv_ref[...], kvg_ref[...]).astype(BF16), wkv_ref[0], preferred_element_type=F32)
    for h in range(MLA_HEADS):
        base = h * Q_HEAD_PAD
        q_ref[h, :, :LANES] = (q[:, base:base + LANES] * scale).astype(BF16)
        q_ref[h, :, LANES:] = (_rope(q[:, base + LANES:base + 2 * LANES], c, s) * scale).astype(BF16)
        kbase = h * (MLA_NOPE_DIM + MLA_V_DIM)
        kn_ref[h] = kv[:, kbase:kbase + MLA_NOPE_DIM].astype(BF16)
        v_ref[h] = kv[:, kbase + MLA_NOPE_DIM:kbase + MLA_NOPE_DIM + MLA_V_DIM].astype(BF16)
    kro_ref[...] = _rope(kr_ref[...], c, s).astype(BF16)


def _mla_up(h, tab_c, tab_s, qn_g, w_uq, kvn_g, w_ukv, l):
    n = h.shape[0]
    tm = min(n, TM_MLA)
    H = MLA_HEADS
    cq_blk = (2 * SGU_WIDTH + 4 * HGRN_WIDTH) // Q_LORA_RANK
    kr_blk = (2 * SGU_WIDTH + 4 * HGRN_WIDTH + Q_LORA_RANK + KV_LORA_RANK) // LANES
    full = lambda shape: pl.BlockSpec(shape, lambda i: (0,) * len(shape))
    heads = lambda w: pl.BlockSpec((H, tm, w), lambda i: (0, i, 0))
    return pl.pallas_call(
        _mla_up_kernel,
        out_shape=(jax.ShapeDtypeStruct((H, n, Q_HEAD_PAD), BF16),
                   jax.ShapeDtypeStruct((H, n, MLA_NOPE_DIM), BF16),
                   jax.ShapeDtypeStruct((H, n, MLA_V_DIM), BF16),
                   jax.ShapeDtypeStruct((n, LANES), BF16)),
        grid=(n // tm,),
        in_specs=[pl.BlockSpec((tm, Q_LORA_RANK), lambda i: (i, cq_blk)),
                  pl.BlockSpec((tm, KV_LORA_RANK), lambda i: (i, cq_blk + 1)),
                  pl.BlockSpec((tm, LANES), lambda i: (i, kr_blk)),
                  pl.BlockSpec((tm, LANES), lambda i: (i, 0)),
                  pl.BlockSpec((tm, LANES), lambda i: (i, 0)),
                  full((1, Q_LORA_RANK)), full((1, KV_LORA_RANK)),
                  pl.BlockSpec((1, Q_LORA_RANK, H * Q_HEAD_PAD), lambda i: (l, 0, 0)),
                  pl.BlockSpec((1, KV_LORA_RANK, H * (MLA_NOPE_DIM + MLA_V_DIM)), lambda i: (l, 0, 0))],
        out_specs=(heads(Q_HEAD_PAD), heads(MLA_NOPE_DIM), heads(MLA_V_DIM),
                   pl.BlockSpec((tm, LANES), lambda i: (i, 0))),
        compiler_params=_params("parallel"),
        name="mla_up",
    )(h, h, h, tab_c, tab_s, qn_g[None, :], kvn_g[None, :], w_uq, w_ukv)


def _attn_kernel(q_ref, kn_ref, kr_ref, v_ref, o_ref, m_ref, l_ref, acc_ref):
    i = pl.program_id(2)
    nch, tk, _ = acc_ref.shape
    m_ref[...] = jnp.full_like(m_ref, -jnp.inf)
    l_ref[...] = jnp.zeros_like(l_ref)
    acc_ref[...] = jnp.zeros_like(acc_ref)

    def chain(c, j, masked):
        start = pl.multiple_of(j * tk, tk)
        q = q_ref[0, c * tk:(c + 1) * tk, :]
        k = jnp.concatenate([kn_ref[0, pl.ds(start, tk), :], kr_ref[pl.ds(start, tk), :]], axis=-1)
        s = _dot_nt(q, k)
        if masked:
            row = lax.broadcasted_iota(jnp.int32, s.shape, 0)
            col = lax.broadcasted_iota(jnp.int32, s.shape, 1)
            s = jnp.where(col <= row, s, -jnp.inf)
        chunks = [s[:, t * LANES:(t + 1) * LANES] for t in range(tk // LANES)]
        smax = functools.reduce(jnp.maximum, chunks)
        m_prev = m_ref[c]
        m_new = jnp.maximum(m_prev, jnp.max(smax, axis=-1, keepdims=True))
        alpha = jnp.exp2(m_prev - m_new)
        ps = [jnp.exp2(ch - m_new) for ch in chunks]
        l_ref[c] = alpha * l_ref[c] + functools.reduce(jnp.add, ps)
        p = jnp.concatenate([x.astype(BF16) for x in ps], axis=-1)
        acc_ref[c] = alpha * acc_ref[c] + jnp.dot(p, v_ref[0, pl.ds(start, tk), :], preferred_element_type=F32)
        m_ref[c] = m_new

    def body(j, carry):
        for c in range(nch):
            chain(c, j, masked=False)
        return carry

    lax.fori_loop(0, nch * i, body, 0)
    for t in range(nch):
        for c in range(t, nch):
            chain(c, nch * i + t, masked=(c == t))
    for c in range(nch):
        l = jnp.sum(l_ref[c], axis=-1, keepdims=True)
        o_ref[c * tk:(c + 1) * tk, :] = (acc_ref[c] / l).astype(o_ref.dtype)


def _attention(q, kn, kr, v, batch):
    H, n, _ = q.shape
    S = n // batch
    tq = min(S, TQ_ATTN)
    tk = min(tq, TK_ATTN)
    nq = S // tq
    return pl.pallas_call(
        _attn_kernel,
        out_shape=jax.ShapeDtypeStruct((n, H * MLA_V_DIM), BF16),
        grid=(batch, H, nq),
        in_specs=[pl.BlockSpec((1, tq, Q_HEAD_PAD), lambda b, h, i: (h, b * nq + i, 0)),
                  pl.BlockSpec((1, S, MLA_NOPE_DIM), lambda b, h, i: (h, b, 0)),
                  pl.BlockSpec((S, LANES), lambda b, h, i: (b, 0)),
                  pl.BlockSpec((1, S, MLA_V_DIM), lambda b, h, i: (h, b, 0))],
        out_specs=pl.BlockSpec((tq, MLA_V_DIM), lambda b, h, i: (b * nq + i, h)),
        scratch_shapes=[pltpu.VMEM((tq // tk, tk, LANES), F32), pltpu.VMEM((tq // tk, tk, LANES), F32),
                        pltpu.VMEM((tq // tk, tk, MLA_V_DIM), F32)],
        compiler_params=_params("parallel", "parallel", "arbitrary"),
        name="mla_attention",
    )(q, kn, kr, v)


def _layer_norm(x, g, b):
    mu = jnp.mean(x, axis=-1, keepdims=True)
    xc = x - mu
    var = jnp.mean(xc * xc, axis=-1, keepdims=True)
    return xc * lax.rsqrt(var + NORM_EPS) * g + b


def _first_index(hit, lane):
    return jnp.min(jnp.where(hit, lane, LANES), axis=-1, keepdims=True)


def _route(logits):
    lane = lax.broadcasted_iota(jnp.int32, logits.shape, 1)
    neg = -jnp.inf
    gl = jnp.where((lane >= N_EXPERTS) & (lane < N_EXPERTS + N_GROUPS), logits, neg)
    ge = jnp.exp(gl - jnp.max(gl, axis=-1, keepdims=True))
    gp = ge / jnp.sum(ge, axis=-1, keepdims=True)
    g_val = jnp.max(gp, axis=-1, keepdims=True)
    g_idx = _first_index(gp == g_val, lane) - N_EXPERTS
    el = jnp.where((lane // EXPERTS_PER_GROUP == g_idx) & (lane < N_EXPERTS), logits, neg)
    m1 = jnp.max(el, axis=-1, keepdims=True)
    i1 = _first_index(el == m1, lane)
    el2 = jnp.where(lane == i1, neg, el)
    m2 = jnp.max(el2, axis=-1, keepdims=True)
    i2 = _first_index(el2 == m2, lane)
    e2 = jnp.exp(m2 - m1)
    den = 1.0 + e2
    w1 = g_val * (1.0 / den)
    w2 = g_val * (e2 / den)
    return w1, w2, i1, i2


def _rows_to_slabs(ref, x, stride):
    m, d = x.shape
    for s in range(d // LANES):
        ref[pl.ds(s, m, stride=stride), :] = x[:, s * LANES:(s + 1) * LANES]


def _slabs_to_rows(ref, m, d, stride):
    return jnp.concatenate([ref[pl.ds(s, m, stride=stride), :] for s in range(d // LANES)], axis=-1)


def _outproj_kernel(ya_ref, yb_ref, yc_ref, x_ref, w_ref, g_ref, b_ref, wr_ref, br_ref,
                    x1_ref, x1s_ref, rt_ref, cnt_ref):
    @pl.when(pl.program_id(0) == 0)
    def _():
        cnt_ref[...] = jnp.zeros_like(cnt_ref)

    wa = SGU_WIDTH
    wb = SGU_WIDTH + HGRN_WIDTH
    d = x_ref.shape[1]
    slab = d // LANES
    ch = OUT_CHAIN_ROWS
    lane = lax.broadcasted_iota(jnp.int32, (ch, LANES), 1)
    row = lax.broadcasted_iota(jnp.int32, (ch, ch), 0)
    col = lax.broadcasted_iota(jnp.int32, (ch, ch), 1)
    before = (col < row).astype(BF16)
    carry = cnt_ref[...]
    for c in range(x_ref.shape[0] // ch):
        rs = slice(c * ch, (c + 1) * ch)
        mix = jnp.dot(ya_ref[rs, :], w_ref[0, :wa, :], preferred_element_type=F32)
        mix = mix + jnp.dot(yb_ref[rs, :], w_ref[0, wa:wb, :], preferred_element_type=F32)
        mix = mix + jnp.dot(yc_ref[rs, :], w_ref[0, wb:, :], preferred_element_type=F32)
        x1 = _layer_norm(DEEPNORM_ALPHA * x_ref[rs, :] + mix, g_ref[...], b_ref[...])
        x1_ref[rs, :] = x1
        _rows_to_slabs(x1s_ref.at[pl.ds(c * ch * slab, ch * slab), :], x1, slab)
        x_hi = x1.astype(BF16)
        x_lo = (x1 - x_hi.astype(F32)).astype(BF16)
        r = jnp.dot(x_hi, wr_ref[...], preferred_element_type=F32)
        logits = (r[:, :LANES] + r[:, LANES:] + jnp.dot(x_lo, wr_ref[:, :LANES], preferred_element_type=F32)
                  + br_ref[...])
        w1, w2, i1, i2 = _route(logits)

        hit1, hit2 = lane == i1, lane == i2
        oh1, oh2 = hit1.astype(F32), hit2.astype(F32)
        pre = jnp.dot(before, jnp.concatenate([oh1, oh2], axis=-1).astype(BF16), preferred_element_type=F32)
        tot1 = jnp.sum(oh1, axis=0, keepdims=True)
        tot2 = jnp.sum(oh2, axis=0, keepdims=True)
        rank1 = jnp.sum(jnp.where(hit1, pre[:, :LANES] + carry, 0.0), axis=-1, keepdims=True)
        rank2 = jnp.sum(jnp.where(hit2, pre[:, LANES:] + (carry + tot1), 0.0), axis=-1, keepdims=True)
        carry = carry + tot1 + tot2

        out = jnp.where(lane == 0, w1, 0.0)
        out = jnp.where(lane == 1, w2, out)
        out = jnp.where(lane == 2, i1.astype(F32), out)
        out = jnp.where(lane == 3, i2.astype(F32), out)
        out = jnp.where(lane == 4, rank1, out)
        out = jnp.where(lane == 5, rank2, out)
        rt_ref[rs, :] = out
    cnt_ref[...] = carry


def _outproj(ya, yb, yc, x, w_out, ln_g, ln_b, wr, br, l):
    n, d = x.shape
    tm = min(n, TM_OUT)
    assert tm % OUT_CHAIN_ROWS == 0
    slab = d // LANES
    full = lambda shape: pl.BlockSpec(shape, lambda i: (0,) * len(shape))
    rows = lambda w: pl.BlockSpec((tm, w), lambda i: (i, 0))
    wr_hi = wr.astype(BF16)
    wr_lo = (wr - wr_hi.astype(F32)).astype(BF16)
    return pl.pallas_call(
        _outproj_kernel,
        out_shape=(jax.ShapeDtypeStruct((n, d), F32), jax.ShapeDtypeStruct((n * slab, LANES), F32),
                   jax.ShapeDtypeStruct((n, LANES), F32), jax.ShapeDtypeStruct((1, LANES), F32)),
        grid=(n // tm,),
        in_specs=[rows(ya.shape[1]), rows(yb.shape[1]), rows(yc.shape[1]), rows(d),
                  pl.BlockSpec((1, d, d), lambda i: (l, 0, 0), pipeline_mode=pl.Buffered(1)), full((1, d)), full((1, d)),
                  full((d, 2 * LANES)), full((1, LANES))],
        out_specs=(rows(d), pl.BlockSpec((tm * slab, LANES), lambda i: (i, 0)), rows(LANES), full((1, LANES))),
        compiler_params=_params("arbitrary"),
        name="out_proj_ln_router",
    )(ya, yb, yc, x, w_out, ln_g[None, :], ln_b[None, :], jnp.concatenate([wr_hi, wr_lo], axis=1), br)


def _moe_kernel(src_ref, dst_ref, exp_ref, used_ref, x_hbm, wg_ref, wu_ref, wd_ref, y_hbm,
                xbuf0, xbuf1, xbuf2, xbuf3, ybuf0, ybuf1, ybuf2, ybuf3, wgb, wub, wdb, gsem, ssem):
    xbufs, ybufs = (xbuf0, xbuf1, xbuf2, xbuf3), (ybuf0, ybuf1, ybuf2, ybuf3)
    nx = len(xbufs)
    i = pl.program_id(0)
    nb = pl.num_programs(0) - 1
    d = wgb.shape[0]
    slab = d // LANES
    vs = MOE_VMEM_STRIDE
    rows = xbuf0.shape[0] // vs
    used = used_ref[0]
    n_real = y_hbm.shape[0] // slab - (nx + 1) * rows

    def gather_copy(tok, r, slot):
        src = x_hbm.at[pl.ds(pl.multiple_of(tok * slab, slab), slab), :]
        return pltpu.make_async_copy(src, xbufs[slot].at[pl.ds(r * vs, slab), :], gsem.at[slot])

    def scatter_copy(dst, r, slot):
        out = y_hbm.at[pl.ds(pl.multiple_of(dst * slab, slab), slab), :]
        return pltpu.make_async_copy(ybufs[slot].at[pl.ds(r * vs, slab), :], out, ssem.at[slot])

    def fill_copy(ybuf, region, sem):
        return pltpu.make_async_copy(ybuf.at[pl.ds(0, rows * slab), :],
                                     y_hbm.at[pl.ds((n_real + region * rows) * slab, rows * slab), :], sem)

    def wait_gather(slot):
        pltpu.make_async_copy(x_hbm.at[pl.ds(0, rows * slab), :], xbuf0.at[pl.ds(0, rows * slab), :],
                              gsem.at[slot]).wait()

    def wait_scatter(slot):
        fill_copy(ybuf0, 0, ssem.at[slot]).wait()

    @pl.when(i == 0)
    def _():
        ybuf3[...] = jnp.zeros_like(ybuf3)
        for t in range(nx - 1):
            fill_copy(ybuf3, 1 + t, ssem.at[t]).start()
        last = fill_copy(ybuf3, nx, ssem.at[nx])
        last.start()
        last.wait()
        for blk in range(2):
            def issue(r, carry):
                gather_copy(src_ref[jnp.minimum(blk, nb - 1) * rows + r], r, blk).start()
                return carry
            lax.fori_loop(0, rows, issue, 0, unroll=8)

    cur = jnp.minimum(i, nb - 1)
    @pl.when((i < used) & ((i == 0) | (exp_ref[cur] != exp_ref[jnp.maximum(cur - 1, 0)])))
    def _():
        wgb[...] = wg_ref[0, 0].astype(BF16)
        wub[...] = wu_ref[0, 0].astype(BF16)
        wdb[...] = wd_ref[0, 0].astype(BF16)

    def compute_block(phase):
        xs = ys = phase
        wait_gather(xs)
        wait_scatter(ys)
        nxt = jnp.minimum(i + 2, nb - 1) * rows
        prv = i * rows
        for r in range(rows):
            gather_copy(src_ref[nxt + r], r, (xs + 2) % nx).start()
            scatter_copy(dst_ref[prv + r], r, (ys - 1) % nx).start()
        xb = _slabs_to_rows(xbufs[xs], rows, d, vs).astype(BF16)
        hg = jnp.dot(xb, wgb[...], preferred_element_type=F32)
        hu = jnp.dot(xb, wub[...], preferred_element_type=F32)
        hh = (jax.nn.silu(hg) * hu).astype(BF16)
        _rows_to_slabs(ybufs[ys], jnp.dot(hh, wdb[...], preferred_element_type=F32), vs)

    def drain_block(ys):
        def issue(r, carry):
            scatter_copy(dst_ref[i * rows + r], r, ys).start()
            return carry
        lax.fori_loop(0, rows, issue, 0, unroll=8)
        wait_scatter(ys)

    for phase in range(nx):
        pl.when((i < used) & (i % nx == phase))(functools.partial(compute_block, phase))

    @pl.when(i == used)
    def _():
        wait_gather(i % nx)
        wait_gather((i + 1) % nx)
        for t in range(nx - 1):
            wait_scatter((i + t) % nx)

    for phase in range(nx):
        pl.when((i == used) & (i >= 1) & ((i - 1) % nx == phase))(functools.partial(drain_block, phase))


def _moe(slot_src, slot_dst, blk_exp, used, x1s, w_gate, w_up, w_down, n_out, l):
    d, de = w_gate.shape[2:]
    slab = d // LANES
    nb = blk_exp.shape[0]
    rows = slot_src.shape[0] // nb
    blk = lambda i, src, dst, exp, used: (l, exp[jnp.minimum(i, nb - 1)], 0, 0)
    return pl.pallas_call(
        _moe_kernel,
        out_shape=jax.ShapeDtypeStruct((n_out * slab, LANES), F32),
        grid_spec=pltpu.PrefetchScalarGridSpec(
            num_scalar_prefetch=4,
            grid=(nb + 1,),
            in_specs=[pl.BlockSpec(memory_space=pl.ANY),
                      pl.BlockSpec((1, 1, d, de), blk), pl.BlockSpec((1, 1, d, de), blk),
                      pl.BlockSpec((1, 1, de, d), blk)],
            out_specs=pl.BlockSpec(memory_space=pl.ANY),
            scratch_shapes=[pltpu.VMEM((rows * MOE_VMEM_STRIDE, LANES), F32)] * (2 * MOE_RING) + [
                            pltpu.VMEM((d, de), BF16), pltpu.VMEM((d, de), BF16), pltpu.VMEM((de, d), BF16),
                            pltpu.SemaphoreType.DMA((MOE_RING,)), pltpu.SemaphoreType.DMA((MOE_RING + 1,))]),
        compiler_params=_params("arbitrary"),
        name="moe_experts",
    )(slot_src, slot_dst, blk_exp, used, x1s, w_gate, w_up, w_down)


def _combine_kernel(y0_ref, y1_ref, x_ref, rt_ref, g_ref, b_ref, o_ref):
    tm, d = x_ref.shape
    slab = d // LANES
    rt = rt_ref[...]
    ffn = _slabs_to_rows(y0_ref, tm, d, slab) * rt[:, 0:1] + _slabs_to_rows(y1_ref, tm, d, slab) * rt[:, 1:2]
    o_ref[...] = _layer_norm(DEEPNORM_ALPHA * x_ref[...] + ffn, g_ref[...], b_ref[...])


def _combine(y_tok, x1, route, ln_g, ln_b):
    n, d = x1.shape
    tm = min(n, TM_COMB)
    slab = d // LANES
    return pl.pallas_call(
        _combine_kernel,
        out_shape=jax.ShapeDtypeStruct((n, d), F32),
        grid=(n // tm,),
        in_specs=[pl.BlockSpec((tm * slab, LANES), lambda i: (i, 0)),
                  pl.BlockSpec((tm * slab, LANES), lambda i: (i + n // tm, 0)),
                  pl.BlockSpec((tm, d), lambda i: (i, 0)),
                  pl.BlockSpec((tm, LANES), lambda i: (i, 0)),
                  pl.BlockSpec((1, d), lambda i: (0, 0)),
                  pl.BlockSpec((1, d), lambda i: (0, 0))],
        out_specs=pl.BlockSpec((tm, d), lambda i: (i, 0)),
        compiler_params=_params("parallel"),
        name="moe_combine_ln",
    )(y_tok, y_tok, x1, route, ln_g[None, :], ln_b[None, :])


def _slots_kernel(dest_ref, src0_hbm, dst0_hbm, src_ref, dst_ref, sem):
    fills = [pltpu.make_async_copy(src0_hbm, src_ref, sem.at[0]), pltpu.make_async_copy(dst0_hbm, dst_ref, sem.at[1])]
    for f in fills:
        f.start()
    for f in fills:
        f.wait()
    nk = dest_ref.shape[0]
    n = nk // TOP_K
    lead = dst_ref.shape[0] - src_ref.shape[0]

    batch = 16

    def place(b, carry):
        t0 = b * (batch // TOP_K)
        slots = [dest_ref[b * batch + u] for u in range(batch)]
        for u, d in enumerate(slots):
            t = t0 + u // TOP_K
            src_ref[d] = t
            dst_ref[lead + d] = (u % TOP_K) * n + t
        return carry

    lax.fori_loop(0, nk // batch, place, 0)


def _slot_tables(dest, src0, dst0):
    smem = pl.BlockSpec(memory_space=pltpu.SMEM)
    hbm = pl.BlockSpec(memory_space=pl.ANY)
    return pl.pallas_call(
        _slots_kernel,
        out_shape=(jax.ShapeDtypeStruct(src0.shape, jnp.int32), jax.ShapeDtypeStruct(dst0.shape, jnp.int32)),
        grid_spec=pltpu.PrefetchScalarGridSpec(
            num_scalar_prefetch=1, grid=(1,), in_specs=[hbm, hbm], out_specs=(smem, smem),
            scratch_shapes=[pltpu.SemaphoreType.DMA((2,))]),
        name="moe_slot_tables",
    )(dest, src0, dst0)


def _dispatch_plan(route, counts, rows):
    n = route.shape[0]
    nk = n * TOP_K
    nb = (nk + rows - 1) // rows + N_EXPERTS
    counts = counts[0, :N_EXPERTS].astype(jnp.int32)
    padded = ((counts + rows - 1) // rows) * rows
    pad_end = jnp.cumsum(padded)
    pad_start = pad_end - padded
    expert = route[:, 2:2 + TOP_K].astype(jnp.int32)
    rank = route[:, 2 + TOP_K:2 + 2 * TOP_K].astype(jnp.int32)
    dest = (pad_start[expert] + rank).reshape(nk)
    slot = jnp.arange(nb * rows, dtype=jnp.int32)
    dump = nk + rows + ((slot // rows) % MOE_RING) * rows + slot % rows
    slot_src, slot_dst = _slot_tables(dest, jnp.zeros((nb * rows,), jnp.int32),
                                      jnp.concatenate([nk + jnp.arange(rows, dtype=jnp.int32), dump]))
    blk_exp = jnp.minimum(
        jnp.sum((pad_end[None, :] <= (jnp.arange(nb, dtype=jnp.int32) * rows)[:, None]).astype(jnp.int32), axis=1),
        N_EXPERTS - 1).astype(jnp.int32)
    used = (pad_end[-1] // rows).astype(jnp.int32).reshape(1)
    return slot_src, slot_dst, blk_exp, used, nk + (MOE_RING + 1) * rows


def _pad_q_heads(w_uq):
    lead = w_uq.shape[:-1]
    w = w_uq.reshape(*lead, MLA_HEADS, MLA_QK_DIM)
    w = jnp.pad(w, [(0, 0)] * len(lead) + [(0, 0), (0, Q_HEAD_PAD - MLA_QK_DIM)])
    return w.reshape(*lead, MLA_HEADS * Q_HEAD_PAD)


def kernel(x, positions, w_in, sgu_ln_g, sgu_ln_b, sgu_ws, sgu_b, hgrn_lb_logits, hgrn_norm_g, mla_qn_g, mla_w_uq, mla_kvn_g, mla_w_ukv, w_out, ln1_g, ln1_b, router_group_w, router_group_b, router_expert_w, router_expert_b, expert_w_gate, expert_w_up, expert_w_down, ln2_g, ln2_b):
    B, S, D = x.shape
    n = B * S
    depth = w_in.shape[0]
    tab_c, tab_s = _rope_tables(positions)
    lb_cum = jnp.cumsum(jax.nn.softmax(hgrn_lb_logits.astype(F32), axis=0), axis=0)
    lower_bounds = lb_cum - lb_cum[0:1]
    xt = x.reshape(n, D)
    w_in_b = jnp.pad(w_in, ((0, 0), (0, 0), (0, D_IN_PAD - D_IN))).astype(BF16)
    w_uq_b = _pad_q_heads(mla_w_uq).astype(BF16)
    w_ukv_b = mla_w_ukv.astype(BF16)
    w_out_b = w_out.astype(BF16)
    for l in range(depth):
        h = _inproj(xt, w_in_b, l)
        y_a = _sgu(h, sgu_ln_g[l], sgu_ln_b[l], sgu_ws[l], sgu_b[l])
        y_b = _hgrn(h, lower_bounds[l], hgrn_norm_g[l], B)
        q, kn, v, kr = _mla_up(h, tab_c, tab_s, mla_qn_g[l], w_uq_b, mla_kvn_g[l], w_ukv_b, l)
        y_c = _attention(q, kn, kr, v, B)
        wr = jnp.pad(jnp.concatenate([router_expert_w[l], router_group_w[l]], axis=1),
                     ((0, 0), (0, LANES - N_EXPERTS - N_GROUPS)))
        br = jnp.pad(jnp.concatenate([router_expert_b[l], router_group_b[l]]), (0, LANES - N_EXPERTS - N_GROUPS))
        x1, x1s, route, counts = _outproj(y_a, y_b, y_c, xt, w_out_b, ln1_g[l], ln1_b[l], wr, br[None, :], l)
        slot_src, slot_dst, blk_exp, used, n_out = _dispatch_plan(route, counts, MOE_ROWS)
        y_tok = _moe(slot_src, slot_dst, blk_exp, used, x1s, expert_w_gate, expert_w_up, expert_w_down, n_out, l)
        xt = _combine(y_tok, x1, route, ln2_g[l], ln2_b[l])
    return xt.reshape(B, S, D)
```

```python
import functools

import jax
import jax.numpy as jnp
from jax import lax
from jax.experimental import pallas as pl
from jax.experimental.pallas import tpu as pltpu

D_MODEL = 2048
DEPTH = 4
CHUNK = 128
SGU_WIDTH = D_MODEL // 4
SGU_GROUPS = 4
SGU_GROUP_DIM = SGU_WIDTH // SGU_GROUPS
HGRN_WIDTH = D_MODEL // 4
HGRN_HEADS = 4
HGRN_HEAD_DIM = HGRN_WIDTH // HGRN_HEADS
MLA_HEADS = 8
MLA_V_DIM = (D_MODEL - SGU_WIDTH - HGRN_WIDTH) // MLA_HEADS
MLA_NOPE_DIM = 128
MLA_ROPE_DIM = 64
MLA_QK_DIM = MLA_NOPE_DIM + MLA_ROPE_DIM
Q_LORA_RANK = D_MODEL // 4
KV_LORA_RANK = D_MODEL // 4
ROPE_THETA = 10000.0
N_GROUPS = 4
EXPERTS_PER_GROUP = 8
N_EXPERTS = N_GROUPS * EXPERTS_PER_GROUP
TOP_K = 2
D_EXPERT = D_MODEL // 4
NORM_EPS = 1e-5
DEEPNORM_ALPHA = (2 * DEPTH) ** 0.25
D_IN = SGU_WIDTH * 2 + HGRN_WIDTH * 4 + Q_LORA_RANK + KV_LORA_RANK + MLA_ROPE_DIM

LANES = 128
D_IN_PAD = ((D_IN + LANES - 1) // LANES) * LANES
Q_HEAD_PAD = 2 * LANES
VMEM_LIMIT = 56 * 1024 * 1024

TM_INPROJ = 1024
TN_INPROJ = D_IN_PAD // 3
TM_SGU = 1024
HGRN_CHUNKS_PER_STEP = 8
TM_MLA = 512
TQ_ATTN = 4096
TK_ATTN = 1024
TM_OUT = 512
OUT_CHAIN_ROWS = 256
MOE_ROWS = 256
MOE_RING = 4
MOE_VMEM_STRIDE = 20
TM_COMB = 512

BF16 = jnp.bfloat16
F32 = jnp.float32
LOG2_E = 1.4426950408889634


def _params(*sem):
    return pltpu.CompilerParams(dimension_semantics=sem, vmem_limit_bytes=VMEM_LIMIT)


def _rope_kernel(pos_ref, inv_ref, sgn_ref, c_ref, s_ref):
    ang = pos_ref[...].astype(F32) * inv_ref[...]
    live = sgn_ref[...] != 0.0
    c_ref[...] = jnp.where(live, jnp.cos(ang), 0.0)
    s_ref[...] = jnp.sin(ang) * sgn_ref[...]


def _rope_tables(positions):
    n = positions.size
    tm = min(n, 1024)
    half = MLA_ROPE_DIM // 2
    inv = 1.0 / (ROPE_THETA ** (jnp.arange(0, MLA_ROPE_DIM, 2, dtype=F32) / MLA_ROPE_DIM))
    zeros = jnp.zeros((LANES - 2 * half,), F32)
    inv_t = jnp.concatenate([inv, inv, zeros])[None, :]
    sgn_t = jnp.concatenate([-jnp.ones((half,), F32), jnp.ones((half,), F32), zeros])[None, :]
    row = pl.BlockSpec((1, LANES), lambda i: (0, 0))
    return pl.pallas_call(
        _rope_kernel,
        out_shape=(jax.ShapeDtypeStruct((n, LANES), F32),) * 2,
        grid=(n // tm,),
        in_specs=[pl.BlockSpec((tm, 1), lambda i: (i, 0)), row, row],
        out_specs=(pl.BlockSpec((tm, LANES), lambda i: (i, 0)),) * 2,
        compiler_params=_params("parallel"),
        name="rope_tables",
    )(positions.reshape(n, 1), inv_t, sgn_t)


def _inproj_kernel(x_ref, w_ref, o_ref):
    o_ref[...] = jnp.dot(x_ref[...].astype(BF16), w_ref[0], preferred_element_type=F32)


def _inproj(x, w, l):
    n, d = x.shape
    tm = min(n, TM_INPROJ)
    return pl.pallas_call(
        _inproj_kernel,
        out_shape=jax.ShapeDtypeStruct((n, D_IN_PAD), F32),
        grid=(D_IN_PAD // TN_INPROJ, n // tm),
        in_specs=[pl.BlockSpec((tm, d), lambda j, i: (i, 0)),
                  pl.BlockSpec((1, d, TN_INPROJ), lambda j, i: (l, 0, j))],
        out_specs=pl.BlockSpec((tm, TN_INPROJ), lambda j, i: (i, j)),
        compiler_params=_params("parallel", "parallel"),
        name="in_proj",
    )(x, w)


def _sgu_kernel(u_ref, v_ref, g_ref, b_ref, ws_ref, bs_ref, o_ref):
    u = jax.nn.gelu(u_ref[...])
    v = jax.nn.gelu(v_ref[...])
    mu = jnp.mean(v, axis=-1, keepdims=True)
    vc = v - mu
    var = jnp.mean(vc * vc, axis=-1, keepdims=True)
    vn = (vc * lax.rsqrt(var + NORM_EPS) * g_ref[...] + b_ref[...]).astype(BF16)
    row = lax.broadcasted_iota(jnp.int32, (CHUNK, CHUNK), 0)
    col = lax.broadcasted_iota(jnp.int32, (CHUNK, CHUNK), 1)
    causal = row >= col
    bs = bs_ref[...]
    for g in range(SGU_GROUPS):
        w = jnp.where(causal, ws_ref[g], 0.0).astype(BF16)
        cs = slice(g * SGU_GROUP_DIM, (g + 1) * SGU_GROUP_DIM)
        for c in range(u.shape[0] // CHUNK):
            rs = slice(c * CHUNK, (c + 1) * CHUNK)
            mixed = jnp.dot(w, vn[rs, cs], preferred_element_type=F32) + bs[:, g:g + 1]
            o_ref[rs, cs] = (u[rs, cs] * mixed).astype(o_ref.dtype)


def _sgu(h, ln_g, ln_b, ws, bs):
    n = h.shape[0]
    tm = min(n, TM_SGU)
    full = lambda shape: pl.BlockSpec(shape, lambda i: (0,) * len(shape))
    return pl.pallas_call(
        _sgu_kernel,
        out_shape=jax.ShapeDtypeStruct((n, SGU_WIDTH), BF16),
        grid=(n // tm,),
        in_specs=[pl.BlockSpec((tm, SGU_WIDTH), lambda i: (i, 0)),
                  pl.BlockSpec((tm, SGU_WIDTH), lambda i: (i, 1)),
                  full((1, SGU_WIDTH)), full((1, SGU_WIDTH)),
                  full((SGU_GROUPS, CHUNK, CHUNK)), full((CHUNK, SGU_GROUPS))],
        out_specs=pl.BlockSpec((tm, SGU_WIDTH), lambda i: (i, 0)),
        compiler_params=_params("parallel"),
        name="sgu",
    )(h, h, ln_g[None, :], ln_b[None, :], ws, bs.T)


def _dot_nt(a, b):
    return lax.dot_general(a, b, (((1,), (1,)), ((), ())), preferred_element_type=F32)


def _hgrn_kernel(q_ref, f_ref, i_ref, g_ref, la_ref, lc_ref, om_ref, ng_ref, o_ref, state_ref):
    @pl.when(pl.program_id(1) == 0)
    def _():
        state_ref[...] = jnp.zeros_like(state_ref)

    T, K = CHUNK, HGRN_HEAD_DIM
    row = lax.broadcasted_iota(jnp.int32, (T, T), 0)
    col = lax.broadcasted_iota(jnp.int32, (T, T), 1)
    tril = (row >= col).astype(F32)
    SUB = 8
    wide = {}
    for c, h in [(c, h) for c in range(q_ref.shape[0] // T) for h in range(HGRN_HEADS)]:
        rs = slice(c * T, (c + 1) * T)
        cs = slice(h * K, (h + 1) * K)
        if c not in wide:
            fx = f_ref[rs, :]
            log_f = jnp.logaddexp(la_ref[...], lc_ref[...] + jax.nn.log_sigmoid(fx))
            G_all = jnp.dot(tril, log_f, preferred_element_type=F32, precision=lax.Precision.HIGHEST)
            wide = {c: (G_all, om_ref[...] * jax.nn.sigmoid(-fx), jax.nn.silu(q_ref[rs, :]))}
        G, k, q = (a[:, cs] for a in wide[c])
        v = i_ref[rs, cs]
        A = jnp.zeros((T, T), F32)
        m = T // 2
        while m >= SUB:
            blk = 2 * m
            g_mid = jnp.concatenate(
                [jnp.broadcast_to(G[b * blk + m - 1:b * blk + m, :], (blk, K)) for b in range(T // blk)], axis=0)
            lower = (row % blk) >= m
            qs = jnp.where(lower, q * jnp.exp(G - g_mid), 0.0).astype(BF16)
            ks = jnp.where(lower, 0.0, k * jnp.exp(g_mid - G)).astype(BF16)
            same = (row // blk) == (col // blk)
            A = A + jnp.where(same, _dot_nt(qs, ks), 0.0)
            m //= 2
        rmod = row % SUB
        k3, G3 = k.reshape(T // SUB, SUB, K), G.reshape(T // SUB, SUB, K)
        for d in range(SUB):
            if d == 0:
                prod = q * k
            else:
                kd = pltpu.roll(k3, d, axis=1).reshape(T, K)
                gd = pltpu.roll(G3, d, axis=1).reshape(T, K)
                prod = jnp.where(rmod >= d, q * kd * jnp.exp(G - gd), 0.0)
            A = A + jnp.where(col == row - d, jnp.sum(prod, axis=-1, keepdims=True), 0.0)

        st = state_ref[h]
        o = jnp.dot(A.astype(BF16), v.astype(BF16), preferred_element_type=F32)
        o = o + _dot_nt((q * jnp.exp(G)).astype(BF16), st.astype(BF16))
        g_last = G[T - 1:T, :]
        k_dec = (k * jnp.exp(g_last - G)).astype(BF16)
        state_ref[h] = jnp.exp(g_last) * st + jnp.dot(v.T.astype(BF16), k_dec, preferred_element_type=F32)

        ms = jnp.mean(o * o, axis=-1, keepdims=True)
        o = o * lax.rsqrt(ms + NORM_EPS) * ng_ref[:, cs]
        o_ref[rs, cs] = (o * jax.nn.silu(g_ref[rs, cs])).astype(o_ref.dtype)


def _hgrn(h, lb, norm_g, batch):
    n = h.shape[0]
    tb = CHUNK * HGRN_CHUNKS_PER_STEP
    nc = n // batch // tb
    W = HGRN_WIDTH
    col = lambda j: pl.BlockSpec((tb, W), lambda b, c: (b * nc + c, j))
    vec = pl.BlockSpec((1, W), lambda b, c: (0, 0))
    base = 2 * SGU_WIDTH // W
    return pl.pallas_call(
        _hgrn_kernel,
        out_shape=jax.ShapeDtypeStruct((n, W), BF16),
        grid=(batch, nc),
        in_specs=[col(base), col(base + 1), col(base + 2), col(base + 3), vec, vec, vec, vec],
        out_specs=pl.BlockSpec((tb, W), lambda b, c: (b * nc + c, 0)),
        scratch_shapes=[pltpu.VMEM((HGRN_HEADS, HGRN_HEAD_DIM, HGRN_HEAD_DIM), F32)],
        compiler_params=_params("parallel", "arbitrary"),
        name="hgrn2",
    )(h, h, h, h, jnp.log(lb)[None, :], jnp.log1p(-lb)[None, :], (1.0 - lb)[None, :], norm_g[None, :])


def _rms(x, g):
    ms = jnp.mean(x * x, axis=-1, keepdims=True)
    return x * lax.rsqrt(ms + NORM_EPS) * g


def _rope(x, c, s):
    half = MLA_ROPE_DIM // 2
    lane = lax.broadcasted_iota(jnp.int32, x.shape, 1)
    swapped = jnp.where(lane < half, pltpu.roll(x, LANES - half, axis=1), pltpu.roll(x, half, axis=1))
    return x * c + swapped * s


def _mla_up_kernel(cq_ref, ckv_ref, kr_ref, c_ref, s_ref, qg_ref, kvg_ref, wq_ref, wkv_ref,
                   q_ref, kn_ref, v_ref, kro_ref):
    c, s = c_ref[...], s_ref[...]
    scale = MLA_QK_DIM ** -0.5 * LOG2_E
    q = jnp.dot(_rms(cq_ref[...], qg_ref[...]).astype(BF16), wq_ref[0], preferred_element_type=F32)
    kv = jnp.dot(_rms(ckv_ref[...], kvg_ref[...]).astype(BF16), wkv_ref[0], preferred_element_type=F32)
    for h in range(MLA_HEADS):
        base = h * Q_HEAD_PAD
        q_ref[h, :, :LANES] = (q[:, base:base + LANES] * scale).astype(BF16)
        q_ref[h, :, LANES:] = (_rope(q[:, base + LANES:base + 2 * LANES], c, s) * scale).astype(BF16)
        kbase = h * (MLA_NOPE_DIM + MLA_V_DIM)
        kn_ref[h] = kv[:, kbase:kbase + MLA_NOPE_DIM].astype(BF16)
        v_ref[h] = kv[:, kbase + MLA_NOPE_DIM:kbase + MLA_NOPE_DIM + MLA_V_DIM].astype(BF16)
    kro_ref[...] = _rope(kr_ref[...], c, s).astype(BF16)


def _mla_up(h, tab_c, tab_s, qn_g, w_uq, kvn_g, w_ukv, l):
    n = h.shape[0]
    tm = min(n, TM_MLA)
    H = MLA_HEADS
    cq_blk = (2 * SGU_WIDTH + 4 * HGRN_WIDTH) // Q_LORA_RANK
    kr_blk = (2 * SGU_WIDTH + 4 * HGRN_WIDTH + Q_LORA_RANK + KV_LORA_RANK) // LANES
    full = lambda shape: pl.BlockSpec(shape, lambda i: (0,) * len(shape))
    heads = lambda w: pl.BlockSpec((H, tm, w), lambda i: (0, i, 0))
    return pl.pallas_call(
        _mla_up_kernel,
        out_shape=(jax.ShapeDtypeStruct((H, n, Q_HEAD_PAD), BF16),
                   jax.ShapeDtypeStruct((H, n, MLA_NOPE_DIM), BF16),
                   jax.ShapeDtypeStruct((H, n, MLA_V_DIM), BF16),
                   jax.ShapeDtypeStruct((n, LANES), BF16)),
        grid=(n // tm,),
        in_specs=[pl.BlockSpec((tm, Q_LORA_RANK), lambda i: (i, cq_blk)),
                  pl.BlockSpec((tm, KV_LORA_RANK), lambda i: (i, cq_blk + 1)),
                  pl.BlockSpec((tm, LANES), lambda i: (i, kr_blk)),
                  pl.BlockSpec((tm, LANES), lambda i: (i, 0)),
                  pl.BlockSpec((tm, LANES), lambda i: (i, 0)),
                  full((1, Q_LORA_RANK)), full((1, KV_LORA_RANK)),
                  pl.BlockSpec((1, Q_LORA_RANK, H * Q_HEAD_PAD), lambda i: (l, 0, 0)),
                  pl.BlockSpec((1, KV_LORA_RANK, H * (MLA_NOPE_DIM + MLA_V_DIM)), lambda i: (l, 0, 0))],
        out_specs=(heads(Q_HEAD_PAD), heads(MLA_NOPE_DIM), heads(MLA_V_DIM),
                   pl.BlockSpec((tm, LANES), lambda i: (i, 0))),
        compiler_params=_params("parallel"),
        name="mla_up",
    )(h, h, h, tab_c, tab_s, qn_g[None, :], kvn_g[None, :], w_uq, w_ukv)


def _attn_kernel(q_ref, kn_ref, kr_ref, v_ref, o_ref, m_ref, l_ref, acc_ref):
    i = pl.program_id(2)
    nch, tk, _ = acc_ref.shape
    m_ref[...] = jnp.full_like(m_ref, -jnp.inf)
    l_ref[...] = jnp.zeros_like(l_ref)
    acc_ref[...] = jnp.zeros_like(acc_ref)

    def chain(c, j, masked):
        start = pl.multiple_of(j * tk, tk)
        q = q_ref[0, c * tk:(c + 1) * tk, :]
        k = jnp.concatenate([kn_ref[0, pl.ds(start, tk), :], kr_ref[pl.ds(start, tk), :]], axis=-1)
        s = _dot_nt(q, k)
        if masked:
            row = lax.broadcasted_iota(jnp.int32, s.shape, 0)
            col = lax.broadcasted_iota(jnp.int32, s.shape, 1)
            s = jnp.where(col <= row, s, -jnp.inf)
        chunks = [s[:, t * LANES:(t + 1) * LANES] for t in range(tk // LANES)]
        smax = functools.reduce(jnp.maximum, chunks)
        m_prev = m_ref[c]
        m_new = jnp.maximum(m_prev, jnp.max(smax, axis=-1, keepdims=True))
        alpha = jnp.exp2(m_prev - m_new)
        ps = [jnp.exp2(ch - m_new) for ch in chunks]
        l_ref[c] = alpha * l_ref[c] + functools.reduce(jnp.add, ps)
        p = jnp.concatenate([x.astype(BF16) for x in ps], axis=-1)
        acc_ref[c] = alpha * acc_ref[c] + jnp.dot(p, v_ref[0, pl.ds(start, tk), :], preferred_element_type=F32)
        m_ref[c] = m_new

    def body(j, carry):
        for c in range(nch):
            chain(c, j, masked=False)
        return carry

    lax.fori_loop(0, nch * i, body, 0)
    for t in range(nch):
        for c in range(t, nch):
            chain(c, nch * i + t, masked=(c == t))
    for c in range(nch):
        l = jnp.sum(l_ref[c], axis=-1, keepdims=True)
        o_ref[c * tk:(c + 1) * tk, :] = (acc_ref[c] / l).astype(o_ref.dtype)


def _attention(q, kn, kr, v, batch):
    H, n, _ = q.shape
    S = n // batch
    tq = min(S, TQ_ATTN)
    tk = min(tq, TK_ATTN)
    nq = S // tq
    return pl.pallas_call(
        _attn_kernel,
        out_shape=jax.ShapeDtypeStruct((n, H * MLA_V_DIM), BF16),
        grid=(batch, H, nq),
        in_specs=[pl.BlockSpec((1, tq, Q_HEAD_PAD), lambda b, h, i: (h, b * nq + i, 0)),
                  pl.BlockSpec((1, S, MLA_NOPE_DIM), lambda b, h, i: (h, b, 0)),
                  pl.BlockSpec((S, LANES), lambda b, h, i: (b, 0)),
                  pl.BlockSpec((1, S, MLA_V_DIM), lambda b, h, i: (h, b, 0))],
        out_specs=pl.BlockSpec((tq, MLA_V_DIM), lambda b, h, i: (b * nq + i, h)),
        scratch_shapes=[pltpu.VMEM((tq // tk, tk, LANES), F32), pltpu.VMEM((tq // tk, tk, LANES), F32),
                        pltpu.VMEM((tq // tk, tk, MLA_V_DIM), F32)],
        compiler_params=_params("parallel", "parallel", "arbitrary"),
        name="mla_attention",
    )(q, kn, kr, v)


def _layer_norm(x, g, b):
    mu = jnp.mean(x, axis=-1, keepdims=True)
    xc = x - mu
    var = jnp.mean(xc * xc, axis=-1, keepdims=True)
    return xc * lax.rsqrt(var + NORM_EPS) * g + b


def _first_index(hit, lane):
    return jnp.min(jnp.where(hit, lane, LANES), axis=-1, keepdims=True)


def _route(logits):
    lane = lax.broadcasted_iota(jnp.int32, logits.shape, 1)
    neg = -jnp.inf
    gl = jnp.where((lane >= N_EXPERTS) & (lane < N_EXPERTS + N_GROUPS), logits, neg)
    ge = jnp.exp(gl - jnp.max(gl, axis=-1, keepdims=True))
    gp = ge / jnp.sum(ge, axis=-1, keepdims=True)
    g_val = jnp.max(gp, axis=-1, keepdims=True)
    g_idx = _first_index(gp == g_val, lane) - N_EXPERTS
    el = jnp.where((lane // EXPERTS_PER_GROUP == g_idx) & (lane < N_EXPERTS), logits, neg)
    m1 = jnp.max(el, axis=-1, keepdims=True)
    i1 = _first_index(el == m1, lane)
    el2 = jnp.where(lane == i1, neg, el)
    m2 = jnp.max(el2, axis=-1, keepdims=True)
    i2 = _first_index(el2 == m2, lane)
    e2 = jnp.exp(m2 - m1)
    den = 1.0 + e2
    w1 = g_val * (1.0 / den)
    w2 = g_val * (e2 / den)
    return w1, w2, i1, i2


def _rows_to_slabs(ref, x, stride):
    m, d = x.shape
    for s in range(d // LANES):
        ref[pl.ds(s, m, stride=stride), :] = x[:, s * LANES:(s + 1) * LANES]


def _slabs_to_rows(ref, m, d, stride):
    return jnp.concatenate([ref[pl.ds(s, m, stride=stride), :] for s in range(d // LANES)], axis=-1)


def _outproj_kernel(ya_ref, yb_ref, yc_ref, x_ref, w_ref, g_ref, b_ref, wr_ref, br_ref,
                    x1_ref, x1s_ref, rt_ref, cnt_ref):
    @pl.when(pl.program_id(0) == 0)
    def _():
        cnt_ref[...] = jnp.zeros_like(cnt_ref)

    wa = SGU_WIDTH
    wb = SGU_WIDTH + HGRN_WIDTH
    d = x_ref.shape[1]
    slab = d // LANES
    ch = OUT_CHAIN_ROWS
    lane = lax.broadcasted_iota(jnp.int32, (ch, LANES), 1)
    row = lax.broadcasted_iota(jnp.int32, (ch, ch), 0)
    col = lax.broadcasted_iota(jnp.int32, (ch, ch), 1)
    before = (col < row).astype(BF16)
    carry = cnt_ref[...]
    for c in range(x_ref.shape[0] // ch):
        rs = slice(c * ch, (c + 1) * ch)
        mix = jnp.dot(ya_ref[rs, :], w_ref[0, :wa, :], preferred_element_type=F32)
        mix = mix + jnp.dot(yb_ref[rs, :], w_ref[0, wa:wb, :], preferred_element_type=F32)
        mix = mix + jnp.dot(yc_ref[rs, :], w_ref[0, wb:, :], preferred_element_type=F32)
        x1 = _layer_norm(DEEPNORM_ALPHA * x_ref[rs, :] + mix, g_ref[...], b_ref[...])
        x1_ref[rs, :] = x1
        _rows_to_slabs(x1s_ref.at[pl.ds(c * ch * slab, ch * slab), :], x1, slab)
        x_hi = x1.astype(BF16)
        x_lo = (x1 - x_hi.astype(F32)).astype(BF16)
        r = jnp.dot(x_hi, wr_ref[...], preferred_element_type=F32)
        logits = (r[:, :LANES] + r[:, LANES:] + jnp.dot(x_lo, wr_ref[:, :LANES], preferred_element_type=F32)
                  + br_ref[...])
        w1, w2, i1, i2 = _route(logits)

        hit1, hit2 = lane == i1, lane == i2
        oh1, oh2 = hit1.astype(F32), hit2.astype(F32)
        pre = jnp.dot(before, jnp.concatenate([oh1, oh2], axis=-1).astype(BF16), preferred_element_type=F32)
        tot1 = jnp.sum(oh1, axis=0, keepdims=True)
        tot2 = jnp.sum(oh2, axis=0, keepdims=True)
        rank1 = jnp.sum(jnp.where(hit1, pre[:, :LANES] + carry, 0.0), axis=-1, keepdims=True)
        rank2 = jnp.sum(jnp.where(hit2, pre[:, LANES:] + (carry + tot1), 0.0), axis=-1, keepdims=True)
        carry = carry + tot1 + tot2

        out = jnp.where(lane == 0, w1, 0.0)
        out = jnp.where(lane == 1, w2, out)
        out = jnp.where(lane == 2, i1.astype(F32), out)
        out = jnp.where(lane == 3, i2.astype(F32), out)
        out = jnp.where(lane == 4, rank1, out)
        out = jnp.where(lane == 5, rank2, out)
        rt_ref[rs, :] = out
    cnt_ref[...] = carry


def _outproj(ya, yb, yc, x, w_out, ln_g, ln_b, wr, br, l):
    n, d = x.shape
    tm = min(n, TM_OUT)
    assert tm % OUT_CHAIN_ROWS == 0
    slab = d // LANES
    full = lambda shape: pl.BlockSpec(shape, lambda i: (0,) * len(shape))
    rows = lambda w: pl.BlockSpec((tm, w), lambda i: (i, 0))
    wr_hi = wr.astype(BF16)
    wr_lo = (wr - wr_hi.astype(F32)).astype(BF16)
    return pl.pallas_call(
        _outproj_kernel,
        out_shape=(jax.ShapeDtypeStruct((n, d), F32), jax.ShapeDtypeStruct((n * slab, LANES), F32),
                   jax.ShapeDtypeStruct((n, LANES), F32), jax.ShapeDtypeStruct((1, LANES), F32)),
        grid=(n // tm,),
        in_specs=[rows(ya.shape[1]), rows(yb.shape[1]), rows(yc.shape[1]), rows(d),
                  pl.BlockSpec((1, d, d), lambda i: (l, 0, 0), pipeline_mode=pl.Buffered(1)), full((1, d)), full((1, d)),
                  full((d, 2 * LANES)), full((1, LANES))],
        out_specs=(rows(d), pl.BlockSpec((tm * slab, LANES), lambda i: (i, 0)), rows(LANES), full((1, LANES))),
        compiler_params=_params("arbitrary"),
        name="out_proj_ln_router",
    )(ya, yb, yc, x, w_out, ln_g[None, :], ln_b[None, :], jnp.concatenate([wr_hi, wr_lo], axis=1), br)


def _moe_kernel(src_ref, dst_ref, exp_ref, used_ref, x_hbm, wg_ref, wu_ref, wd_ref, y_hbm,
                xbuf0, xbuf1, xbuf2, xbuf3, ybuf0, ybuf1, ybuf2, ybuf3, wgb, wub, wdb, gsem, ssem):
    xbufs, ybufs = (xbuf0, xbuf1, xbuf2, xbuf3), (ybuf0, ybuf1, ybuf2, ybuf3)
    nx = len(xbufs)
    i = pl.program_id(0)
    nb = pl.num_programs(0) - 1
    d = wgb.shape[0]
    slab = d // LANES
    vs = MOE_VMEM_STRIDE
    rows = xbuf0.shape[0] // vs
    used = used_ref[0]
    n_real = y_hbm.shape[0] // slab - (nx + 1) * rows

    def gather_copy(tok, r, slot):
        src = x_hbm.at[pl.ds(pl.multiple_of(tok * slab, slab), slab), :]
        return pltpu.make_async_copy(src, xbufs[slot].at[pl.ds(r * vs, slab), :], gsem.at[slot])

    def scatter_copy(dst, r, slot):
        out = y_hbm.at[pl.ds(pl.multiple_of(dst * slab, slab), slab), :]
        return pltpu.make_async_copy(ybufs[slot].at[pl.ds(r * vs, slab), :], out, ssem.at[slot])

    def fill_copy(ybuf, region, sem):
        return pltpu.make_async_copy(ybuf.at[pl.ds(0, rows * slab), :],
                                     y_hbm.at[pl.ds((n_real + region * rows) * slab, rows * slab), :], sem)

    def wait_gather(slot):
        pltpu.make_async_copy(x_hbm.at[pl.ds(0, rows * slab), :], xbuf0.at[pl.ds(0, rows * slab), :],
                              gsem.at[slot]).wait()

    def wait_scatter(slot):
        fill_copy(ybuf0, 0, ssem.at[slot]).wait()

    @pl.when(i == 0)
    def _():
        ybuf3[...] = jnp.zeros_like(ybuf3)
        for t in range(nx - 1):
            fill_copy(ybuf3, 1 + t, ssem.at[t]).start()
        last = fill_copy(ybuf3, nx, ssem.at[nx])
        last.start()
        last.wait()
        for blk in range(2):
            def issue(r, carry):
                gather_copy(src_ref[jnp.minimum(blk, nb - 1) * rows + r], r, blk).start()
                return carry
            lax.fori_loop(0, rows, issue, 0, unroll=8)

    cur = jnp.minimum(i, nb - 1)
    @pl.when((i < used) & ((i == 0) | (exp_ref[cur] != exp_ref[jnp.maximum(cur - 1, 0)])))
    def _():
        wgb[...] = wg_ref[0, 0].astype(BF16)
        wub[...] = wu_ref[0, 0].astype(BF16)
        wdb[...] = wd_ref[0, 0].astype(BF16)

    def compute_block(phase):
        xs = ys = phase
        wait_gather(xs)
        wait_scatter(ys)
        nxt = jnp.minimum(i + 2, nb - 1) * rows
        prv = i * rows
        for r in range(rows):
            gather_copy(src_ref[nxt + r], r, (xs + 2) % nx).start()
            scatter_copy(dst_ref[prv + r], r, (ys - 1) % nx).start()
        xb = _slabs_to_rows(xbufs[xs], rows, d, vs).astype(BF16)
        hg = jnp.dot(xb, wgb[...], preferred_element_type=F32)
        hu = jnp.dot(xb, wub[...], preferred_element_type=F32)
        hh = (jax.nn.silu(hg) * hu).astype(BF16)
        _rows_to_slabs(ybufs[ys], jnp.dot(hh, wdb[...], preferred_element_type=F32), vs)

    def drain_block(ys):
        def issue(r, carry):
            scatter_copy(dst_ref[i * rows + r], r, ys).start()
            return carry
        lax.fori_loop(0, rows, issue, 0, unroll=8)
        wait_scatter(ys)

    for phase in range(nx):
        pl.when((i < used) & (i % nx == phase))(functools.partial(compute_block, phase))

    @pl.when(i == used)
    def _():
        wait_gather(i % nx)
        wait_gather((i + 1) % nx)
        for t in range(nx - 1):
            wait_scatter((i + t) % nx)

    for phase in range(nx):
        pl.when((i == used) & (i >= 1) & ((i - 1) % nx == phase))(functools.partial(drain_block, phase))


def _moe(slot_src, slot_dst, blk_exp, used, x1s, w_gate, w_up, w_down, n_out, l):
    d, de = w_gate.shape[2:]
    slab = d // LANES
    nb = blk_exp.shape[0]
    rows = slot_src.shape[0] // nb
    blk = lambda i, src, dst, exp, used: (l, exp[jnp.minimum(i, nb - 1)], 0, 0)
    return pl.pallas_call(
        _moe_kernel,
        out_shape=jax.ShapeDtypeStruct((n_out * slab, LANES), F32),
        grid_spec=pltpu.PrefetchScalarGridSpec(
            num_scalar_prefetch=4,
            grid=(nb + 1,),
            in_specs=[pl.BlockSpec(memory_space=pl.ANY),
                      pl.BlockSpec((1, 1, d, de), blk), pl.BlockSpec((1, 1, d, de), blk),
                      pl.BlockSpec((1, 1, de, d), blk)],
            out_specs=pl.BlockSpec(memory_space=pl.ANY),
            scratch_shapes=[pltpu.VMEM((rows * MOE_VMEM_STRIDE, LANES), F32)] * (2 * MOE_RING) + [
                            pltpu.VMEM((d, de), BF16), pltpu.VMEM((d, de), BF16), pltpu.VMEM((de, d), BF16),
                            pltpu.SemaphoreType.DMA((MOE_RING,)), pltpu.SemaphoreType.DMA((MOE_RING + 1,))]),
        compiler_params=_params("arbitrary"),
        name="moe_experts",
    )(slot_src, slot_dst, blk_exp, used, x1s, w_gate, w_up, w_down)


def _combine_kernel(y0_ref, y1_ref, x_ref, rt_ref, g_ref, b_ref, o_ref):
    tm, d = x_ref.shape
    slab = d // LANES
    rt = rt_ref[...]
    ffn = _slabs_to_rows(y0_ref, tm, d, slab) * rt[:, 0:1] + _slabs_to_rows(y1_ref, tm, d, slab) * rt[:, 1:2]
    o_ref[...] = _layer_norm(DEEPNORM_ALPHA * x_ref[...] + ffn, g_ref[...], b_ref[...])


def _combine(y_tok, x1, route, ln_g, ln_b):
    n, d = x1.shape
    tm = min(n, TM_COMB)
    slab = d // LANES
    return pl.pallas_call(
        _combine_kernel,
        out_shape=jax.ShapeDtypeStruct((n, d), F32),
        grid=(n // tm,),
        in_specs=[pl.BlockSpec((tm * slab, LANES), lambda i: (i, 0)),
                  pl.BlockSpec((tm * slab, LANES), lambda i: (i + n // tm, 0)),
                  pl.BlockSpec((tm, d), lambda i: (i, 0)),
                  pl.BlockSpec((tm, LANES), lambda i: (i, 0)),
                  pl.BlockSpec((1, d), lambda i: (0, 0)),
                  pl.BlockSpec((1, d), lambda i: (0, 0))],
        out_specs=pl.BlockSpec((tm, d), lambda i: (i, 0)),
        compiler_params=_params("parallel"),
        name="moe_combine_ln",
    )(y_tok, y_tok, x1, route, ln_g[None, :], ln_b[None, :])


def _slots_kernel(dest_ref, src0_hbm, dst0_hbm, src_ref, dst_ref, sem):
    fills = [pltpu.make_async_copy(src0_hbm, src_ref, sem.at[0]), pltpu.make_async_copy(dst0_hbm, dst_ref, sem.at[1])]
    for f in fills:
        f.start()
    for f in fills:
        f.wait()
    nk = dest_ref.shape[0]
    n = nk // TOP_K
    lead = dst_ref.shape[0] - src_ref.shape[0]

    batch = 16

    def place(b, carry):
        t0 = b * (batch // TOP_K)
        slots = [dest_ref[b * batch + u] for u in range(batch)]
        for u, d in enumerate(slots):
            t = t0 + u // TOP_K
            src_ref[d] = t
            dst_ref[lead + d] = (u % TOP_K) * n + t
        return carry

    lax.fori_loop(0, nk // batch, place, 0)


def _slot_tables(dest, src0, dst0):
    smem = pl.BlockSpec(memory_space=pltpu.SMEM)
    hbm = pl.BlockSpec(memory_space=pl.ANY)
    return pl.pallas_call(
        _slots_kernel,
        out_shape=(jax.ShapeDtypeStruct(src0.shape, jnp.int32), jax.ShapeDtypeStruct(dst0.shape, jnp.int32)),
        grid_spec=pltpu.PrefetchScalarGridSpec(
            num_scalar_prefetch=1, grid=(1,), in_specs=[hbm, hbm], out_specs=(smem, smem),
            scratch_shapes=[pltpu.SemaphoreType.DMA((2,))]),
        name="moe_slot_tables",
    )(dest, src0, dst0)


def _dispatch_plan(route, counts, rows):
    n = route.shape[0]
    nk = n * TOP_K
    nb = (nk + rows - 1) // rows + N_EXPERTS
    counts = counts[0, :N_EXPERTS].astype(jnp.int32)
    padded = ((counts + rows - 1) // rows) * rows
    pad_end = jnp.cumsum(padded)
    pad_start = pad_end - padded
    expert = route[:, 2:2 + TOP_K].astype(jnp.int32)
    rank = route[:, 2 + TOP_K:2 + 2 * TOP_K].astype(jnp.int32)
    dest = (pad_start[expert] + rank).reshape(nk)
    slot = jnp.arange(nb * rows, dtype=jnp.int32)
    dump = nk + rows + ((slot // rows) % MOE_RING) * rows + slot % rows
    slot_src, slot_dst = _slot_tables(dest, jnp.zeros((nb * rows,), jnp.int32),
                                      jnp.concatenate([nk + jnp.arange(rows, dtype=jnp.int32), dump]))
    blk_exp = jnp.minimum(
        jnp.sum((pad_end[None, :] <= (jnp.arange(nb, dtype=jnp.int32) * rows)[:, None]).astype(jnp.int32), axis=1),
        N_EXPERTS - 1).astype(jnp.int32)
    used = (pad_end[-1] // rows).astype(jnp.int32).reshape(1)
    return slot_src, slot_dst, blk_exp, used, nk + (MOE_RING + 1) * rows


def _pad_q_heads(w_uq):
    lead = w_uq.shape[:-1]
    w = w_uq.reshape(*lead, MLA_HEADS, MLA_QK_DIM)
    w = jnp.pad(w, [(0, 0)] * len(lead) + [(0, 0), (0, Q_HEAD_PAD - MLA_QK_DIM)])
    return w.reshape(*lead, MLA_HEADS * Q_HEAD_PAD)


def kernel(x, positions, w_in, sgu_ln_g, sgu_ln_b, sgu_ws, sgu_b, hgrn_lb_logits, hgrn_norm_g, mla_qn_g, mla_w_uq, mla_kvn_g, mla_w_ukv, w_out, ln1_g, ln1_b, router_group_w, router_group_b, router_expert_w, router_expert_b, expert_w_gate, expert_w_up, expert_w_down, ln2_g, ln2_b):
    B, S, D = x.shape
    n = B * S
    depth = w_in.shape[0]
    tab_c, tab_s = _rope_tables(positions)
    lb_cum = jnp.cumsum(jax.nn.softmax(hgrn_lb_logits.astype(F32), axis=0), axis=0)
    lower_bounds = lb_cum - lb_cum[0:1]
    xt = x.reshape(n, D)
    w_in_b = jnp.pad(w_in, ((0, 0), (0, 0), (0, D_IN_PAD - D_IN))).astype(BF16)
    w_uq_b = _pad_q_heads(mla_w_uq).astype(BF16)
    w_ukv_b = mla_w_ukv.astype(BF16)
    w_out_b = w_out.astype(BF16)
    for l in range(depth):
        h = _inproj(xt, w_in_b, l)
        y_a = _sgu(h, sgu_ln_g[l], sgu_ln_b[l], sgu_ws[l], sgu_b[l])
        y_b = _hgrn(h, lower_bounds[l], hgrn_norm_g[l], B)
        q, kn, v, kr = _mla_up(h, tab_c, tab_s, mla_qn_g[l], w_uq_b, mla_kvn_g[l], w_ukv_b, l)
        y_c = _attention(q, kn, kr, v, B)
        wr = jnp.pad(jnp.concatenate([router_expert_w[l], router_group_w[l]], axis=1),
                     ((0, 0), (0, LANES - N_EXPERTS - N_GROUPS)))
        br = jnp.pad(jnp.concatenate([router_expert_b[l], router_group_b[l]]), (0, LANES - N_EXPERTS - N_GROUPS))
        x1, x1s, route, counts = _outproj(y_a, y_b, y_c, xt, w_out_b, ln1_g[l], ln1_b[l], wr, br[None, :], l)
        slot_src, slot_dst, blk_exp, used, n_out = _dispatch_plan(route, counts, MOE_ROWS)
        y_tok = _moe(slot_src, slot_dst, blk_exp, used, x1s, expert_w_gate, expert_w_up, expert_w_down, n_out, l)
        xt = _combine(y_tok, x1, route, ln2_g[l], ln2_b[l])
    return xt.reshape(B, S, D)
```

```python
import functools

import jax
import jax.numpy as jnp
from jax import lax
from jax.experimental import pallas as pl
from jax.experimental.pallas import tpu as pltpu

D_MODEL = 2048
DEPTH = 4
CHUNK = 128
SGU_WIDTH = D_MODEL // 4
SGU_GROUPS = 4
SGU_GROUP_DIM = SGU_WIDTH // SGU_GROUPS
HGRN_WIDTH = D_MODEL // 4
HGRN_HEADS = 4
HGRN_HEAD_DIM = HGRN_WIDTH // HGRN_HEADS
MLA_HEADS = 8
MLA_V_DIM = (D_MODEL - SGU_WIDTH - HGRN_WIDTH) // MLA_HEADS
MLA_NOPE_DIM = 128
MLA_ROPE_DIM = 64
MLA_QK_DIM = MLA_NOPE_DIM + MLA_ROPE_DIM
Q_LORA_RANK = D_MODEL // 4
KV_LORA_RANK = D_MODEL // 4
ROPE_THETA = 10000.0
N_GROUPS = 4
EXPERTS_PER_GROUP = 8
N_EXPERTS = N_GROUPS * EXPERTS_PER_GROUP
TOP_K = 2
D_EXPERT = D_MODEL // 4
NORM_EPS = 1e-5
DEEPNORM_ALPHA = (2 * DEPTH) ** 0.25
D_IN = SGU_WIDTH * 2 + HGRN_WIDTH * 4 + Q_LORA_RANK + KV_LORA_RANK + MLA_ROPE_DIM

LANES = 128
D_IN_PAD = ((D_IN + LANES - 1) // LANES) * LANES
Q_HEAD_PAD = 2 * LANES
VMEM_LIMIT = 56 * 1024 * 1024

TM_INPROJ = 1024
TN_INPROJ = D_IN_PAD // 3
TM_SGU = 1024
HGRN_CHUNKS_PER_STEP = 8
TM_MLA = 512
TQ_ATTN = 4096
TK_ATTN = 1024
TM_OUT = 512
OUT_CHAIN_ROWS = 512
MOE_ROWS = 256
MOE_RING = 4
MOE_VMEM_STRIDE = 20
TM_COMB = 512

BF16 = jnp.bfloat16
F32 = jnp.float32
LOG2_E = 1.4426950408889634


def _params(*sem):
    return pltpu.CompilerParams(dimension_semantics=sem, vmem_limit_bytes=VMEM_LIMIT)


def _rope_kernel(pos_ref, inv_ref, sgn_ref, c_ref, s_ref):
    ang = pos_ref[...].astype(F32) * inv_ref[...]
    live = sgn_ref[...] != 0.0
    c_ref[...] = jnp.where(live, jnp.cos(ang), 0.0)
    s_ref[...] = jnp.sin(ang) * sgn_ref[...]


def _rope_tables(positions):
    n = positions.size
    tm = min(n, 1024)
    half = MLA_ROPE_DIM // 2
    inv = 1.0 / (ROPE_THETA ** (jnp.arange(0, MLA_ROPE_DIM, 2, dtype=F32) / MLA_ROPE_DIM))
    zeros = jnp.zeros((LANES - 2 * half,), F32)
    inv_t = jnp.concatenate([inv, inv, zeros])[None, :]
    sgn_t = jnp.concatenate([-jnp.ones((half,), F32), jnp.ones((half,), F32), zeros])[None, :]
    row = pl.BlockSpec((1, LANES), lambda i: (0, 0))
    return pl.pallas_call(
        _rope_kernel,
        out_shape=(jax.ShapeDtypeStruct((n, LANES), F32),) * 2,
        grid=(n // tm,),
        in_specs=[pl.BlockSpec((tm, 1), lambda i: (i, 0)), row, row],
        out_specs=(pl.BlockSpec((tm, LANES), lambda i: (i, 0)),) * 2,
        compiler_params=_params("parallel"),
        name="rope_tables",
    )(positions.reshape(n, 1), inv_t, sgn_t)


def _inproj_kernel(x_ref, w_ref, o_ref):
    o_ref[...] = jnp.dot(x_ref[...].astype(BF16), w_ref[0], preferred_element_type=F32)


def _inproj(x, w, l):
    n, d = x.shape
    tm = min(n, TM_INPROJ)
    return pl.pallas_call(
        _inproj_kernel,
        out_shape=jax.ShapeDtypeStruct((n, D_IN_PAD), F32),
        grid=(D_IN_PAD // TN_INPROJ, n // tm),
        in_specs=[pl.BlockSpec((tm, d), lambda j, i: (i, 0)),
                  pl.BlockSpec((1, d, TN_INPROJ), lambda j, i: (l, 0, j))],
        out_specs=pl.BlockSpec((tm, TN_INPROJ), lambda j, i: (i, j)),
        compiler_params=_params("parallel", "parallel"),
        name="in_proj",
    )(x, w)


def _sgu_kernel(u_ref, v_ref, g_ref, b_ref, ws_ref, bs_ref, o_ref):
    u = jax.nn.gelu(u_ref[...])
    v = jax.nn.gelu(v_ref[...])
    mu = jnp.mean(v, axis=-1, keepdims=True)
    vc = v - mu
    var = jnp.mean(vc * vc, axis=-1, keepdims=True)
    vn = (vc * lax.rsqrt(var + NORM_EPS) * g_ref[...] + b_ref[...]).astype(BF16)
    row = lax.broadcasted_iota(jnp.int32, (CHUNK, CHUNK), 0)
    col = lax.broadcasted_iota(jnp.int32, (CHUNK, CHUNK), 1)
    causal = row >= col
    bs = bs_ref[...]
    for g in range(SGU_GROUPS):
        w = jnp.where(causal, ws_ref[g], 0.0).astype(BF16)
        cs = slice(g * SGU_GROUP_DIM, (g + 1) * SGU_GROUP_DIM)
        for c in range(u.shape[0] // CHUNK):
            rs = slice(c * CHUNK, (c + 1) * CHUNK)
            mixed = jnp.dot(w, vn[rs, cs], preferred_element_type=F32) + bs[:, g:g + 1]
            o_ref[rs, cs] = (u[rs, cs] * mixed).astype(o_ref.dtype)


def _sgu(h, ln_g, ln_b, ws, bs):
    n = h.shape[0]
    tm = min(n, TM_SGU)
    full = lambda shape: pl.BlockSpec(shape, lambda i: (0,) * len(shape))
    return pl.pallas_call(
        _sgu_kernel,
        out_shape=jax.ShapeDtypeStruct((n, SGU_WIDTH), BF16),
        grid=(n // tm,),
        in_specs=[pl.BlockSpec((tm, SGU_WIDTH), lambda i: (i, 0)),
                  pl.BlockSpec((tm, SGU_WIDTH), lambda i: (i, 1)),
                  full((1, SGU_WIDTH)), full((1, SGU_WIDTH)),
                  full((SGU_GROUPS, CHUNK, CHUNK)), full((CHUNK, SGU_GROUPS))],
        out_specs=pl.BlockSpec((tm, SGU_WIDTH), lambda i: (i, 0)),
        compiler_params=_params("parallel"),
        name="sgu",
    )(h, h, ln_g[None, :], ln_b[None, :], ws, bs.T)


def _dot_nt(a, b):
    return lax.dot_general(a, b, (((1,), (1,)), ((), ())), preferred_element_type=F32)


def _hgrn_kernel(q_ref, f_ref, i_ref, g_ref, la_ref, lc_ref, om_ref, ng_ref, o_ref, state_ref):
    @pl.when(pl.program_id(1) == 0)
    def _():
        state_ref[...] = jnp.zeros_like(state_ref)

    T, K = CHUNK, HGRN_HEAD_DIM
    row = lax.broadcasted_iota(jnp.int32, (T, T), 0)
    col = lax.broadcasted_iota(jnp.int32, (T, T), 1)
    tril = (row >= col).astype(F32)
    SUB = 8
    wide = {}
    for c, h in [(c, h) for c in range(q_ref.shape[0] // T) for h in range(HGRN_HEADS)]:
        rs = slice(c * T, (c + 1) * T)
        cs = slice(h * K, (h + 1) * K)
        if c not in wide:
            fx = f_ref[rs, :]
            log_f = jnp.logaddexp(la_ref[...], lc_ref[...] + jax.nn.log_sigmoid(fx))
            G_all = jnp.dot(tril, log_f, preferred_element_type=F32, precision=lax.Precision.HIGHEST)
            wide = {c: (G_all, om_ref[...] * jax.nn.sigmoid(-fx), jax.nn.silu(q_ref[rs, :]))}
        G, k, q = (a[:, cs] for a in wide[c])
        v = i_ref[rs, cs]
        A = jnp.zeros((T, T), F32)
        m = T // 2
        while m >= SUB:
            blk = 2 * m
            g_mid = jnp.concatenate(
                [jnp.broadcast_to(G[b * blk + m - 1:b * blk + m, :], (blk, K)) for b in range(T // blk)], axis=0)
            lower = (row % blk) >= m
            qs = jnp.where(lower, q * jnp.exp(G - g_mid), 0.0).astype(BF16)
            ks = jnp.where(lower, 0.0, k * jnp.exp(g_mid - G)).astype(BF16)
            same = (row // blk) == (col // blk)
            A = A + jnp.where(same, _dot_nt(qs, ks), 0.0)
            m //= 2
        rmod = row % SUB
        k3, G3 = k.reshape(T // SUB, SUB, K), G.reshape(T // SUB, SUB, K)
        for d in range(SUB):
            if d == 0:
                prod = q * k
            else:
                kd = pltpu.roll(k3, d, axis=1).reshape(T, K)
                gd = pltpu.roll(G3, d, axis=1).reshape(T, K)
                prod = jnp.where(rmod >= d, q * kd * jnp.exp(G - gd), 0.0)
            A = A + jnp.where(col == row - d, jnp.sum(prod, axis=-1, keepdims=True), 0.0)

        st = state_ref[h]
        o = jnp.dot(A.astype(BF16), v.astype(BF16), preferred_element_type=F32)
        o = o + _dot_nt((q * jnp.exp(G)).astype(BF16), st.astype(BF16))
        g_last = G[T - 1:T, :]
        k_dec = (k * jnp.exp(g_last - G)).astype(BF16)
        state_ref[h] = jnp.exp(g_last) * st + jnp.dot(v.T.astype(BF16), k_dec, preferred_element_type=F32)

        ms = jnp.mean(o * o, axis=-1, keepdims=True)
        o = o * lax.rsqrt(ms + NORM_EPS) * ng_ref[:, cs]
        o_ref[rs, cs] = (o * jax.nn.silu(g_ref[rs, cs])).astype(o_ref.dtype)


def _hgrn(h, lb, norm_g, batch):
    n = h.shape[0]
    tb = CHUNK * HGRN_CHUNKS_PER_STEP
    nc = n // batch // tb
    W = HGRN_WIDTH
    col = lambda j: pl.BlockSpec((tb, W), lambda b, c: (b * nc + c, j))
    vec = pl.BlockSpec((1, W), lambda b, c: (0, 0))
    base = 2 * SGU_WIDTH // W
    return pl.pallas_call(
        _hgrn_kernel,
        out_shape=jax.ShapeDtypeStruct((n, W), BF16),
        grid=(batch, nc),
        in_specs=[col(base), col(base + 1), col(base + 2), col(base + 3), vec, vec, vec, vec],
        out_specs=pl.BlockSpec((tb, W), lambda b, c: (b * nc + c, 0)),
        scratch_shapes=[pltpu.VMEM((HGRN_HEADS, HGRN_HEAD_DIM, HGRN_HEAD_DIM), F32)],
        compiler_params=_params("parallel", "arbitrary"),
        name="hgrn2",
    )(h, h, h, h, jnp.log(lb)[None, :], jnp.log1p(-lb)[None, :], (1.0 - lb)[None, :], norm_g[None, :])


def _rms(x, g):
    ms = jnp.mean(x * x, axis=-1, keepdims=True)
    return x * lax.rsqrt(ms + NORM_EPS) * g


def _rope(x, c, s):
    half = MLA_ROPE_DIM // 2
    lane = lax.broadcasted_iota(jnp.int32, x.shape, 1)
    swapped = jnp.where(lane < half, pltpu.roll(x, LANES - half, axis=1), pltpu.roll(x, half, axis=1))
    return x * c + swapped * s


def _mla_up_kernel(cq_ref, ckv_ref, kr_ref, c_ref, s_ref, qg_ref, kvg_ref, wq_ref, wkv_ref,
                   q_ref, kn_ref, v_ref, kro_ref):
    c, s = c_ref[...], s_ref[...]
    scale = MLA_QK_DIM ** -0.5 * LOG2_E
    q = jnp.dot(_rms(cq_ref[...], qg_ref[...]).astype(BF16), wq_ref[0], preferred_element_type=F32)
    kv = jnp.dot(_rms(ckv_ref[...], kvg_ref[...]).astype(BF16), wkv_ref[0], preferred_element_type=F32)
    for h in range(MLA_HEADS):
        base = h * Q_HEAD_PAD
        q_ref[h, :, :LANES] = (q[:, base:base + LANES] * scale).astype(BF16)
        q_ref[h, :, LANES:] = (_rope(q[:, base + LANES:base + 2 * LANES], c, s) * scale).astype(BF16)
        kbase = h * (MLA_NOPE_DIM + MLA_V_DIM)
        kn_ref[h] = kv[:, kbase:kbase + MLA_NOPE_DIM].astype(BF16)
        v_ref[h] = kv[:, kbase + MLA_NOPE_DIM:kbase + MLA_NOPE_DIM + MLA_V_DIM].astype(BF16)
    kro_ref[...] = _rope(kr_ref[...], c, s).astype(BF16)


def _mla_up(h, tab_c, tab_s, qn_g, w_uq, kvn_g, w_ukv, l):
    n = h.shape[0]
    tm = min(n, TM_MLA)
    H = MLA_HEADS
    cq_blk = (2 * SGU_WIDTH + 4 * HGRN_WIDTH) // Q_LORA_RANK
    kr_blk = (2 * SGU_WIDTH + 4 * HGRN_WIDTH + Q_LORA_RANK + KV_LORA_RANK) // LANES
    full = lambda shape: pl.BlockSpec(shape, lambda i: (0,) * len(shape))
    heads = lambda w: pl.BlockSpec((H, tm, w), lambda i: (0, i, 0))
    return pl.pallas_call(
        _mla_up_kernel,
        out_shape=(jax.ShapeDtypeStruct((H, n, Q_HEAD_PAD), BF16),
                   jax.ShapeDtypeStruct((H, n, MLA_NOPE_DIM), BF16),
                   jax.ShapeDtypeStruct((H, n, MLA_V_DIM), BF16),
                   jax.ShapeDtypeStruct((n, LANES), BF16)),
        grid=(n // tm,),
        in_specs=[pl.BlockSpec((tm, Q_LORA_RANK), lambda i: (i, cq_blk)),
                  pl.BlockSpec((tm, KV_LORA_RANK), lambda i: (i, cq_blk + 1)),
                  pl.BlockSpec((tm, LANES), lambda i: (i, kr_blk)),
                  pl.BlockSpec((tm, LANES), lambda i: (i, 0)),
                  pl.BlockSpec((tm, LANES), lambda i: (i, 0)),
                  full((1, Q_LORA_RANK)), full((1, KV_LORA_RANK)),
                  pl.BlockSpec((1, Q_LORA_RANK, H * Q_HEAD_PAD), lambda i: (l, 0, 0)),
                  pl.BlockSpec((1, KV_LORA_RANK, H * (MLA_NOPE_DIM + MLA_V_DIM)), lambda i: (l, 0, 0))],
        out_specs=(heads(Q_HEAD_PAD), heads(MLA_NOPE_DIM), heads(MLA_V_DIM),
                   pl.BlockSpec((tm, LANES), lambda i: (i, 0))),
        compiler_params=_params("parallel"),
        name="mla_up",
    )(h, h, h, tab_c, tab_s, qn_g[None, :], kvn_g[None, :], w_uq, w_ukv)


def _attn_kernel(q_ref, kn_ref, kr_ref, v_ref, o_ref, m_ref, l_ref, acc_ref):
    i = pl.program_id(2)
    nch, tk, _ = acc_ref.shape
    m_ref[...] = jnp.full_like(m_ref, -jnp.inf)
    l_ref[...] = jnp.zeros_like(l_ref)
    acc_ref[...] = jnp.zeros_like(acc_ref)

    def chain(c, j, masked):
        start = pl.multiple_of(j * tk, tk)
        q = q_ref[0, c * tk:(c + 1) * tk, :]
        k = jnp.concatenate([kn_ref[0, pl.ds(start, tk), :], kr_ref[pl.ds(start, tk), :]], axis=-1)
        s = _dot_nt(q, k)
        if masked:
            row = lax.broadcasted_iota(jnp.int32, s.shape, 0)
            col = lax.broadcasted_iota(jnp.int32, s.shape, 1)
            s = jnp.where(col <= row, s, -jnp.inf)
        chunks = [s[:, t * LANES:(t + 1) * LANES] for t in range(tk // LANES)]
        smax = functools.reduce(jnp.maximum, chunks)
        m_prev = m_ref[c]
        m_new = jnp.maximum(m_prev, jnp.max(smax, axis=-1, keepdims=True))
        alpha = jnp.exp2(m_prev - m_new)
        ps = [jnp.exp2(ch - m_new) for ch in chunks]
        l_ref[c] = alpha * l_ref[c] + functools.reduce(jnp.add, ps)
        p = jnp.concatenate([x.astype(BF16) for x in ps], axis=-1)
        acc_ref[c] = alpha * acc_ref[c] + jnp.dot(p, v_ref[0, pl.ds(start, tk), :], preferred_element_type=F32)
        m_ref[c] = m_new

    def body(j, carry):
        for c in range(nch):
            chain(c, j, masked=False)
        return carry

    lax.fori_loop(0, nch * i, body, 0)
    for t in range(nch):
        for c in range(t, nch):
            chain(c, nch * i + t, masked=(c == t))
    for c in range(nch):
        l = jnp.sum(l_ref[c], axis=-1, keepdims=True)
        o_ref[c * tk:(c + 1) * tk, :] = (acc_ref[c] / l).astype(o_ref.dtype)


def _attention(q, kn, kr, v, batch):
    H, n, _ = q.shape
    S = n // batch
    tq = min(S, TQ_ATTN)
    tk = min(tq, TK_ATTN)
    nq = S // tq
    return pl.pallas_call(
        _attn_kernel,
        out_shape=jax.ShapeDtypeStruct((n, H * MLA_V_DIM), BF16),
        grid=(batch, H, nq),
        in_specs=[pl.BlockSpec((1, tq, Q_HEAD_PAD), lambda b, h, i: (h, b * nq + i, 0)),
                  pl.BlockSpec((1, S, MLA_NOPE_DIM), lambda b, h, i: (h, b, 0)),
                  pl.BlockSpec((S, LANES), lambda b, h, i: (b, 0)),
                  pl.BlockSpec((1, S, MLA_V_DIM), lambda b, h, i: (h, b, 0))],
        out_specs=pl.BlockSpec((tq, MLA_V_DIM), lambda b, h, i: (b * nq + i, h)),
        scratch_shapes=[pltpu.VMEM((tq // tk, tk, LANES), F32), pltpu.VMEM((tq // tk, tk, LANES), F32),
                        pltpu.VMEM((tq // tk, tk, MLA_V_DIM), F32)],
        compiler_params=_params("parallel", "parallel", "arbitrary"),
        name="mla_attention",
    )(q, kn, kr, v)


def _layer_norm(x, g, b):
    mu = jnp.mean(x, axis=-1, keepdims=True)
    xc = x - mu
    var = jnp.mean(xc * xc, axis=-1, keepdims=True)
    return xc * lax.rsqrt(var + NORM_EPS) * g + b


def _first_index(hit, lane):
    return jnp.min(jnp.where(hit, lane, LANES), axis=-1, keepdims=True)


def _route(logits):
    lane = lax.broadcasted_iota(jnp.int32, logits.shape, 1)
    neg = -jnp.inf
    gl = jnp.where((lane >= N_EXPERTS) & (lane < N_EXPERTS + N_GROUPS), logits, neg)
    ge = jnp.exp(gl - jnp.max(gl, axis=-1, keepdims=True))
    gp = ge / jnp.sum(ge, axis=-1, keepdims=True)
    g_val = jnp.max(gp, axis=-1, keepdims=True)
    g_idx = _first_index(gp == g_val, lane) - N_EXPERTS
    el = jnp.where((lane // EXPERTS_PER_GROUP == g_idx) & (lane < N_EXPERTS), logits, neg)
    m1 = jnp.max(el, axis=-1, keepdims=True)
    i1 = _first_index(el == m1, lane)
    el2 = jnp.where(lane == i1, neg, el)
    m2 = jnp.max(el2, axis=-1, keepdims=True)
    i2 = _first_index(el2 == m2, lane)
    e2 = jnp.exp(m2 - m1)
    den = 1.0 + e2
    w1 = g_val * (1.0 / den)
    w2 = g_val * (e2 / den)
    return w1, w2, i1, i2


def _rows_to_slabs(ref, x, stride):
    m, d = x.shape
    for s in range(d // LANES):
        ref[pl.ds(s, m, stride=stride), :] = x[:, s * LANES:(s + 1) * LANES]


def _slabs_to_rows(ref, m, d, stride):
    return jnp.concatenate([ref[pl.ds(s, m, stride=stride), :] for s in range(d // LANES)], axis=-1)


def _outproj_kernel(ya_ref, yb_ref, yc_ref, x_ref, w_ref, g_ref, b_ref, wr_ref, br_ref,
                    x1_ref, x1s_ref, rt_ref, cnt_ref):
    @pl.when(pl.program_id(0) == 0)
    def _():
        cnt_ref[...] = jnp.zeros_like(cnt_ref)

    wa = SGU_WIDTH
    wb = SGU_WIDTH + HGRN_WIDTH
    d = x_ref.shape[1]
    slab = d // LANES
    ch = OUT_CHAIN_ROWS
    lane = lax.broadcasted_iota(jnp.int32, (ch, LANES), 1)
    row = lax.broadcasted_iota(jnp.int32, (ch, ch), 0)
    col = lax.broadcasted_iota(jnp.int32, (ch, ch), 1)
    before = (col < row).astype(BF16)
    carry = cnt_ref[...]
    for c in range(x_ref.shape[0] // ch):
        rs = slice(c * ch, (c + 1) * ch)
        mix = jnp.dot(ya_ref[rs, :], w_ref[0, :wa, :], preferred_element_type=F32)
        mix = mix + jnp.dot(yb_ref[rs, :], w_ref[0, wa:wb, :], preferred_element_type=F32)
        mix = mix + jnp.dot(yc_ref[rs, :], w_ref[0, wb:, :], preferred_element_type=F32)
        x1 = _layer_norm(DEEPNORM_ALPHA * x_ref[rs, :] + mix, g_ref[...], b_ref[...])
        x1_ref[rs, :] = x1
        _rows_to_slabs(x1s_ref.at[pl.ds(c * ch * slab, ch * slab), :], x1, slab)
        x_hi = x1.astype(BF16)
        x_lo = (x1 - x_hi.astype(F32)).astype(BF16)
        r = jnp.dot(x_hi, wr_ref[...], preferred_element_type=F32)
        logits = (r[:, :LANES] + r[:, LANES:] + jnp.dot(x_lo, wr_ref[:, :LANES], preferred_element_type=F32)
                  + br_ref[...])
        w1, w2, i1, i2 = _route(logits)

        hit1, hit2 = lane == i1, lane == i2
        oh1, oh2 = hit1.astype(F32), hit2.astype(F32)
        pre = jnp.dot(before, jnp.concatenate([oh1, oh2], axis=-1).astype(BF16), preferred_element_type=F32)
        tot1 = jnp.sum(oh1, axis=0, keepdims=True)
        tot2 = jnp.sum(oh2, axis=0, keepdims=True)
        rank1 = jnp.sum(jnp.where(hit1, pre[:, :LANES] + carry, 0.0), axis=-1, keepdims=True)
        rank2 = jnp.sum(jnp.where(hit2, pre[:, LANES:] + (carry + tot1), 0.0), axis=-1, keepdims=True)
        carry = carry + tot1 + tot2

        out = jnp.where(lane == 0, w1, 0.0)
        out = jnp.where(lane == 1, w2, out)
        out = jnp.where(lane == 2, i1.astype(F32), out)
        out = jnp.where(lane == 3, i2.astype(F32), out)
        out = jnp.where(lane == 4, rank1, out)
        out = jnp.where(lane == 5, rank2, out)
        rt_ref[rs, :] = out
    cnt_ref[...] = carry


def _outproj(ya, yb, yc, x, w_out, ln_g, ln_b, wr, br, l):
    n, d = x.shape
    tm = min(n, TM_OUT)
    assert tm % OUT_CHAIN_ROWS == 0
    slab = d // LANES
    full = lambda shape: pl.BlockSpec(shape, lambda i: (0,) * len(shape))
    rows = lambda w: pl.BlockSpec((tm, w), lambda i: (i, 0))
    wr_hi = wr.astype(BF16)
    wr_lo = (wr - wr_hi.astype(F32)).astype(BF16)
    return pl.pallas_call(
        _outproj_kernel,
        out_shape=(jax.ShapeDtypeStruct((n, d), F32), jax.ShapeDtypeStruct((n * slab, LANES), F32),
                   jax.ShapeDtypeStruct((n, LANES), F32), jax.ShapeDtypeStruct((1, LANES), F32)),
        grid=(n // tm,),
        in_specs=[rows(ya.shape[1]), rows(yb.shape[1]), rows(yc.shape[1]), rows(d),
                  pl.BlockSpec((1, d, d), lambda i: (l, 0, 0), pipeline_mode=pl.Buffered(1)), full((1, d)), full((1, d)),
                  full((d, 2 * LANES)), full((1, LANES))],
        out_specs=(rows(d), pl.BlockSpec((tm * slab, LANES), lambda i: (i, 0)), rows(LANES), full((1, LANES))),
        compiler_params=_params("arbitrary"),
        name="out_proj_ln_router",
    )(ya, yb, yc, x, w_out, ln_g[None, :], ln_b[None, :], jnp.concatenate([wr_hi, wr_lo], axis=1), br)


def _moe_kernel(src_ref, dst_ref, exp_ref, used_ref, x_hbm, wg_ref, wu_ref, wd_ref, y_hbm,
                xbuf0, xbuf1, xbuf2, xbuf3, ybuf0, ybuf1, ybuf2, ybuf3, wgb, wub, wdb, gsem, ssem):
    xbufs, ybufs = (xbuf0, xbuf1, xbuf2, xbuf3), (ybuf0, ybuf1, ybuf2, ybuf3)
    nx = len(xbufs)
    i = pl.program_id(0)
    nb = pl.num_programs(0) - 1
    d = wgb.shape[0]
    slab = d // LANES
    vs = MOE_VMEM_STRIDE
    rows = xbuf0.shape[0] // vs
    used = used_ref[0]
    n_real = y_hbm.shape[0] // slab - (nx + 1) * rows

    def gather_copy(tok, r, slot):
        src = x_hbm.at[pl.ds(pl.multiple_of(tok * slab, slab), slab), :]
        return pltpu.make_async_copy(src, xbufs[slot].at[pl.ds(r * vs, slab), :], gsem.at[slot])

    def scatter_copy(dst, r, slot):
        out = y_hbm.at[pl.ds(pl.multiple_of(dst * slab, slab), slab), :]
        return pltpu.make_async_copy(ybufs[slot].at[pl.ds(r * vs, slab), :], out, ssem.at[slot])

    def fill_copy(ybuf, region, sem):
        return pltpu.make_async_copy(ybuf.at[pl.ds(0, rows * slab), :],
                                     y_hbm.at[pl.ds((n_real + region * rows) * slab, rows * slab), :], sem)

    def wait_gather(slot):
        pltpu.make_async_copy(x_hbm.at[pl.ds(0, rows * slab), :], xbuf0.at[pl.ds(0, rows * slab), :],
                              gsem.at[slot]).wait()

    def wait_scatter(slot):
        fill_copy(ybuf0, 0, ssem.at[slot]).wait()

    @pl.when(i == 0)
    def _():
        ybuf3[...] = jnp.zeros_like(ybuf3)
        for t in range(nx - 1):
            fill_copy(ybuf3, 1 + t, ssem.at[t]).start()
        last = fill_copy(ybuf3, nx, ssem.at[nx])
        last.start()
        last.wait()
        for blk in range(2):
            def issue(r, carry):
                gather_copy(src_ref[jnp.minimum(blk, nb - 1) * rows + r], r, blk).start()
                return carry
            lax.fori_loop(0, rows, issue, 0, unroll=8)

    cur = jnp.minimum(i, nb - 1)
    @pl.when((i < used) & ((i == 0) | (exp_ref[cur] != exp_ref[jnp.maximum(cur - 1, 0)])))
    def _():
        wgb[...] = wg_ref[0, 0].astype(BF16)
        wub[...] = wu_ref[0, 0].astype(BF16)
        wdb[...] = wd_ref[0, 0].astype(BF16)

    def compute_block(phase):
        xs = ys = phase
        wait_gather(xs)
        wait_scatter(ys)
        nxt = jnp.minimum(i + 2, nb - 1) * rows
        prv = i * rows
        for r in range(rows):
            gather_copy(src_ref[nxt + r], r, (xs + 2) % nx).start()
            scatter_copy(dst_ref[prv + r], r, (ys - 1) % nx).start()
        xb = _slabs_to_rows(xbufs[xs], rows, d, vs).astype(BF16)
        hg = jnp.dot(xb, wgb[...], preferred_element_type=F32)
        hu = jnp.dot(xb, wub[...], preferred_element_type=F32)
        hh = (jax.nn.silu(hg) * hu).astype(BF16)
        _rows_to_slabs(ybufs[ys], jnp.dot(hh, wdb[...], preferred_element_type=F32), vs)

    def drain_block(ys):
        def issue(r, carry):
            scatter_copy(dst_ref[i * rows + r], r, ys).start()
            return carry
        lax.fori_loop(0, rows, issue, 0, unroll=8)
        wait_scatter(ys)

    for phase in range(nx):
        pl.when((i < used) & (i % nx == phase))(functools.partial(compute_block, phase))

    @pl.when(i == used)
    def _():
        wait_gather(i % nx)
        wait_gather((i + 1) % nx)
        for t in range(nx - 1):
            wait_scatter((i + t) % nx)

    for phase in range(nx):
        pl.when((i == used) & (i >= 1) & ((i - 1) % nx == phase))(functools.partial(drain_block, phase))


def _moe(slot_src, slot_dst, blk_exp, used, x1s, w_gate, w_up, w_down, n_out, l):
    d, de = w_gate.shape[2:]
    slab = d // LANES
    nb = blk_exp.shape[0]
    rows = slot_src.shape[0] // nb
    blk = lambda i, src, dst, exp, used: (l, exp[jnp.minimum(i, nb - 1)], 0, 0)
    return pl.pallas_call(
        _moe_kernel,
        out_shape=jax.ShapeDtypeStruct((n_out * slab, LANES), F32),
        grid_spec=pltpu.PrefetchScalarGridSpec(
            num_scalar_prefetch=4,
            grid=(nb + 1,),
            in_specs=[pl.BlockSpec(memory_space=pl.ANY),
                      pl.BlockSpec((1, 1, d, de), blk), pl.BlockSpec((1, 1, d, de), blk),
                      pl.BlockSpec((1, 1, de, d), blk)],
            out_specs=pl.BlockSpec(memory_space=pl.ANY),
            scratch_shapes=[pltpu.VMEM((rows * MOE_VMEM_STRIDE, LANES), F32)] * (2 * MOE_RING) + [
                            pltpu.VMEM((d, de), BF16), pltpu.VMEM((d, de), BF16), pltpu.VMEM((de, d), BF16),
                            pltpu.SemaphoreType.DMA((MOE_RING,)), pltpu.SemaphoreType.DMA((MOE_RING + 1,))]),
        compiler_params=_params("arbitrary"),
        name="moe_experts",
    )(slot_src, slot_dst, blk_exp, used, x1s, w_gate, w_up, w_down)


def _combine_kernel(y0_ref, y1_ref, x_ref, rt_ref, g_ref, b_ref, o_ref):
    tm, d = x_ref.shape
    slab = d // LANES
    rt = rt_ref[...]
    ffn = _slabs_to_rows(y0_ref, tm, d, slab) * rt[:, 0:1] + _slabs_to_rows(y1_ref, tm, d, slab) * rt[:, 1:2]
    o_ref[...] = _layer_norm(DEEPNORM_ALPHA * x_ref[...] + ffn, g_ref[...], b_ref[...])


def _combine(y_tok, x1, route, ln_g, ln_b):
    n, d = x1.shape
    tm = min(n, TM_COMB)
    slab = d // LANES
    return pl.pallas_call(
        _combine_kernel,
        out_shape=jax.ShapeDtypeStruct((n, d), F32),
        grid=(n // tm,),
        in_specs=[pl.BlockSpec((tm * slab, LANES), lambda i: (i, 0)),
                  pl.BlockSpec((tm * slab, LANES), lambda i: (i + n // tm, 0)),
                  pl.BlockSpec((tm, d), lambda i: (i, 0)),
                  pl.BlockSpec((tm, LANES), lambda i: (i, 0)),
                  pl.BlockSpec((1, d), lambda i: (0, 0)),
                  pl.BlockSpec((1, d), lambda i: (0, 0))],
        out_specs=pl.BlockSpec((tm, d), lambda i: (i, 0)),
        compiler_params=_params("parallel"),
        name="moe_combine_ln",
    )(y_tok, y_tok, x1, route, ln_g[None, :], ln_b[None, :])


def _slots_kernel(dest_ref, src0_hbm, dst0_hbm, src_ref, dst_ref, sem):
    fills = [pltpu.make_async_copy(src0_hbm, src_ref, sem.at[0]), pltpu.make_async_copy(dst0_hbm, dst_ref, sem.at[1])]
    for f in fills:
        f.start()
    for f in fills:
        f.wait()
    nk = dest_ref.shape[0]
    n = nk // TOP_K
    lead = dst_ref.shape[0] - src_ref.shape[0]

    batch = 16

    def place(b, carry):
        t0 = b * (batch // TOP_K)
        slots = [dest_ref[b * batch + u] for u in range(batch)]
        for u, d in enumerate(slots):
            t = t0 + u // TOP_K
            src_ref[d] = t
            dst_ref[lead + d] = (u % TOP_K) * n + t
        return carry

    lax.fori_loop(0, nk // batch, place, 0)


def _slot_tables(dest, src0, dst0):
    smem = pl.BlockSpec(memory_space=pltpu.SMEM)
    hbm = pl.BlockSpec(memory_space=pl.ANY)
    return pl.pallas_call(
        _slots_kernel,
        out_shape=(jax.ShapeDtypeStruct(src0.shape, jnp.int32), jax.ShapeDtypeStruct(dst0.shape, jnp.int32)),
        grid_spec=pltpu.PrefetchScalarGridSpec(
            num_scalar_prefetch=1, grid=(1,), in_specs=[hbm, hbm], out_specs=(smem, smem),
            scratch_shapes=[pltpu.SemaphoreType.DMA((2,))]),
        name="moe_slot_tables",
    )(dest, src0, dst0)


def _dispatch_plan(route, counts, rows):
    n = route.shape[0]
    nk = n * TOP_K
    nb = (nk + rows - 1) // rows + N_EXPERTS
    counts = counts[0, :N_EXPERTS].astype(jnp.int32)
    padded = ((counts + rows - 1) // rows) * rows
    pad_end = jnp.cumsum(padded)
    pad_start = pad_end - padded
    expert = route[:, 2:2 + TOP_K].astype(jnp.int32)
    rank = route[:, 2 + TOP_K:2 + 2 * TOP_K].astype(jnp.int32)
    dest = (pad_start[expert] + rank).reshape(nk)
    slot = jnp.arange(nb * rows, dtype=jnp.int32)
    dump = nk + rows + ((slot // rows) % MOE_RING) * rows + slot % rows
    slot_src, slot_dst = _slot_tables(dest, jnp.zeros((nb * rows,), jnp.int32),
                                      jnp.concatenate([nk + jnp.arange(rows, dtype=jnp.int32), dump]))
    blk_exp = jnp.minimum(
        jnp.sum((pad_end[None, :] <= (jnp.arange(nb, dtype=jnp.int32) * rows)[:, None]).astype(jnp.int32), axis=1),
        N_EXPERTS - 1).astype(jnp.int32)
    used = (pad_end[-1] // rows).astype(jnp.int32).reshape(1)
    return slot_src, slot_dst, blk_exp, used, nk + (MOE_RING + 1) * rows


def _pad_q_heads(w_uq):
    lead = w_uq.shape[:-1]
    w = w_uq.reshape(*lead, MLA_HEADS, MLA_QK_DIM)
    w = jnp.pad(w, [(0, 0)] * len(lead) + [(0, 0), (0, Q_HEAD_PAD - MLA_QK_DIM)])
    return w.reshape(*lead, MLA_HEADS * Q_HEAD_PAD)


def kernel(x, positions, w_in, sgu_ln_g, sgu_ln_b, sgu_ws, sgu_b, hgrn_lb_logits, hgrn_norm_g, mla_qn_g, mla_w_uq, mla_kvn_g, mla_w_ukv, w_out, ln1_g, ln1_b, router_group_w, router_group_b, router_expert_w, router_expert_b, expert_w_gate, expert_w_up, expert_w_down, ln2_g, ln2_b):
    B, S, D = x.shape
    n = B * S
    depth = w_in.shape[0]
    tab_c, tab_s = _rope_tables(positions)
    lb_cum = jnp.cumsum(jax.nn.softmax(hgrn_lb_logits.astype(F32), axis=0), axis=0)
    lower_bounds = lb_cum - lb_cum[0:1]
    xt = x.reshape(n, D)
    w_in_b = jnp.pad(w_in, ((0, 0), (0, 0), (0, D_IN_PAD - D_IN))).astype(BF16)
    w_uq_b = _pad_q_heads(mla_w_uq).astype(BF16)
    w_ukv_b = mla_w_ukv.astype(BF16)
    w_out_b = w_out.astype(BF16)
    for l in range(depth):
        h = _inproj(xt, w_in_b, l)
        y_a = _sgu(h, sgu_ln_g[l], sgu_ln_b[l], sgu_ws[l], sgu_b[l])
        y_b = _hgrn(h, lower_bounds[l], hgrn_norm_g[l], B)
        q, kn, v, kr = _mla_up(h, tab_c, tab_s, mla_qn_g[l], w_uq_b, mla_kvn_g[l], w_ukv_b, l)
        y_c = _attention(q, kn, kr, v, B)
        wr = jnp.pad(jnp.concatenate([router_expert_w[l], router_group_w[l]], axis=1),
                     ((0, 0), (0, LANES - N_EXPERTS - N_GROUPS)))
        br = jnp.pad(jnp.concatenate([router_expert_b[l], router_group_b[l]]), (0, LANES - N_EXPERTS - N_GROUPS))
        x1, x1s, route, counts = _outproj(y_a, y_b, y_c, xt, w_out_b, ln1_g[l], ln1_b[l], wr, br[None, :], l)
        slot_src, slot_dst, blk_exp, used, n_out = _dispatch_plan(route, counts, MOE_ROWS)
        y_tok = _moe(slot_src, slot_dst, blk_exp, used, x1s, expert_w_gate, expert_w_up, expert_w_down, n_out, l)
        xt = _combine(y_tok, x1, route, ln2_g[l], ln2_b[l])
    return xt.reshape(B, S, D)
```
